```python
import math
import jax
import jax.numpy as jnp
from jax import lax
import numpy as np

D_MODEL = 1024
BATCH = 4
SEQ = 4096
DEPTH = 4

N_HEADS_RWKV = 8
HEAD_RWKV = 64
RWKV_WIDTH = N_HEADS_RWKV * HEAD_RWKV
LORA_W = 64
LORA_A = 64
LORA_G = 128
RWKV_GN_EPS = 64e-5

N_HEADS_NSA = 8
N_KV_GROUPS = 2
HEADS_PER_GROUP = N_HEADS_NSA // N_KV_GROUPS
HEAD_NSA = 64
NSA_Q_WIDTH = N_HEADS_NSA * HEAD_NSA
NSA_KV_WIDTH = N_KV_GROUPS * HEAD_NSA
CMP_BLOCK = 32
CMP_STRIDE = 16
CMP_HIDDEN = 128
SEL_BLOCK = 64
N_SELECT = 16
WINDOW = 512
Q_BLOCK = 128
N_BAND = WINDOW // Q_BLOCK + 1
NEG_INF = -1e30
FORCED_SCORE = 1e4

NUM_BUCKETS = 32
MAX_DISTANCE = 1024

N_GROUPS = 4
EXPERTS_PER_GROUP = 8
N_EXPERTS = N_GROUPS * EXPERTS_PER_GROUP
TOP_K_INNER = 2
D_EXPERT = 512
MOE_BLOCK = 128

ALPHA = (2 * DEPTH) ** 0.25
BETA = (8 * DEPTH) ** -0.25
LN_EPS = 1e-5

SHIFT_SPLITS = (RWKV_WIDTH, RWKV_WIDTH, RWKV_WIDTH, LORA_W, LORA_A, LORA_G)
SHIFT_WIDTH = 3 * RWKV_WIDTH + LORA_W + LORA_A + LORA_G
REST_SPLITS = (NSA_Q_WIDTH,) + (NSA_KV_WIDTH,) * 6 + (3 * N_HEADS_NSA, D_MODEL, D_MODEL)
IN_WIDTH = SHIFT_WIDTH + NSA_Q_WIDTH + 6 * NSA_KV_WIDTH + 3 * N_HEADS_NSA + 2 * D_MODEL

kernel_name = 'hybrid_rwkv7_nsa_hmoe_deepnorm'


def _split_cols(z, sizes):
    return jnp.split(z, np.cumsum(sizes)[:-1].tolist(), axis=-1)


def _layer_norm(x, g, b, eps=LN_EPS):
    xf = x.astype(jnp.float32)
    mu = jnp.mean(xf, axis=-1, keepdims=True)
    var = jnp.mean(jnp.square(xf - mu), axis=-1, keepdims=True)
    return ((xf - mu) * lax.rsqrt(var + eps) * g + b).astype(x.dtype)


def _t5_bucket(dist):
    n = jnp.maximum(dist, 0)
    max_exact = NUM_BUCKETS // 2
    nf = jnp.maximum(n, 1).astype(jnp.float32)
    large = max_exact + (jnp.log(nf / max_exact) / math.log(MAX_DISTANCE / max_exact)
                         * (NUM_BUCKETS - max_exact)).astype(jnp.int32)
    large = jnp.minimum(large, NUM_BUCKETS - 1)
    return jnp.where(n < max_exact, n, large)


def _bias_from_buckets(rel_bias, bucket):
    b = jnp.moveaxis(rel_bias.astype(jnp.float32)[bucket], -1, 0)
    return b.reshape((N_KV_GROUPS, HEADS_PER_GROUP) + bucket.shape)


def _nsa_positional(rel_bias, seq):
    n_cmp = seq // CMP_STRIDE - CMP_BLOCK // CMP_STRIDE + 1
    n_sel_blocks = seq // SEL_BLOCK
    t = jnp.arange(seq)[:, None]
    c = jnp.arange(n_cmp)[None, :]
    d_cmp = t - (c * CMP_STRIDE + CMP_BLOCK - 1)
    mask_cmp = d_cmp >= 0
    bias_cmp = _bias_from_buckets(rel_bias, _t5_bucket(d_cmp))
    qo = jnp.arange(Q_BLOCK)[:, None]
    m = jnp.arange(WINDOW + Q_BLOCK)[None, :]
    d_win = qo + WINDOW - m
    bias_win = _bias_from_buckets(rel_bias, _t5_bucket(d_win))
    blk = jnp.arange(seq // Q_BLOCK)[:, None, None]
    mask_win = (d_win >= 0) & (d_win < WINDOW) & (blk * Q_BLOCK - WINDOW + m >= 0)
    cs = jnp.arange(n_cmp)[:, None] * CMP_STRIDE
    ss = jnp.arange(n_sel_blocks)[None, :] * SEL_BLOCK
    overlap = jnp.clip(jnp.minimum(cs + CMP_BLOCK, ss + SEL_BLOCK) - jnp.maximum(cs, ss), 0, None)
    cmp_to_sel = overlap.astype(jnp.float32) / CMP_BLOCK
    return bias_cmp, mask_cmp, bias_win, mask_win, cmp_to_sel


def _rwkv7_time_mix(r, k, v, wl, al, gl, w0, w2, a0, a2, g2, k_k, k_a, r_k, ln_g, ln_b):
    B, S, C = r.shape
    H, N = N_HEADS_RWKV, HEAD_RWKV
    f32 = jnp.float32
    logw = -jax.nn.softplus(-(w0 + jnp.tanh(wl) @ w2).astype(f32)) - 0.5
    decay = jnp.exp(-jnp.exp(logw))
    a = jax.nn.sigmoid((a0 + al @ a2).astype(f32))
    g = jax.nn.sigmoid(gl) @ g2
    r, k, v = r.astype(f32), k.astype(f32), v.astype(f32)
    heads = lambda t: t.reshape(B, S, H, N)
    kk = heads(k * k_k)
    kk = kk / jnp.maximum(jnp.linalg.norm(kk, axis=-1, keepdims=True), 1e-12)
    k = k * (1.0 + (a - 1.0) * k_a)
    rh, kh, vh, ah, wh = heads(r), heads(k), heads(v), heads(a), heads(decay)

    def step(state, inp):
        r_t, w_t, k_t, v_t, kk_t, a_t = inp
        sa = jnp.einsum('bhvk,bhk->bhv', state, -kk_t)
        state = (state * w_t[:, :, None, :] + sa[..., None] * (kk_t * a_t)[:, :, None, :]
                 + v_t[..., None] * k_t[:, :, None, :])
        return state, jnp.einsum('bhvk,bhk->bhv', state, r_t)

    xs = tuple(jnp.moveaxis(t, 1, 0) for t in (rh, wh, kh, vh, kk, ah))
    _, y = lax.scan(step, jnp.zeros((B, H, N, N), f32), xs)
    y = jnp.moveaxis(y, 0, 1)
    mu = jnp.mean(y, axis=-1, keepdims=True)
    var = jnp.mean(jnp.square(y - mu), axis=-1, keepdims=True)
    y = ((y - mu) * lax.rsqrt(var + RWKV_GN_EPS)).reshape(B, S, C) * ln_g + ln_b
    bonus = jnp.sum(rh * kh * r_k, axis=-1, keepdims=True) * vh
    return (y + bonus.reshape(B, S, C)) * g


def _compress(t, pe, w1, w2):
    B, S, G, Dh = t.shape
    rep = CMP_BLOCK // CMP_STRIDE
    nc = S // CMP_STRIDE - rep + 1
    sub = t.reshape(B, S // CMP_STRIDE, CMP_STRIDE, G, Dh)
    blk = jnp.concatenate([sub[:, j:j + nc] for j in range(rep)], axis=2)
    blk = blk + pe[:, None, :]
    blk = blk.transpose(0, 1, 3, 2, 4).reshape(B, nc, G, CMP_BLOCK * Dh)
    out = jax.nn.gelu(blk @ w1) @ w2
    return out.transpose(0, 2, 1, 3)


def _nsa_attention(q, k_cmp, v_cmp, k_slc, v_slc, k_win, v_win, gate_logits,
                   pe_k, w1_k, w2_k, pe_v, w1_v, w2_v, rel_bias, pos):
    bias_cmp, mask_cmp, bias_win, mask_win, cmp_to_sel = pos
    B, S, _ = q.shape
    G, Hg, Dh = N_KV_GROUPS, HEADS_PER_GROUP, HEAD_NSA
    f32 = jnp.float32
    scale = HEAD_NSA ** -0.5
    qh = q.reshape(B, S, G, Hg, Dh).transpose(0, 2, 3, 1, 4)
    kv = lambda t: t.reshape(B, S, G, Dh).transpose(0, 2, 1, 3)

    kc = _compress(k_cmp.reshape(B, S, G, Dh), pe_k, w1_k, w2_k)
    vc = _compress(v_cmp.reshape(B, S, G, Dh), pe_v, w1_v, w2_v)
    lg = jnp.einsum('bghqd,bgcd->bghqc', qh, kc).astype(f32) * scale + bias_cmp
    p_cmp = jax.nn.softmax(jnp.where(mask_cmp, lg, NEG_INF), axis=-1) * mask_cmp
    o_cmp = jnp.einsum('bghqc,bgcd->bghqd', p_cmp, vc.astype(f32))

    n_sel_blocks = S // SEL_BLOCK
    n_sel = min(N_SELECT, n_sel_blocks)
    score = jnp.einsum('bgqc,cj->bgqj', jnp.sum(p_cmp, axis=2), cmp_to_sel)
    t = jnp.arange(S)[:, None]
    j = jnp.arange(n_sel_blocks)[None, :]
    cur = t // SEL_BLOCK
    forced = (j == 0) | (j == cur) | (j == cur - 1)
    score = jnp.where(forced, FORCED_SCORE, jnp.where(j <= cur, score, -1.0))
    _, sel_idx = lax.top_k(score, n_sel)
    ks_b = kv(k_slc).reshape(B, G, n_sel_blocks, SEL_BLOCK, Dh)
    vs_b = kv(v_slc).reshape(B, G, n_sel_blocks, SEL_BLOCK, Dh)
    table = rel_bias.astype(f32).reshape(NUM_BUCKETS, G, Hg).transpose(1, 2, 0)
    bi = jnp.arange(B)[:, None, None, None]
    gi = jnp.arange(G)[None, :, None, None]
    gi6 = jnp.arange(G)[None, :, None, None, None, None]
    hi6 = jnp.arange(Hg)[None, None, :, None, None, None]

    def sel_chunk(c):
        t0 = c * Q_BLOCK
        qc = lax.dynamic_slice_in_dim(qh, t0, Q_BLOCK, axis=3)
        ic = lax.dynamic_slice_in_dim(sel_idx, t0, Q_BLOCK, axis=2)
        kg = ks_b[bi, gi, ic]
        vg = vs_b[bi, gi, ic]
        kpos = ic[..., None] * SEL_BLOCK + jnp.arange(SEL_BLOCK)
        dist = (t0 + jnp.arange(Q_BLOCK))[:, None, None] - kpos
        bias = table[gi6, hi6, _t5_bucket(dist)[:, :, None]]
        lgs = jnp.einsum('bghqd,bgqnld->bghqnl', qc, kg).astype(f32) * scale + bias
        lgs = jnp.where((dist >= 0)[:, :, None], lgs, NEG_INF)
        p = jax.nn.softmax(lgs.reshape(B, G, Hg, Q_BLOCK, n_sel * SEL_BLOCK), axis=-1).reshape(lgs.shape)
        return jnp.einsum('bghqnl,bgqnld->bghqd', p, vg.astype(f32))

    o_slc = lax.map(sel_chunk, jnp.arange(S // Q_BLOCK))
    o_slc = jnp.moveaxis(o_slc, 0, 3).reshape(B, G, Hg, S, Dh)

    nb = S // Q_BLOCK

    def band(t):
        tp = jnp.pad(kv(t), ((0, 0), (0, 0), (WINDOW, 0), (0, 0))).reshape(B, G, nb + N_BAND - 1, Q_BLOCK, Dh)
        return jnp.concatenate([tp[:, :, o:o + nb] for o in range(N_BAND)], axis=3)

    kw_b, vw_b = band(k_win), band(v_win)
    qb = qh.reshape(B, G, Hg, nb, Q_BLOCK, Dh)
    lgw = jnp.einsum('bghnqd,bgnkd->bghnqk', qb, kw_b).astype(f32) * scale + bias_win[:, :, None]
    pw = jax.nn.softmax(jnp.where(mask_win, lgw, NEG_INF), axis=-1)
    o_win = jnp.einsum('bghnqk,bgnkd->bghnqd', pw, vw_b.astype(f32)).reshape(B, G, Hg, S, Dh)

    gates = jax.nn.sigmoid(gate_logits.astype(f32)).reshape(B, S, G, Hg, 3).transpose(0, 2, 3, 1, 4)
    o = gates[..., 0:1] * o_cmp + gates[..., 1:2] * o_slc + gates[..., 2:3] * o_win
    return o.transpose(0, 3, 1, 2, 4).reshape(B, S, NSA_Q_WIDTH)


def _token_mixer(x, w_in, shift_mu, rw_w0, rw_w2, rw_a0, rw_a2, rw_g2, rw_kk, rw_ka, rw_rk,
                 rw_ln_g, rw_ln_b, cmp_pe_k, cmp_w1_k, cmp_w2_k, cmp_pe_v, cmp_w1_v, cmp_w2_v,
                 w_up_rwkv, w_up_nsa, w_out, rel_bias, pos):
    z = x @ w_in
    z_shift, z_rest = z[..., :SHIFT_WIDTH], z[..., SHIFT_WIDTH:]
    z_prev = jnp.pad(z_shift, ((0, 0), (1, 0), (0, 0)))[:, :-1]
    z_shift = z_shift + (z_prev - z_shift) * shift_mu
    r, k, v, wl, al, gl = _split_cols(z_shift, SHIFT_SPLITS)
    q, kc, vc, ks, vs, kw, vw, nsa_g, g_rw, g_nsa = _split_cols(z_rest, REST_SPLITS)
    y_rw = _rwkv7_time_mix(r, k, v, wl, al, gl, rw_w0, rw_w2, rw_a0, rw_a2, rw_g2,
                           rw_kk, rw_ka, rw_rk, rw_ln_g, rw_ln_b).astype(x.dtype)
    y_nsa = _nsa_attention(q, kc, vc, ks, vs, kw, vw, nsa_g, cmp_pe_k, cmp_w1_k, cmp_w2_k,
                           cmp_pe_v, cmp_w1_v, cmp_w2_v, rel_bias, pos).astype(x.dtype)
    merged = jax.nn.sigmoid(g_rw) * (y_rw @ w_up_rwkv) + jax.nn.sigmoid(g_nsa) * (y_nsa @ w_up_nsa)
    return merged @ w_out


def _hier_moe(x, wg, bg, we, be, w1, w3, w2):
    B, S, D = x.shape
    N = B * S
    f32 = jnp.float32
    xf = x.reshape(N, D)
    g_prob = jax.nn.softmax((xf @ wg + bg).astype(f32), axis=-1)
    grp = jnp.argmax(g_prob, axis=-1)
    p_grp = jnp.take_along_axis(g_prob, grp[:, None], axis=1)[:, 0]
    e_logits = (xf @ we + be).astype(f32).reshape(N, N_GROUPS, EXPERTS_PER_GROUP)
    e_logits = jnp.take_along_axis(e_logits, grp[:, None, None], axis=1)[:, 0]
    top_v, top_i = lax.top_k(e_logits, TOP_K_INNER)
    top_w = jax.nn.softmax(top_v, axis=-1) * p_grp[:, None]
    eid = (grp[:, None] * EXPERTS_PER_GROUP + top_i).reshape(-1).astype(jnp.int32)
    slot_w = top_w.reshape(-1)
    slot_tok = jnp.repeat(jnp.arange(N, dtype=jnp.int32), TOP_K_INNER)
    n_slots = N * TOP_K_INNER
    order = jnp.argsort(eid)
    eid_s = eid[order]
    counts = jax.ops.segment_sum(jnp.ones_like(eid), eid, num_segments=N_EXPERTS)
    starts = jnp.cumsum(counts) - counts
    pcounts = (counts + MOE_BLOCK - 1) // MOE_BLOCK * MOE_BLOCK
    pends = jnp.cumsum(pcounts)
    pstarts = pends - pcounts
    dest = pstarts[eid_s] + (jnp.arange(n_slots) - starts[eid_s])
    n_rows = n_slots + N_EXPERTS * MOE_BLOCK
    n_blk = n_rows // MOE_BLOCK
    row_tok = jnp.full((n_rows,), N, jnp.int32).at[dest].set(slot_tok[order])
    row_w = jnp.zeros((n_rows,), f32).at[dest].set(slot_w[order])
    blk_exp = jnp.minimum(jnp.sum(jnp.arange(n_blk)[:, None] * MOE_BLOCK >= pends[None, :], axis=1),
                          N_EXPERTS - 1)
    xpad = jnp.concatenate([xf, jnp.zeros((1, D), xf.dtype)], axis=0)
    xs = xpad[row_tok].reshape(n_blk, MOE_BLOCK, D)

    def expert_block(args):
        xb, e = args
        h = jax.nn.silu(xb @ w1[e]) * (xb @ w3[e])
        return h @ w2[e]

    ys = lax.map(expert_block, (xs, blk_exp)).reshape(n_rows, D)
    out = jnp.zeros((N + 1, D), f32).at[row_tok].add(ys.astype(f32) * row_w[:, None])
    return out[:N].reshape(B, S, D).astype(x.dtype)


def setup_inputs(seed: int = 0) -> dict:
    key = jax.random.key(seed)
    keys = iter(jax.random.split(key, 48))
    f32 = jnp.float32

    def nrm(shape, scale):
        return jax.random.normal(next(keys), shape, f32) * scale

    L, D, C = DEPTH, D_MODEL, RWKV_WIDTH
    return {
        'x': nrm((BATCH, SEQ, D), 1.0),
        'rel_bias': nrm((NUM_BUCKETS, N_HEADS_NSA), 0.3),
        'w_in': nrm((L, D, IN_WIDTH), D ** -0.5),
        'shift_mu': jax.random.uniform(next(keys), (L, SHIFT_WIDTH), f32),
        'rw_w0': jnp.linspace(-6.0, -1.0, C, dtype=f32)[None, :] + nrm((L, C), 0.1),
        'rw_w2': nrm((L, LORA_W, C), 0.1 * LORA_W ** -0.5),
        'rw_a0': nrm((L, C), 0.1),
        'rw_a2': nrm((L, LORA_A, C), 0.3 * LORA_A ** -0.5),
        'rw_g2': nrm((L, LORA_G, C), LORA_G ** -0.5),
        'rw_kk': 0.85 + nrm((L, C), 0.05),
        'rw_ka': 1.0 + nrm((L, C), 0.05),
        'rw_rk': nrm((L, N_HEADS_RWKV, HEAD_RWKV), 0.1),
        'rw_ln_g': 1.0 + nrm((L, C), 0.02),
        'rw_ln_b': nrm((L, C), 0.02),
        'cmp_pe_k': nrm((L, CMP_BLOCK, HEAD_NSA), 0.1),
        'cmp_w1_k': nrm((L, CMP_BLOCK * HEAD_NSA, CMP_HIDDEN), (CMP_BLOCK * HEAD_NSA) ** -0.5),
        'cmp_w2_k': nrm((L, CMP_HIDDEN, HEAD_NSA), CMP_HIDDEN ** -0.5),
        'cmp_pe_v': nrm((L, CMP_BLOCK, HEAD_NSA), 0.1),
        'cmp_w1_v': nrm((L, CMP_BLOCK * HEAD_NSA, CMP_HIDDEN), (CMP_BLOCK * HEAD_NSA) ** -0.5),
        'cmp_w2_v': nrm((L, CMP_HIDDEN, HEAD_NSA), CMP_HIDDEN ** -0.5),
        'w_up_rwkv': nrm((L, C, D), C ** -0.5),
        'w_up_nsa': nrm((L, NSA_Q_WIDTH, D), NSA_Q_WIDTH ** -0.5),
        'w_out': nrm((L, D, D), BETA * D ** -0.5),
        'ln1_g': 1.0 + nrm((L, D), 0.02),
        'ln1_b': nrm((L, D), 0.02),
        'router_group_w': nrm((L, D, N_GROUPS), D ** -0.5),
        'router_group_b': nrm((L, N_GROUPS), 0.01),
        'router_expert_w': nrm((L, D, N_EXPERTS), D ** -0.5),
        'router_expert_b': nrm((L, N_EXPERTS), 0.01),
        'exp_w1': nrm((L, N_EXPERTS, D, D_EXPERT), D ** -0.5),
        'exp_w3': nrm((L, N_EXPERTS, D, D_EXPERT), D ** -0.5),
        'exp_w2': nrm((L, N_EXPERTS, D_EXPERT, D), BETA * D_EXPERT ** -0.5),
        'ln2_g': 1.0 + nrm((L, D), 0.02),
        'ln2_b': nrm((L, D), 0.02),
    }


def reference(x, rel_bias, w_in, shift_mu, rw_w0, rw_w2, rw_a0, rw_a2, rw_g2, rw_kk, rw_ka, rw_rk,
              rw_ln_g, rw_ln_b, cmp_pe_k, cmp_w1_k, cmp_w2_k, cmp_pe_v, cmp_w1_v, cmp_w2_v,
              w_up_rwkv, w_up_nsa, w_out, ln1_g, ln1_b, router_group_w, router_group_b,
              router_expert_w, router_expert_b, exp_w1, exp_w3, exp_w2, ln2_g, ln2_b):
    pos = _nsa_positional(rel_bias, x.shape[1])
    for l in range(DEPTH):
        h = _token_mixer(x, w_in[l], shift_mu[l], rw_w0[l], rw_w2[l], rw_a0[l], rw_a2[l], rw_g2[l],
                         rw_kk[l], rw_ka[l], rw_rk[l], rw_ln_g[l], rw_ln_b[l],
                         cmp_pe_k[l], cmp_w1_k[l], cmp_w2_k[l], cmp_pe_v[l], cmp_w1_v[l], cmp_w2_v[l],
                         w_up_rwkv[l], w_up_nsa[l], w_out[l], rel_bias, pos)
        x = _layer_norm(ALPHA * x + h, ln1_g[l], ln1_b[l])
        h = _hier_moe(x, router_group_w[l], router_group_b[l], router_expert_w[l], router_expert_b[l],
                      exp_w1[l], exp_w3[l], exp_w2[l])
        x = _layer_norm(ALPHA * x + h, ln2_g[l], ln2_b[l])
    return x
```

```python
import functools
import math

import jax
import jax.numpy as jnp
import numpy as np
from jax import lax
from jax.experimental import pallas as pl
from jax.experimental.pallas import tpu as pltpu

D_MODEL = 1024
DEPTH = 4
N_HEADS_RWKV = 8
HEAD_RWKV = 64
RWKV_WIDTH = N_HEADS_RWKV * HEAD_RWKV
LORA_W = 64
LORA_A = 64
LORA_G = 128
RWKV_GN_EPS = 64e-5
N_HEADS_NSA = 8
N_KV_GROUPS = 2
HEADS_PER_GROUP = N_HEADS_NSA // N_KV_GROUPS
HEAD_NSA = 64
NSA_Q_WIDTH = N_HEADS_NSA * HEAD_NSA
NSA_KV_WIDTH = N_KV_GROUPS * HEAD_NSA
CMP_BLOCK = 32
CMP_STRIDE = 16
CMP_HIDDEN = 128
SEL_BLOCK = 64
N_SELECT = 16
WINDOW = 512
Q_BLOCK = 128
N_BAND = WINDOW // Q_BLOCK + 1
NEG_INF = -1e30
FORCED_SCORE = 1e4
NUM_BUCKETS = 32
MAX_DISTANCE = 1024
N_GROUPS = 4
EXPERTS_PER_GROUP = 8
N_EXPERTS = N_GROUPS * EXPERTS_PER_GROUP
TOP_K_INNER = 2
D_EXPERT = 512
MOE_BLOCK = 128
ALPHA = (2 * DEPTH) ** 0.25
LN_EPS = 1e-5
SHIFT_SPLITS = (RWKV_WIDTH, RWKV_WIDTH, RWKV_WIDTH, LORA_W, LORA_A, LORA_G)
SHIFT_WIDTH = 3 * RWKV_WIDTH + LORA_W + LORA_A + LORA_G
REST_SPLITS = (NSA_Q_WIDTH,) + (NSA_KV_WIDTH,) * 6 + (3 * N_HEADS_NSA, D_MODEL, D_MODEL)


def _mm_body(x_ref, w_ref, o_ref):
    o_ref[...] = jnp.dot(x_ref[...].astype(jnp.bfloat16), w_ref[...].astype(jnp.bfloat16),
                         preferred_element_type=jnp.float32)


def _matmul(x, w, tm=512, tn=512):
    m, k = x.shape
    n = w.shape[1]
    assert m % tm == 0 and n % tn == 0
    return pl.pallas_call(
        _mm_body,
        grid=(n // tn, m // tm),
        in_specs=[pl.BlockSpec((tm, k), lambda j, i: (i, 0)),
                  pl.BlockSpec((k, tn), lambda j, i: (0, j))],
        out_specs=pl.BlockSpec((tm, tn), lambda j, i: (i, j)),
        out_shape=jax.ShapeDtypeStruct((m, n), jnp.float32),
        compiler_params=pltpu.CompilerParams(dimension_semantics=("arbitrary", "arbitrary")),
        name="matmul",
    )(x, w)


def _split_cols(z, sizes):
    return jnp.split(z, np.cumsum(sizes)[:-1].tolist(), axis=-1)


def _layer_norm(x, g, b, eps=LN_EPS):
    mu = jnp.mean(x, axis=-1, keepdims=True)
    var = jnp.mean(jnp.square(x - mu), axis=-1, keepdims=True)
    return (x - mu) * lax.rsqrt(var + eps) * g + b


def _t5_bucket(dist):
    n = jnp.maximum(dist, 0)
    max_exact = NUM_BUCKETS // 2
    nf = jnp.maximum(n, 1).astype(jnp.float32)
    large = max_exact + (jnp.log(nf / max_exact) / math.log(MAX_DISTANCE / max_exact)
                         * (NUM_BUCKETS - max_exact)).astype(jnp.int32)
    large = jnp.minimum(large, NUM_BUCKETS - 1)
    return jnp.where(n < max_exact, n, large)


def _bias_from_buckets(rel_bias, bucket):
    b = jnp.moveaxis(rel_bias.astype(jnp.float32)[bucket], -1, 0)
    return b.reshape((N_KV_GROUPS, HEADS_PER_GROUP) + bucket.shape)


def _nsa_positional(rel_bias, seq):
    n_cmp = seq // CMP_STRIDE - CMP_BLOCK // CMP_STRIDE + 1
    n_sel_blocks = seq // SEL_BLOCK
    t = jnp.arange(seq)[:, None]
    c = jnp.arange(n_cmp)[None, :]
    d_cmp = t - (c * CMP_STRIDE + CMP_BLOCK - 1)
    mask_cmp = d_cmp >= 0
    bias_cmp = _bias_from_buckets(rel_bias, _t5_bucket(d_cmp))
    qo = jnp.arange(Q_BLOCK)[:, None]
    m = jnp.arange(WINDOW + Q_BLOCK)[None, :]
    d_win = qo + WINDOW - m
    bias_win = _bias_from_buckets(rel_bias, _t5_bucket(d_win))
    blk = jnp.arange(seq // Q_BLOCK)[:, None, None]
    mask_win = (d_win >= 0) & (d_win < WINDOW) & (blk * Q_BLOCK - WINDOW + m >= 0)
    cs = jnp.arange(n_cmp)[:, None] * CMP_STRIDE
    ss = jnp.arange(n_sel_blocks)[None, :] * SEL_BLOCK
    overlap = jnp.clip(jnp.minimum(cs + CMP_BLOCK, ss + SEL_BLOCK) - jnp.maximum(cs, ss), 0, None)
    cmp_to_sel = overlap.astype(jnp.float32) / CMP_BLOCK
    return bias_cmp, mask_cmp, bias_win, mask_win, cmp_to_sel


def _rwkv7_time_mix(r, k, v, wl, al, gl, w0, w2, a0, a2, g2, k_k, k_a, r_k, ln_g, ln_b):
    B, S, C = r.shape
    H, N = N_HEADS_RWKV, HEAD_RWKV
    f32 = jnp.float32
    logw = -jax.nn.softplus(-(w0 + jnp.tanh(wl) @ w2).astype(f32)) - 0.5
    decay = jnp.exp(-jnp.exp(logw))
    a = jax.nn.sigmoid((a0 + al @ a2).astype(f32))
    g = jax.nn.sigmoid(gl) @ g2
    heads = lambda t: t.reshape(B, S, H, N)
    kk = heads(k * k_k)
    kk = kk / jnp.maximum(jnp.linalg.norm(kk, axis=-1, keepdims=True), 1e-12)
    k = k * (1.0 + (a - 1.0) * k_a)
    rh, kh, vh, ah, wh = heads(r), heads(k), heads(v), heads(a), heads(decay)

    def step(state, inp):
        r_t, w_t, k_t, v_t, kk_t, a_t = inp
        sa = jnp.einsum('bhvk,bhk->bhv', state, -kk_t)
        state = (state * w_t[:, :, None, :] + sa[..., None] * (kk_t * a_t)[:, :, None, :]
                 + v_t[..., None] * k_t[:, :, None, :])
        return state, jnp.einsum('bhvk,bhk->bhv', state, r_t)

    xs = tuple(jnp.moveaxis(t, 1, 0) for t in (rh, wh, kh, vh, kk, ah))
    _, y = lax.scan(step, jnp.zeros((B, H, N, N), f32), xs)
    y = jnp.moveaxis(y, 0, 1)
    mu = jnp.mean(y, axis=-1, keepdims=True)
    var = jnp.mean(jnp.square(y - mu), axis=-1, keepdims=True)
    y = ((y - mu) * lax.rsqrt(var + RWKV_GN_EPS)).reshape(B, S, C) * ln_g + ln_b
    bonus = jnp.sum(rh * kh * r_k, axis=-1, keepdims=True) * vh
    return (y + bonus.reshape(B, S, C)) * g


def _compress(t, pe, w1, w2):
    B, S, G, Dh = t.shape
    rep = CMP_BLOCK // CMP_STRIDE
    nc = S // CMP_STRIDE - rep + 1
    sub = t.reshape(B, S // CMP_STRIDE, CMP_STRIDE, G, Dh)
    blk = jnp.concatenate([sub[:, j:j + nc] for j in range(rep)], axis=2)
    blk = blk + pe[:, None, :]
    blk = blk.transpose(0, 1, 3, 2, 4).reshape(B, nc, G, CMP_BLOCK * Dh)
    out = jax.nn.gelu(blk @ w1) @ w2
    return out.transpose(0, 2, 1, 3)


def _nsa_attention(q, k_cmp, v_cmp, k_slc, v_slc, k_win, v_win, gate_logits,
                   pe_k, w1_k, w2_k, pe_v, w1_v, w2_v, rel_bias, pos):
    bias_cmp, mask_cmp, bias_win, mask_win, cmp_to_sel = pos
    B, S, _ = q.shape
    G, Hg, Dh = N_KV_GROUPS, HEADS_PER_GROUP, HEAD_NSA
    f32 = jnp.float32
    scale = HEAD_NSA ** -0.5
    qh = q.reshape(B, S, G, Hg, Dh).transpose(0, 2, 3, 1, 4)
    kv = lambda t: t.reshape(B, S, G, Dh).transpose(0, 2, 1, 3)
    kc = _compress(k_cmp.reshape(B, S, G, Dh), pe_k, w1_k, w2_k)
    vc = _compress(v_cmp.reshape(B, S, G, Dh), pe_v, w1_v, w2_v)
    lg = jnp.einsum('bghqd,bgcd->bghqc', qh, kc).astype(f32) * scale + bias_cmp
    p_cmp = jax.nn.softmax(jnp.where(mask_cmp, lg, NEG_INF), axis=-1) * mask_cmp
    o_cmp = jnp.einsum('bghqc,bgcd->bghqd', p_cmp, vc.astype(f32))
    n_sel_blocks = S // SEL_BLOCK
    n_sel = min(N_SELECT, n_sel_blocks)
    score = jnp.einsum('bgqc,cj->bgqj', jnp.sum(p_cmp, axis=2), cmp_to_sel)
    t = jnp.arange(S)[:, None]
    j = jnp.arange(n_sel_blocks)[None, :]
    cur = t // SEL_BLOCK
    forced = (j == 0) | (j == cur) | (j == cur - 1)
    score = jnp.where(forced, FORCED_SCORE, jnp.where(j <= cur, score, -1.0))
    _, sel_idx = lax.top_k(score, n_sel)
    ks_b = kv(k_slc).reshape(B, G, n_sel_blocks, SEL_BLOCK, Dh)
    vs_b = kv(v_slc).reshape(B, G, n_sel_blocks, SEL_BLOCK, Dh)
    table = rel_bias.astype(f32).reshape(NUM_BUCKETS, G, Hg).transpose(1, 2, 0)
    bi = jnp.arange(B)[:, None, None, None]
    gi = jnp.arange(G)[None, :, None, None]
    gi6 = jnp.arange(G)[None, :, None, None, None, None]
    hi6 = jnp.arange(Hg)[None, None, :, None, None, None]

    def sel_chunk(c):
        t0 = c * Q_BLOCK
        qc = lax.dynamic_slice_in_dim(qh, t0, Q_BLOCK, axis=3)
        ic = lax.dynamic_slice_in_dim(sel_idx, t0, Q_BLOCK, axis=2)
        kg = ks_b[bi, gi, ic]
        vg = vs_b[bi, gi, ic]
        kpos = ic[..., None] * SEL_BLOCK + jnp.arange(SEL_BLOCK)
        dist = (t0 + jnp.arange(Q_BLOCK))[:, None, None] - kpos
        bias = table[gi6, hi6, _t5_bucket(dist)[:, :, None]]
        lgs = jnp.einsum('bghqd,bgqnld->bghqnl', qc, kg).astype(f32) * scale + bias
        lgs = jnp.where((dist >= 0)[:, :, None], lgs, NEG_INF)
        p = jax.nn.softmax(lgs.reshape(B, G, Hg, Q_BLOCK, n_sel * SEL_BLOCK), axis=-1).reshape(lgs.shape)
        return jnp.einsum('bghqnl,bgqnld->bghqd', p, vg.astype(f32))

    o_slc = lax.map(sel_chunk, jnp.arange(S // Q_BLOCK))
    o_slc = jnp.moveaxis(o_slc, 0, 3).reshape(B, G, Hg, S, Dh)
    nb = S // Q_BLOCK

    def band(t):
        tp = jnp.pad(kv(t), ((0, 0), (0, 0), (WINDOW, 0), (0, 0))).reshape(B, G, nb + N_BAND - 1, Q_BLOCK, Dh)
        return jnp.concatenate([tp[:, :, o:o + nb] for o in range(N_BAND)], axis=3)

    kw_b, vw_b = band(k_win), band(v_win)
    qb = qh.reshape(B, G, Hg, nb, Q_BLOCK, Dh)
    lgw = jnp.einsum('bghnqd,bgnkd->bghnqk', qb, kw_b).astype(f32) * scale + bias_win[:, :, None]
    pw = jax.nn.softmax(jnp.where(mask_win, lgw, NEG_INF), axis=-1)
    o_win = jnp.einsum('bghnqk,bgnkd->bghnqd', pw, vw_b.astype(f32)).reshape(B, G, Hg, S, Dh)
    gates = jax.nn.sigmoid(gate_logits.astype(f32)).reshape(B, S, G, Hg, 3).transpose(0, 2, 3, 1, 4)
    o = gates[..., 0:1] * o_cmp + gates[..., 1:2] * o_slc + gates[..., 2:3] * o_win
    return o.transpose(0, 3, 1, 2, 4).reshape(B, S, NSA_Q_WIDTH)


def _token_mixer(x, w_in, shift_mu, rw_w0, rw_w2, rw_a0, rw_a2, rw_g2, rw_kk, rw_ka, rw_rk,
                 rw_ln_g, rw_ln_b, cmp_pe_k, cmp_w1_k, cmp_w2_k, cmp_pe_v, cmp_w1_v, cmp_w2_v,
                 w_up_rwkv, w_up_nsa, w_out, rel_bias, pos):
    B, S, D = x.shape
    n_in = w_in.shape[1]
    n_pad = (-n_in) % 512
    w_in_p = jnp.pad(w_in, ((0, 0), (0, n_pad)))
    z = _matmul(x.reshape(B * S, D), w_in_p)[:, :n_in].reshape(B, S, n_in)
    z_shift, z_rest = z[..., :SHIFT_WIDTH], z[..., SHIFT_WIDTH:]
    z_prev = jnp.pad(z_shift, ((0, 0), (1, 0), (0, 0)))[:, :-1]
    z_shift = z_shift + (z_prev - z_shift) * shift_mu
    r, k, v, wl, al, gl = _split_cols(z_shift, SHIFT_SPLITS)
    q, kc, vc, ks, vs, kw, vw, nsa_g, g_rw, g_nsa = _split_cols(z_rest, REST_SPLITS)
    y_rw = _rwkv7_time_mix(r, k, v, wl, al, gl, rw_w0, rw_w2, rw_a0, rw_a2, rw_g2,
                           rw_kk, rw_ka, rw_rk, rw_ln_g, rw_ln_b)
    y_nsa = _nsa_attention(q, kc, vc, ks, vs, kw, vw, nsa_g, cmp_pe_k, cmp_w1_k, cmp_w2_k,
                           cmp_pe_v, cmp_w1_v, cmp_w2_v, rel_bias, pos)
    merged = jax.nn.sigmoid(g_rw) * (y_rw @ w_up_rwkv) + jax.nn.sigmoid(g_nsa) * (y_nsa @ w_up_nsa)
    return merged @ w_out


def _hier_moe(x, wg, bg, we, be, w1, w3, w2):
    B, S, D = x.shape
    N = B * S
    f32 = jnp.float32
    xf = x.reshape(N, D)
    g_prob = jax.nn.softmax((xf @ wg + bg).astype(f32), axis=-1)
    grp = jnp.argmax(g_prob, axis=-1)
    p_grp = jnp.take_along_axis(g_prob, grp[:, None], axis=1)[:, 0]
    e_logits = (xf @ we + be).astype(f32).reshape(N, N_GROUPS, EXPERTS_PER_GROUP)
    e_logits = jnp.take_along_axis(e_logits, grp[:, None, None], axis=1)[:, 0]
    top_v, top_i = lax.top_k(e_logits, TOP_K_INNER)
    top_w = jax.nn.softmax(top_v, axis=-1) * p_grp[:, None]
    eid = (grp[:, None] * EXPERTS_PER_GROUP + top_i).reshape(-1).astype(jnp.int32)
    slot_w = top_w.reshape(-1)
    slot_tok = jnp.repeat(jnp.arange(N, dtype=jnp.int32), TOP_K_INNER)
    n_slots = N * TOP_K_INNER
    order = jnp.argsort(eid)
    eid_s = eid[order]
    counts = jax.ops.segment_sum(jnp.ones_like(eid), eid, num_segments=N_EXPERTS)
    starts = jnp.cumsum(counts) - counts
    pcounts = (counts + MOE_BLOCK - 1) // MOE_BLOCK * MOE_BLOCK
    pends = jnp.cumsum(pcounts)
    pstarts = pends - pcounts
    dest = pstarts[eid_s] + (jnp.arange(n_slots) - starts[eid_s])
    n_rows = n_slots + N_EXPERTS * MOE_BLOCK
    n_blk = n_rows // MOE_BLOCK
    row_tok = jnp.full((n_rows,), N, jnp.int32).at[dest].set(slot_tok[order])
    row_w = jnp.zeros((n_rows,), f32).at[dest].set(slot_w[order])
    blk_exp = jnp.minimum(jnp.sum(jnp.arange(n_blk)[:, None] * MOE_BLOCK >= pends[None, :], axis=1),
                          N_EXPERTS - 1)
    xpad = jnp.concatenate([xf, jnp.zeros((1, D), xf.dtype)], axis=0)
    xs = xpad[row_tok].reshape(n_blk, MOE_BLOCK, D)

    def expert_block(args):
        xb, e = args
        h = jax.nn.silu(xb @ w1[e]) * (xb @ w3[e])
        return h @ w2[e]

    ys = lax.map(expert_block, (xs, blk_exp)).reshape(n_rows, D)
    out = jnp.zeros((N + 1, D), f32).at[row_tok].add(ys.astype(f32) * row_w[:, None])
    return out[:N].reshape(B, S, D)


def kernel(x, rel_bias, w_in, shift_mu, rw_w0, rw_w2, rw_a0, rw_a2, rw_g2, rw_kk, rw_ka, rw_rk,
           rw_ln_g, rw_ln_b, cmp_pe_k, cmp_w1_k, cmp_w2_k, cmp_pe_v, cmp_w1_v, cmp_w2_v,
           w_up_rwkv, w_up_nsa, w_out, ln1_g, ln1_b, router_group_w, router_group_b,
           router_expert_w, router_expert_b, exp_w1, exp_w3, exp_w2, ln2_g, ln2_b):
    pos = _nsa_positional(rel_bias, x.shape[1])
    for l in range(DEPTH):
        h = _token_mixer(x, w_in[l], shift_mu[l], rw_w0[l], rw_w2[l], rw_a0[l], rw_a2[l], rw_g2[l],
                         rw_kk[l], rw_ka[l], rw_rk[l], rw_ln_g[l], rw_ln_b[l],
                         cmp_pe_k[l], cmp_w1_k[l], cmp_w2_k[l], cmp_pe_v[l], cmp_w1_v[l], cmp_w2_v[l],
                         w_up_rwkv[l], w_up_nsa[l], w_out[l], rel_bias, pos)
        x = _layer_norm(ALPHA * x + h, ln1_g[l], ln1_b[l])
        h = _hier_moe(x, router_group_w[l], router_group_b[l], router_expert_w[l], router_expert_b[l],
                      exp_w1[l], exp_w3[l], exp_w2[l])
        x = _layer_norm(ALPHA * x + h, ln2_g[l], ln2_b[l])
    return x
```

```python
import functools
import math

import jax
import jax.numpy as jnp
import numpy as np
from jax import lax
from jax.experimental import pallas as pl
from jax.experimental.pallas import tpu as pltpu

D_MODEL = 1024
DEPTH = 4
N_HEADS_RWKV = 8
HEAD_RWKV = 64
RWKV_WIDTH = N_HEADS_RWKV * HEAD_RWKV
LORA_W = 64
LORA_A = 64
LORA_G = 128
RWKV_GN_EPS = 64e-5
N_HEADS_NSA = 8
N_KV_GROUPS = 2
HEADS_PER_GROUP = N_HEADS_NSA // N_KV_GROUPS
HEAD_NSA = 64
NSA_Q_WIDTH = N_HEADS_NSA * HEAD_NSA
NSA_KV_WIDTH = N_KV_GROUPS * HEAD_NSA
CMP_BLOCK = 32
CMP_STRIDE = 16
CMP_HIDDEN = 128
SEL_BLOCK = 64
N_SELECT = 16
WINDOW = 512
Q_BLOCK = 128
N_BAND = WINDOW // Q_BLOCK + 1
NEG_INF = -1e30
FORCED_SCORE = 1e4
NUM_BUCKETS = 32
MAX_DISTANCE = 1024
N_GROUPS = 4
EXPERTS_PER_GROUP = 8
N_EXPERTS = N_GROUPS * EXPERTS_PER_GROUP
TOP_K_INNER = 2
D_EXPERT = 512
MOE_BLOCK = 128
ALPHA = (2 * DEPTH) ** 0.25
LN_EPS = 1e-5
SHIFT_SPLITS = (RWKV_WIDTH, RWKV_WIDTH, RWKV_WIDTH, LORA_W, LORA_A, LORA_G)
SHIFT_WIDTH = 3 * RWKV_WIDTH + LORA_W + LORA_A + LORA_G
REST_SPLITS = (NSA_Q_WIDTH,) + (NSA_KV_WIDTH,) * 6 + (3 * N_HEADS_NSA, D_MODEL, D_MODEL)


def _mm_body(x_ref, w_ref, o_ref):
    o_ref[...] = jnp.dot(x_ref[...].astype(jnp.bfloat16), w_ref[...].astype(jnp.bfloat16),
                         preferred_element_type=jnp.float32)


def _matmul(x, w, tm=512, tn=512):
    m, k = x.shape
    n = w.shape[1]
    assert m % tm == 0 and n % tn == 0
    return pl.pallas_call(
        _mm_body,
        grid=(n // tn, m // tm),
        in_specs=[pl.BlockSpec((tm, k), lambda j, i: (i, 0)),
                  pl.BlockSpec((k, tn), lambda j, i: (0, j))],
        out_specs=pl.BlockSpec((tm, tn), lambda j, i: (i, j)),
        out_shape=jax.ShapeDtypeStruct((m, n), jnp.float32),
        compiler_params=pltpu.CompilerParams(dimension_semantics=("arbitrary", "arbitrary")),
        name="matmul",
    )(x, w)


def _topk_mask_body(n_sel, s_ref, o_ref):
    s = s_ref[0, 0]
    ns = s.shape[0]
    jidx = lax.broadcasted_iota(jnp.int32, s.shape, 0)
    cnt = jnp.zeros(s.shape, jnp.float32)
    for jp in range(ns):
        row = s[jp:jp + 1, :]
        tie = jnp.where(jidx > jp, 1.0, 0.0)
        cnt = cnt + jnp.where(row > s, 1.0, jnp.where(row == s, tie, 0.0))
    o_ref[0, 0] = jnp.where(cnt < n_sel, 1.0, 0.0)


def _topk_mask(scores_t, n_sel, tq=512):
    B, G, NS, S = scores_t.shape
    tq = min(tq, S)
    return pl.pallas_call(
        functools.partial(_topk_mask_body, float(n_sel)),
        grid=(B, G, S // tq),
        in_specs=[pl.BlockSpec((1, 1, NS, tq), lambda b, g, i: (b, g, 0, i))],
        out_specs=pl.BlockSpec((1, 1, NS, tq), lambda b, g, i: (b, g, 0, i)),
        out_shape=jax.ShapeDtypeStruct((B, G, NS, S), jnp.float32),
        compiler_params=pltpu.CompilerParams(dimension_semantics=("arbitrary",) * 3),
        name="topk_mask",
    )(scores_t)


def _sel_attn_body(n_delta, q_ref, k_ref, v_ref, m_ref, t_ref, o_ref):
    i = pl.program_id(2)
    hg, qb, dh = q_ref.shape[2], q_ref.shape[3], q_ref.shape[4]
    ns = m_ref.shape[3]
    scale = dh ** -0.5
    maskf = m_ref[0, 0].astype(jnp.bfloat16)
    blk = lax.broadcasted_iota(jnp.int32, (ns, qb), 0)
    col = lax.broadcasted_iota(jnp.int32, (ns, qb), 1)
    per = qb // SEL_BLOCK

    def body(j, carry):
        ms, ls, accs = carry
        k = k_ref[0, 0, pl.ds(pl.multiple_of(j * qb, qb), qb), :]
        v = v_ref[0, 0, pl.ds(pl.multiple_of(j * qb, qb), qb), :]
        expand = jnp.where(blk == j * per + col // SEL_BLOCK, 1.0, 0.0).astype(jnp.bfloat16)
        selm = jnp.dot(maskf, expand, preferred_element_type=jnp.float32) > 0.5
        d = jnp.minimum(i - j, n_delta)
        new_ms, new_ls, new_accs = [], [], []
        for h in range(hg):
            lg = lax.dot_general(q_ref[0, 0, h], k, (((1,), (1,)), ((), ())),
                                 preferred_element_type=jnp.float32) * scale + t_ref[0, h, d]
            lg = jnp.where(selm, lg, NEG_INF)
            m_new = jnp.maximum(ms[h], jnp.max(lg, axis=-1, keepdims=True))
            p = jnp.exp(lg - m_new)
            alpha = jnp.exp(ms[h] - m_new)
            new_ls.append(alpha * ls[h] + jnp.sum(p, axis=-1, keepdims=True))
            new_accs.append(alpha * accs[h] + jnp.dot(p.astype(jnp.bfloat16), v,
                                                      preferred_element_type=jnp.float32))
            new_ms.append(m_new)
        return tuple(new_ms), tuple(new_ls), tuple(new_accs)

    init = (tuple(jnp.full((qb, 1), NEG_INF, jnp.float32) for _ in range(hg)),
            tuple(jnp.zeros((qb, 1), jnp.float32) for _ in range(hg)),
            tuple(jnp.zeros((qb, dh), jnp.float32) for _ in range(hg)))
    ms, ls, accs = lax.fori_loop(0, i + 1, body, init)
    for h in range(hg):
        o_ref[0, 0, h] = accs[h] / ls[h]


def _sel_attention(qh, ks, vs, selmask, bias_tiles):
    B, G, Hg, S, Dh = qh.shape
    NS = selmask.shape[-1]
    ND = bias_tiles.shape[2] - 1
    QB = bias_tiles.shape[-1]
    return pl.pallas_call(
        functools.partial(_sel_attn_body, ND),
        grid=(B, G, S // QB),
        in_specs=[pl.BlockSpec((1, 1, Hg, QB, Dh), lambda b, g, i: (b, g, 0, i, 0)),
                  pl.BlockSpec((1, 1, S, Dh), lambda b, g, i: (b, g, 0, 0)),
                  pl.BlockSpec((1, 1, S, Dh), lambda b, g, i: (b, g, 0, 0)),
                  pl.BlockSpec((1, 1, QB, NS), lambda b, g, i: (b, g, i, 0)),
                  pl.BlockSpec((1, Hg, ND + 1, QB, QB), lambda b, g, i: (g, 0, 0, 0, 0))],
        out_specs=pl.BlockSpec((1, 1, Hg, QB, Dh), lambda b, g, i: (b, g, 0, i, 0)),
        out_shape=jax.ShapeDtypeStruct((B, G, Hg, S, Dh), jnp.float32),
        compiler_params=pltpu.CompilerParams(dimension_semantics=("arbitrary",) * 3,
                                             vmem_limit_bytes=48 * 1024 * 1024),
        name="sel_attention",
    )(qh, ks, vs, selmask, bias_tiles)


def _sel_bias_tiles(rel_bias, seq):
    nd = -(-MAX_DISTANCE // Q_BLOCK)
    nd = min(nd, seq // Q_BLOCK)
    half = NUM_BUCKETS // 2
    min_far = nd * Q_BLOCK - (Q_BLOCK - 1)
    assert half + math.log(min_far / half) / math.log(MAX_DISTANCE / half) * half >= NUM_BUCKETS - 0.75
    qi = np.arange(Q_BLOCK)[:, None]
    kj = np.arange(Q_BLOCK)[None, :]
    dist = np.arange(nd + 1)[:, None, None] * Q_BLOCK + qi - kj
    dist[nd] = max(seq - 1, nd * Q_BLOCK)
    tiles = _bias_from_buckets(rel_bias, _t5_bucket(jnp.asarray(np.maximum(dist, 0))))
    causal = jnp.asarray(dist >= 0)
    return jnp.where(causal, tiles, NEG_INF)


def _split_cols(z, sizes):
    return jnp.split(z, np.cumsum(sizes)[:-1].tolist(), axis=-1)


def _layer_norm(x, g, b, eps=LN_EPS):
    mu = jnp.mean(x, axis=-1, keepdims=True)
    var = jnp.mean(jnp.square(x - mu), axis=-1, keepdims=True)
    return (x - mu) * lax.rsqrt(var + eps) * g + b


def _t5_bucket(dist):
    n = jnp.maximum(dist, 0)
    max_exact = NUM_BUCKETS // 2
    nf = jnp.maximum(n, 1).astype(jnp.float32)
    large = max_exact + (jnp.log(nf / max_exact) / math.log(MAX_DISTANCE / max_exact)
                         * (NUM_BUCKETS - max_exact)).astype(jnp.int32)
    large = jnp.minimum(large, NUM_BUCKETS - 1)
    return jnp.where(n < max_exact, n, large)


def _bias_from_buckets(rel_bias, bucket):
    b = jnp.moveaxis(rel_bias.astype(jnp.float32)[bucket], -1, 0)
    return b.reshape((N_KV_GROUPS, HEADS_PER_GROUP) + bucket.shape)


def _nsa_positional(rel_bias, seq):
    n_cmp = seq // CMP_STRIDE - CMP_BLOCK // CMP_STRIDE + 1
    n_sel_blocks = seq // SEL_BLOCK
    t = jnp.arange(seq)[:, None]
    c = jnp.arange(n_cmp)[None, :]
    d_cmp = t - (c * CMP_STRIDE + CMP_BLOCK - 1)
    mask_cmp = d_cmp >= 0
    bias_cmp = _bias_from_buckets(rel_bias, _t5_bucket(d_cmp))
    qo = jnp.arange(Q_BLOCK)[:, None]
    m = jnp.arange(WINDOW + Q_BLOCK)[None, :]
    d_win = qo + WINDOW - m
    bias_win = _bias_from_buckets(rel_bias, _t5_bucket(d_win))
    blk = jnp.arange(seq // Q_BLOCK)[:, None, None]
    mask_win = (d_win >= 0) & (d_win < WINDOW) & (blk * Q_BLOCK - WINDOW + m >= 0)
    cs = jnp.arange(n_cmp)[:, None] * CMP_STRIDE
    ss = jnp.arange(n_sel_blocks)[None, :] * SEL_BLOCK
    overlap = jnp.clip(jnp.minimum(cs + CMP_BLOCK, ss + SEL_BLOCK) - jnp.maximum(cs, ss), 0, None)
    cmp_to_sel = overlap.astype(jnp.float32) / CMP_BLOCK
    return bias_cmp, mask_cmp, bias_win, mask_win, cmp_to_sel, _sel_bias_tiles(rel_bias, seq)


def _rwkv7_time_mix(r, k, v, wl, al, gl, w0, w2, a0, a2, g2, k_k, k_a, r_k, ln_g, ln_b):
    B, S, C = r.shape
    H, N = N_HEADS_RWKV, HEAD_RWKV
    f32 = jnp.float32
    logw = -jax.nn.softplus(-(w0 + jnp.tanh(wl) @ w2).astype(f32)) - 0.5
    decay = jnp.exp(-jnp.exp(logw))
    a = jax.nn.sigmoid((a0 + al @ a2).astype(f32))
    g = jax.nn.sigmoid(gl) @ g2
    heads = lambda t: t.reshape(B, S, H, N)
    kk = heads(k * k_k)
    kk = kk / jnp.maximum(jnp.linalg.norm(kk, axis=-1, keepdims=True), 1e-12)
    k = k * (1.0 + (a - 1.0) * k_a)
    rh, kh, vh, ah, wh = heads(r), heads(k), heads(v), heads(a), heads(decay)

    def step(state, inp):
        r_t, w_t, k_t, v_t, kk_t, a_t = inp
        sa = jnp.einsum('bhvk,bhk->bhv', state, -kk_t)
        state = (state * w_t[:, :, None, :] + sa[..., None] * (kk_t * a_t)[:, :, None, :]
                 + v_t[..., None] * k_t[:, :, None, :])
        return state, jnp.einsum('bhvk,bhk->bhv', state, r_t)

    xs = tuple(jnp.moveaxis(t, 1, 0) for t in (rh, wh, kh, vh, kk, ah))
    _, y = lax.scan(step, jnp.zeros((B, H, N, N), f32), xs)
    y = jnp.moveaxis(y, 0, 1)
    mu = jnp.mean(y, axis=-1, keepdims=True)
    var = jnp.mean(jnp.square(y - mu), axis=-1, keepdims=True)
    y = ((y - mu) * lax.rsqrt(var + RWKV_GN_EPS)).reshape(B, S, C) * ln_g + ln_b
    bonus = jnp.sum(rh * kh * r_k, axis=-1, keepdims=True) * vh
    return (y + bonus.reshape(B, S, C)) * g


def _compress(t, pe, w1, w2):
    B, S, G, Dh = t.shape
    rep = CMP_BLOCK // CMP_STRIDE
    nc = S // CMP_STRIDE - rep + 1
    sub = t.reshape(B, S // CMP_STRIDE, CMP_STRIDE, G, Dh)
    blk = jnp.concatenate([sub[:, j:j + nc] for j in range(rep)], axis=2)
    blk = blk + pe[:, None, :]
    blk = blk.transpose(0, 1, 3, 2, 4).reshape(B, nc, G, CMP_BLOCK * Dh)
    out = jax.nn.gelu(blk @ w1) @ w2
    return out.transpose(0, 2, 1, 3)


def _nsa_attention(q, k_cmp, v_cmp, k_slc, v_slc, k_win, v_win, gate_logits,
                   pe_k, w1_k, w2_k, pe_v, w1_v, w2_v, rel_bias, pos):
    bias_cmp, mask_cmp, bias_win, mask_win, cmp_to_sel, sel_tiles = pos
    B, S, _ = q.shape
    G, Hg, Dh = N_KV_GROUPS, HEADS_PER_GROUP, HEAD_NSA
    f32 = jnp.float32
    scale = HEAD_NSA ** -0.5
    qh = q.reshape(B, S, G, Hg, Dh).transpose(0, 2, 3, 1, 4)
    kv = lambda t: t.reshape(B, S, G, Dh).transpose(0, 2, 1, 3)
    kc = _compress(k_cmp.reshape(B, S, G, Dh), pe_k, w1_k, w2_k)
    vc = _compress(v_cmp.reshape(B, S, G, Dh), pe_v, w1_v, w2_v)
    lg = jnp.einsum('bghqd,bgcd->bghqc', qh, kc).astype(f32) * scale + bias_cmp
    p_cmp = jax.nn.softmax(jnp.where(mask_cmp, lg, NEG_INF), axis=-1) * mask_cmp
    o_cmp = jnp.einsum('bghqc,bgcd->bghqd', p_cmp, vc.astype(f32))
    n_sel_blocks = S // SEL_BLOCK
    n_sel = min(N_SELECT, n_sel_blocks)
    score = jnp.einsum('bgqc,cj->bgqj', jnp.sum(p_cmp, axis=2), cmp_to_sel)
    t = jnp.arange(S)[:, None]
    j = jnp.arange(n_sel_blocks)[None, :]
    cur = t // SEL_BLOCK
    forced = (j == 0) | (j == cur) | (j == cur - 1)
    score = jnp.where(forced, FORCED_SCORE, jnp.where(j <= cur, score, -1.0))
    selmask = _topk_mask(score.transpose(0, 1, 3, 2), n_sel).transpose(0, 1, 3, 2)
    bf16 = jnp.bfloat16
    o_slc = _sel_attention(qh.astype(bf16), kv(k_slc).astype(bf16), kv(v_slc).astype(bf16),
                           selmask, sel_tiles)
    nb = S // Q_BLOCK

    def band(t):
        tp = jnp.pad(kv(t), ((0, 0), (0, 0), (WINDOW, 0), (0, 0))).reshape(B, G, nb + N_BAND - 1, Q_BLOCK, Dh)
        return jnp.concatenate([tp[:, :, o:o + nb] for o in range(N_BAND)], axis=3)

    kw_b, vw_b = band(k_win), band(v_win)
    qb = qh.reshape(B, G, Hg, nb, Q_BLOCK, Dh)
    lgw = jnp.einsum('bghnqd,bgnkd->bghnqk', qb, kw_b).astype(f32) * scale + bias_win[:, :, None]
    pw = jax.nn.softmax(jnp.where(mask_win, lgw, NEG_INF), axis=-1)
    o_win = jnp.einsum('bghnqk,bgnkd->bghnqd', pw, vw_b.astype(f32)).reshape(B, G, Hg, S, Dh)
    gates = jax.nn.sigmoid(gate_logits.astype(f32)).reshape(B, S, G, Hg, 3).transpose(0, 2, 3, 1, 4)
    o = gates[..., 0:1] * o_cmp + gates[..., 1:2] * o_slc + gates[..., 2:3] * o_win
    return o.transpose(0, 3, 1, 2, 4).reshape(B, S, NSA_Q_WIDTH)


def _token_mixer(x, w_in, shift_mu, rw_w0, rw_w2, rw_a0, rw_a2, rw_g2, rw_kk, rw_ka, rw_rk,
                 rw_ln_g, rw_ln_b, cmp_pe_k, cmp_w1_k, cmp_w2_k, cmp_pe_v, cmp_w1_v, cmp_w2_v,
                 w_up_rwkv, w_up_nsa, w_out, rel_bias, pos):
    B, S, D = x.shape
    n_in = w_in.shape[1]
    n_pad = (-n_in) % 512
    w_in_p = jnp.pad(w_in, ((0, 0), (0, n_pad)))
    z = _matmul(x.reshape(B * S, D), w_in_p)[:, :n_in].reshape(B, S, n_in)
    z_shift, z_rest = z[..., :SHIFT_WIDTH], z[..., SHIFT_WIDTH:]
    z_prev = jnp.pad(z_shift, ((0, 0), (1, 0), (0, 0)))[:, :-1]
    z_shift = z_shift + (z_prev - z_shift) * shift_mu
    r, k, v, wl, al, gl = _split_cols(z_shift, SHIFT_SPLITS)
    q, kc, vc, ks, vs, kw, vw, nsa_g, g_rw, g_nsa = _split_cols(z_rest, REST_SPLITS)
    y_rw = _rwkv7_time_mix(r, k, v, wl, al, gl, rw_w0, rw_w2, rw_a0, rw_a2, rw_g2,
                           rw_kk, rw_ka, rw_rk, rw_ln_g, rw_ln_b)
    y_nsa = _nsa_attention(q, kc, vc, ks, vs, kw, vw, nsa_g, cmp_pe_k, cmp_w1_k, cmp_w2_k,
                           cmp_pe_v, cmp_w1_v, cmp_w2_v, rel_bias, pos)
    merged = jax.nn.sigmoid(g_rw) * (y_rw @ w_up_rwkv) + jax.nn.sigmoid(g_nsa) * (y_nsa @ w_up_nsa)
    return merged @ w_out


def _hier_moe(x, wg, bg, we, be, w1, w3, w2):
    B, S, D = x.shape
    N = B * S
    f32 = jnp.float32
    xf = x.reshape(N, D)
    g_prob = jax.nn.softmax((xf @ wg + bg).astype(f32), axis=-1)
    grp = jnp.argmax(g_prob, axis=-1)
    p_grp = jnp.take_along_axis(g_prob, grp[:, None], axis=1)[:, 0]
    e_logits = (xf @ we + be).astype(f32).reshape(N, N_GROUPS, EXPERTS_PER_GROUP)
    e_logits = jnp.take_along_axis(e_logits, grp[:, None, None], axis=1)[:, 0]
    top_v, top_i = lax.top_k(e_logits, TOP_K_INNER)
    top_w = jax.nn.softmax(top_v, axis=-1) * p_grp[:, None]
    eid = (grp[:, None] * EXPERTS_PER_GROUP + top_i).reshape(-1).astype(jnp.int32)
    slot_w = top_w.reshape(-1)
    slot_tok = jnp.repeat(jnp.arange(N, dtype=jnp.int32), TOP_K_INNER)
    n_slots = N * TOP_K_INNER
    order = jnp.argsort(eid)
    eid_s = eid[order]
    counts = jax.ops.segment_sum(jnp.ones_like(eid), eid, num_segments=N_EXPERTS)
    starts = jnp.cumsum(counts) - counts
    pcounts = (counts + MOE_BLOCK - 1) // MOE_BLOCK * MOE_BLOCK
    pends = jnp.cumsum(pcounts)
    pstarts = pends - pcounts
    dest = pstarts[eid_s] + (jnp.arange(n_slots) - starts[eid_s])
    n_rows = n_slots + N_EXPERTS * MOE_BLOCK
    n_blk = n_rows // MOE_BLOCK
    row_tok = jnp.full((n_rows,), N, jnp.int32).at[dest].set(slot_tok[order])
    row_w = jnp.zeros((n_rows,), f32).at[dest].set(slot_w[order])
    blk_exp = jnp.minimum(jnp.sum(jnp.arange(n_blk)[:, None] * MOE_BLOCK >= pends[None, :], axis=1),
                          N_EXPERTS - 1)
    xpad = jnp.concatenate([xf, jnp.zeros((1, D), xf.dtype)], axis=0)
    xs = xpad[row_tok].reshape(n_blk, MOE_BLOCK, D)

    def expert_block(args):
        xb, e = args
        h = jax.nn.silu(xb @ w1[e]) * (xb @ w3[e])
        return h @ w2[e]

    ys = lax.map(expert_block, (xs, blk_exp)).reshape(n_rows, D)
    out = jnp.zeros((N + 1, D), f32).at[row_tok].add(ys.astype(f32) * row_w[:, None])
    return out[:N].reshape(B, S, D)


def kernel(x, rel_bias, w_in, shift_mu, rw_w0, rw_w2, rw_a0, rw_a2, rw_g2, rw_kk, rw_ka, rw_rk,
           rw_ln_g, rw_ln_b, cmp_pe_k, cmp_w1_k, cmp_w2_k, cmp_pe_v, cmp_w1_v, cmp_w2_v,
           w_up_rwkv, w_up_nsa, w_out, ln1_g, ln1_b, router_group_w, router_group_b,
           router_expert_w, router_expert_b, exp_w1, exp_w3, exp_w2, ln2_g, ln2_b):
    pos = _nsa_positional(rel_bias, x.shape[1])
    for l in range(DEPTH):
        h = _token_mixer(x, w_in[l], shift_mu[l], rw_w0[l], rw_w2[l], rw_a0[l], rw_a2[l], rw_g2[l],
                         rw_kk[l], rw_ka[l], rw_rk[l], rw_ln_g[l], rw_ln_b[l],
                         cmp_pe_k[l], cmp_w1_k[l], cmp_w2_k[l], cmp_pe_v[l], cmp_w1_v[l], cmp_w2_v[l],
                         w_up_rwkv[l], w_up_nsa[l], w_out[l], rel_bias, pos)
        x = _layer_norm(ALPHA * x + h, ln1_g[l], ln1_b[l])
        h = _hier_moe(x, router_group_w[l], router_group_b[l], router_expert_w[l], router_expert_b[l],
                      exp_w1[l], exp_w3[l], exp_w2[l])
        x = _layer_norm(ALPHA * x + h, ln2_g[l], ln2_b[l])
    return x
```

```python
import functools
import math

import jax
import jax.numpy as jnp
import numpy as np
from jax import lax
from jax.experimental import pallas as pl
from jax.experimental.pallas import tpu as pltpu

D_MODEL = 1024
DEPTH = 4
N_HEADS_RWKV = 8
HEAD_RWKV = 64
RWKV_WIDTH = N_HEADS_RWKV * HEAD_RWKV
LORA_W = 64
LORA_A = 64
LORA_G = 128
RWKV_GN_EPS = 64e-5
N_HEADS_NSA = 8
N_KV_GROUPS = 2
HEADS_PER_GROUP = N_HEADS_NSA // N_KV_GROUPS
HEAD_NSA = 64
NSA_Q_WIDTH = N_HEADS_NSA * HEAD_NSA
NSA_KV_WIDTH = N_KV_GROUPS * HEAD_NSA
CMP_BLOCK = 32
CMP_STRIDE = 16
CMP_HIDDEN = 128
SEL_BLOCK = 64
N_SELECT = 16
WINDOW = 512
Q_BLOCK = 128
N_BAND = WINDOW // Q_BLOCK + 1
NEG_INF = -1e30
FORCED_SCORE = 1e4
NUM_BUCKETS = 32
MAX_DISTANCE = 1024
N_GROUPS = 4
EXPERTS_PER_GROUP = 8
N_EXPERTS = N_GROUPS * EXPERTS_PER_GROUP
TOP_K_INNER = 2
D_EXPERT = 512
MOE_BLOCK = 128
ALPHA = (2 * DEPTH) ** 0.25
LN_EPS = 1e-5
SHIFT_SPLITS = (RWKV_WIDTH, RWKV_WIDTH, RWKV_WIDTH, LORA_W, LORA_A, LORA_G)
SHIFT_WIDTH = 3 * RWKV_WIDTH + LORA_W + LORA_A + LORA_G
REST_SPLITS = (NSA_Q_WIDTH,) + (NSA_KV_WIDTH,) * 6 + (3 * N_HEADS_NSA, D_MODEL, D_MODEL)


def _mm_body(x_ref, w_ref, o_ref):
    o_ref[...] = jnp.dot(x_ref[...].astype(jnp.bfloat16), w_ref[...].astype(jnp.bfloat16),
                         preferred_element_type=jnp.float32)


def _matmul(x, w, tm=512, tn=512):
    m, k = x.shape
    n = w.shape[1]
    assert m % tm == 0 and n % tn == 0
    return pl.pallas_call(
        _mm_body,
        grid=(n // tn, m // tm),
        in_specs=[pl.BlockSpec((tm, k), lambda j, i: (i, 0)),
                  pl.BlockSpec((k, tn), lambda j, i: (0, j))],
        out_specs=pl.BlockSpec((tm, tn), lambda j, i: (i, j)),
        out_shape=jax.ShapeDtypeStruct((m, n), jnp.float32),
        compiler_params=pltpu.CompilerParams(dimension_semantics=("arbitrary", "arbitrary")),
        name="matmul",
    )(x, w)


def _topk_mask_body(n_sel, s_ref, o_ref):
    s = s_ref[0, 0]
    ns = s.shape[0]
    jidx = lax.broadcasted_iota(jnp.int32, s.shape, 0)
    cnt = jnp.zeros(s.shape, jnp.float32)
    for jp in range(ns):
        row = s[jp:jp + 1, :]
        tie = jnp.where(jidx > jp, 1.0, 0.0)
        cnt = cnt + jnp.where(row > s, 1.0, jnp.where(row == s, tie, 0.0))
    o_ref[0, 0] = jnp.where(cnt < n_sel, 1.0, 0.0)


def _topk_mask(scores_t, n_sel, tq=512):
    B, G, NS, S = scores_t.shape
    tq = min(tq, S)
    return pl.pallas_call(
        functools.partial(_topk_mask_body, float(n_sel)),
        grid=(B, G, S // tq),
        in_specs=[pl.BlockSpec((1, 1, NS, tq), lambda b, g, i: (b, g, 0, i))],
        out_specs=pl.BlockSpec((1, 1, NS, tq), lambda b, g, i: (b, g, 0, i)),
        out_shape=jax.ShapeDtypeStruct((B, G, NS, S), jnp.float32),
        compiler_params=pltpu.CompilerParams(dimension_semantics=("arbitrary",) * 3),
        name="topk_mask",
    )(scores_t)


def _sel_attn_body(n_delta, q_ref, k_ref, v_ref, m_ref, t_ref, o_ref):
    i = pl.program_id(2)
    hg, qb, dh = q_ref.shape[2], q_ref.shape[3], q_ref.shape[4]
    ns = m_ref.shape[3]
    scale = dh ** -0.5
    maskf = m_ref[0, 0].astype(jnp.bfloat16)
    blk = lax.broadcasted_iota(jnp.int32, (ns, qb), 0)
    col = lax.broadcasted_iota(jnp.int32, (ns, qb), 1)
    per = qb // SEL_BLOCK

    def body(j, carry):
        ms, ls, accs = carry
        k = k_ref[0, 0, pl.ds(pl.multiple_of(j * qb, qb), qb), :]
        v = v_ref[0, 0, pl.ds(pl.multiple_of(j * qb, qb), qb), :]
        expand = jnp.where(blk == j * per + col // SEL_BLOCK, 1.0, 0.0).astype(jnp.bfloat16)
        selm = jnp.dot(maskf, expand, preferred_element_type=jnp.float32) > 0.5
        d = jnp.minimum(i - j, n_delta)
        new_ms, new_ls, new_accs = [], [], []
        for h in range(hg):
            lg = lax.dot_general(q_ref[0, 0, h], k, (((1,), (1,)), ((), ())),
                                 preferred_element_type=jnp.float32) * scale + t_ref[0, h, d]
            lg = jnp.where(selm, lg, NEG_INF)
            m_new = jnp.maximum(ms[h], jnp.max(lg, axis=-1, keepdims=True))
            p = jnp.exp(lg - m_new)
            alpha = jnp.exp(ms[h] - m_new)
            new_ls.append(alpha * ls[h] + jnp.sum(p, axis=-1, keepdims=True))
            new_accs.append(alpha * accs[h] + jnp.dot(p.astype(jnp.bfloat16), v,
                                                      preferred_element_type=jnp.float32))
            new_ms.append(m_new)
        return tuple(new_ms), tuple(new_ls), tuple(new_accs)

    init = (tuple(jnp.full((qb, 1), NEG_INF, jnp.float32) for _ in range(hg)),
            tuple(jnp.zeros((qb, 1), jnp.float32) for _ in range(hg)),
            tuple(jnp.zeros((qb, dh), jnp.float32) for _ in range(hg)))
    ms, ls, accs = lax.fori_loop(0, i + 1, body, init)
    for h in range(hg):
        o_ref[0, 0, h] = accs[h] / ls[h]


def _sel_attention(qh, ks, vs, selmask, bias_tiles):
    B, G, Hg, S, Dh = qh.shape
    NS = selmask.shape[-1]
    ND = bias_tiles.shape[2] - 1
    QB = bias_tiles.shape[-1]
    return pl.pallas_call(
        functools.partial(_sel_attn_body, ND),
        grid=(B, G, S // QB),
        in_specs=[pl.BlockSpec((1, 1, Hg, QB, Dh), lambda b, g, i: (b, g, 0, i, 0)),
                  pl.BlockSpec((1, 1, S, Dh), lambda b, g, i: (b, g, 0, 0)),
                  pl.BlockSpec((1, 1, S, Dh), lambda b, g, i: (b, g, 0, 0)),
                  pl.BlockSpec((1, 1, QB, NS), lambda b, g, i: (b, g, i, 0)),
                  pl.BlockSpec((1, Hg, ND + 1, QB, QB), lambda b, g, i: (g, 0, 0, 0, 0))],
        out_specs=pl.BlockSpec((1, 1, Hg, QB, Dh), lambda b, g, i: (b, g, 0, i, 0)),
        out_shape=jax.ShapeDtypeStruct((B, G, Hg, S, Dh), jnp.float32),
        compiler_params=pltpu.CompilerParams(dimension_semantics=("arbitrary",) * 3,
                                             vmem_limit_bytes=48 * 1024 * 1024),
        name="sel_attention",
    )(qh, ks, vs, selmask, bias_tiles)


def _sel_bias_tiles(rel_bias, seq):
    nd = -(-MAX_DISTANCE // Q_BLOCK)
    nd = min(nd, seq // Q_BLOCK)
    half = NUM_BUCKETS // 2
    min_far = nd * Q_BLOCK - (Q_BLOCK - 1)
    assert half + math.log(min_far / half) / math.log(MAX_DISTANCE / half) * half >= NUM_BUCKETS - 0.75
    qi = np.arange(Q_BLOCK)[:, None]
    kj = np.arange(Q_BLOCK)[None, :]
    dist = np.arange(nd + 1)[:, None, None] * Q_BLOCK + qi - kj
    dist[nd] = max(seq - 1, nd * Q_BLOCK)
    tiles = _bias_from_buckets(rel_bias, _t5_bucket(jnp.asarray(np.maximum(dist, 0))))
    causal = jnp.asarray(dist >= 0)
    return jnp.where(causal, tiles, NEG_INF)


RWKV_CHUNK = 64
RWKV_INV_BASE = 8


def _split_bf16(x):
    hi = x.astype(jnp.bfloat16)
    lo = (x - hi.astype(jnp.float32)).astype(jnp.bfloat16)
    return hi, lo


def _dot3(a, b, dims=(((1,), (0,)), ((), ()))):
    a_hi, a_lo = _split_bf16(a)
    b_hi, b_lo = _split_bf16(b)
    d = functools.partial(lax.dot_general, dimension_numbers=dims, preferred_element_type=jnp.float32)
    return d(a_hi, b_hi) + (d(a_lo, b_hi) + d(a_hi, b_lo))


_NT = (((1,), (1,)), ((), ()))
_TN = (((0,), (0,)), ((), ()))


def _rwkv_scan_body(n_heads, r_ref, lw_ref, k_ref, v_ref, kk_ref, lr_ref, y_ref, st_ref):
    c = pl.program_id(1)
    C = r_ref.shape[1]
    N = r_ref.shape[2] // n_heads
    f32 = jnp.float32

    @pl.when(c == 0)
    def _():
        st_ref[...] = jnp.zeros_like(st_ref)

    row = lax.broadcasted_iota(jnp.int32, (C, C), 0)
    col = lax.broadcasted_iota(jnp.int32, (C, C), 1)
    strict = col < row
    incl = col <= row
    eye = jnp.where(row == col, 1.0, 0.0).astype(f32)
    tri = jnp.where(incl, 1.0, 0.0).astype(jnp.bfloat16)

    lw = lw_ref[0]
    p1 = lw.astype(jnp.bfloat16)
    r1 = lw - p1.astype(f32)
    p2 = r1.astype(jnp.bfloat16)
    p3 = (r1 - p2.astype(f32)).astype(jnp.bfloat16)
    dd = functools.partial(jnp.dot, preferred_element_type=f32)
    cl_all = dd(tri, p1) + (dd(tri, p2) + dd(tri, p3))

    base = RWKV_INV_BASE
    diag_blk = strict & ((row // base) == (col // base))
    level_masks = []
    s = base
    while s < C:
        level_masks.append(strict & ((row // (2 * s)) == (col // (2 * s))) & ((row // s) != (col // s)))
        s *= 2

    ys = []
    for h in range(n_heads):
        sl = slice(h * N, (h + 1) * N)
        r, k, v, kk, lr = (ref[0, :, sl] for ref in (r_ref, k_ref, v_ref, kk_ref, lr_ref))
        lwh = lw[:, sl]
        cl = cl_all[:, sl]
        cl_end = cl[C - 1:C, :]
        g_in = jnp.exp(cl - lwh)
        g_t = jnp.exp(cl)
        g_inv = jnp.exp(-cl)
        g_rem = jnp.exp(cl_end - cl)
        b = kk * lr
        abar = -kk * g_in
        rbar = r * g_t
        bbar = b * g_inv
        kbar = k * g_inv
        bhat = b * g_rem
        khat = k * g_rem
        gmat = _dot3(jnp.concatenate([abar, rbar], axis=0), jnp.concatenate([bbar, kbar], axis=0), _NT)
        a_ab = jnp.where(strict, gmat[:C, :C], 0.0)
        a_ak = jnp.where(strict, gmat[:C, C:], 0.0)
        m_r = jnp.concatenate([jnp.where(incl, gmat[C:, :C], 0.0), jnp.where(incl, gmat[C:, C:], 0.0)], axis=1)
        d1 = jnp.where(diag_blk, a_ab, 0.0)
        x = eye + d1
        dp = d1
        s = 2
        while s < base:
            dp = _dot3(dp, dp)
            x = x + _dot3(x, dp)
            s *= 2
        for lm in level_masks:
            x = x + _dot3(x, _dot3(jnp.where(lm, a_ab, 0.0), x))
        w0 = jnp.concatenate([_dot3(a_ak, v), abar], axis=1)
        xw = _dot3(x, w0)
        u_loc, atil = xw[:, :N], xw[:, N:]
        uv = jnp.concatenate([u_loc, v], axis=0)
        y_loc = _dot3(m_r, uv)
        qm = rbar + _dot3(m_r[:, :C], atil)
        s_loc = _dot3(jnp.concatenate([bhat, khat], axis=0), uv, _TN)
        pm = eye[:N, :N] * jnp.exp(cl_end) + _dot3(bhat, atil, _TN)
        s0 = st_ref[h]
        ys.append(y_loc + _dot3(qm, s0))
        st_ref[h] = s_loc + _dot3(pm, s0)
    y_ref[0] = jnp.concatenate(ys, axis=1)


def _rwkv_scan(r, lw, k, v, kk, lr, n_heads):
    B, S, W = r.shape
    C = min(RWKV_CHUNK, S)
    N = W // n_heads
    spec = pl.BlockSpec((1, C, W), lambda b, c: (b, c, 0))
    return pl.pallas_call(
        functools.partial(_rwkv_scan_body, n_heads),
        grid=(B, S // C),
        in_specs=[spec] * 6,
        out_specs=spec,
        out_shape=jax.ShapeDtypeStruct((B, S, W), jnp.float32),
        scratch_shapes=[pltpu.VMEM((n_heads, N, N), jnp.float32)],
        compiler_params=pltpu.CompilerParams(dimension_semantics=("arbitrary", "arbitrary")),
        name="rwkv_scan",
    )(r, lw, k, v, kk, lr)


def _split_cols(z, sizes):
    return jnp.split(z, np.cumsum(sizes)[:-1].tolist(), axis=-1)


def _layer_norm(x, g, b, eps=LN_EPS):
    mu = jnp.mean(x, axis=-1, keepdims=True)
    var = jnp.mean(jnp.square(x - mu), axis=-1, keepdims=True)
    return (x - mu) * lax.rsqrt(var + eps) * g + b


def _t5_bucket(dist):
    n = jnp.maximum(dist, 0)
    max_exact = NUM_BUCKETS // 2
    nf = jnp.maximum(n, 1).astype(jnp.float32)
    large = max_exact + (jnp.log(nf / max_exact) / math.log(MAX_DISTANCE / max_exact)
                         * (NUM_BUCKETS - max_exact)).astype(jnp.int32)
    large = jnp.minimum(large, NUM_BUCKETS - 1)
    return jnp.where(n < max_exact, n, large)


def _bias_from_buckets(rel_bias, bucket):
    b = jnp.moveaxis(rel_bias.astype(jnp.float32)[bucket], -1, 0)
    return b.reshape((N_KV_GROUPS, HEADS_PER_GROUP) + bucket.shape)


def _nsa_positional(rel_bias, seq):
    n_cmp = seq // CMP_STRIDE - CMP_BLOCK // CMP_STRIDE + 1
    n_sel_blocks = seq // SEL_BLOCK
    t = jnp.arange(seq)[:, None]
    c = jnp.arange(n_cmp)[None, :]
    d_cmp = t - (c * CMP_STRIDE + CMP_BLOCK - 1)
    mask_cmp = d_cmp >= 0
    bias_cmp = _bias_from_buckets(rel_bias, _t5_bucket(d_cmp))
    qo = jnp.arange(Q_BLOCK)[:, None]
    m = jnp.arange(WINDOW + Q_BLOCK)[None, :]
    d_win = qo + WINDOW - m
    bias_win = _bias_from_buckets(rel_bias, _t5_bucket(d_win))
    blk = jnp.arange(seq // Q_BLOCK)[:, None, None]
    mask_win = (d_win >= 0) & (d_win < WINDOW) & (blk * Q_BLOCK - WINDOW + m >= 0)
    cs = jnp.arange(n_cmp)[:, None] * CMP_STRIDE
    ss = jnp.arange(n_sel_blocks)[None, :] * SEL_BLOCK
    overlap = jnp.clip(jnp.minimum(cs + CMP_BLOCK, ss + SEL_BLOCK) - jnp.maximum(cs, ss), 0, None)
    cmp_to_sel = overlap.astype(jnp.float32) / CMP_BLOCK
    return bias_cmp, mask_cmp, bias_win, mask_win, cmp_to_sel, _sel_bias_tiles(rel_bias, seq)


def _rwkv7_time_mix(r, k, v, wl, al, gl, w0, w2, a0, a2, g2, k_k, k_a, r_k, ln_g, ln_b):
    B, S, C = r.shape
    H, N = N_HEADS_RWKV, HEAD_RWKV
    f32 = jnp.float32
    logw = -jax.nn.softplus(-(w0 + jnp.tanh(wl) @ w2).astype(f32)) - 0.5
    log_decay = -jnp.exp(logw)
    a = jax.nn.sigmoid((a0 + al @ a2).astype(f32))
    g = jax.nn.sigmoid(gl) @ g2
    heads = lambda t: t.reshape(B, S, H, N)
    kk = heads(k * k_k)
    kk = kk / jnp.maximum(jnp.linalg.norm(kk, axis=-1, keepdims=True), 1e-12)
    k = k * (1.0 + (a - 1.0) * k_a)
    rh, kh, vh = heads(r), heads(k), heads(v)
    y = heads(_rwkv_scan(r, log_decay, k, v, kk.reshape(B, S, C), a, H))
    mu = jnp.mean(y, axis=-1, keepdims=True)
    var = jnp.mean(jnp.square(y - mu), axis=-1, keepdims=True)
    y = ((y - mu) * lax.rsqrt(var + RWKV_GN_EPS)).reshape(B, S, C) * ln_g + ln_b
    bonus = jnp.sum(rh * kh * r_k, axis=-1, keepdims=True) * vh
    return (y + bonus.reshape(B, S, C)) * g


def _compress(t, pe, w1, w2):
    B, S, G, Dh = t.shape
    rep = CMP_BLOCK // CMP_STRIDE
    nc = S // CMP_STRIDE - rep + 1
    sub = t.reshape(B, S // CMP_STRIDE, CMP_STRIDE, G, Dh)
    blk = jnp.concatenate([sub[:, j:j + nc] for j in range(rep)], axis=2)
    blk = blk + pe[:, None, :]
    blk = blk.transpose(0, 1, 3, 2, 4).reshape(B, nc, G, CMP_BLOCK * Dh)
    out = jax.nn.gelu(blk @ w1) @ w2
    return out.transpose(0, 2, 1, 3)


def _nsa_attention(q, k_cmp, v_cmp, k_slc, v_slc, k_win, v_win, gate_logits,
                   pe_k, w1_k, w2_k, pe_v, w1_v, w2_v, rel_bias, pos):
    bias_cmp, mask_cmp, bias_win, mask_win, cmp_to_sel, sel_tiles = pos
    B, S, _ = q.shape
    G, Hg, Dh = N_KV_GROUPS, HEADS_PER_GROUP, HEAD_NSA
    f32 = jnp.float32
    scale = HEAD_NSA ** -0.5
    qh = q.reshape(B, S, G, Hg, Dh).transpose(0, 2, 3, 1, 4)
    kv = lambda t: t.reshape(B, S, G, Dh).transpose(0, 2, 1, 3)
    kc = _compress(k_cmp.reshape(B, S, G, Dh), pe_k, w1_k, w2_k)
    vc = _compress(v_cmp.reshape(B, S, G, Dh), pe_v, w1_v, w2_v)
    lg = jnp.einsum('bghqd,bgcd->bghqc', qh, kc).astype(f32) * scale + bias_cmp
    p_cmp = jax.nn.softmax(jnp.where(mask_cmp, lg, NEG_INF), axis=-1) * mask_cmp
    o_cmp = jnp.einsum('bghqc,bgcd->bghqd', p_cmp, vc.astype(f32))
    n_sel_blocks = S // SEL_BLOCK
    n_sel = min(N_SELECT, n_sel_blocks)
    score = jnp.einsum('bgqc,cj->bgqj', jnp.sum(p_cmp, axis=2), cmp_to_sel)
    t = jnp.arange(S)[:, None]
    j = jnp.arange(n_sel_blocks)[None, :]
    cur = t // SEL_BLOCK
    forced = (j == 0) | (j == cur) | (j == cur - 1)
    score = jnp.where(forced, FORCED_SCORE, jnp.where(j <= cur, score, -1.0))
    selmask = _topk_mask(score.transpose(0, 1, 3, 2), n_sel).transpose(0, 1, 3, 2)
    bf16 = jnp.bfloat16
    o_slc = _sel_attention(qh.astype(bf16), kv(k_slc).astype(bf16), kv(v_slc).astype(bf16),
                           selmask, sel_tiles)
    nb = S // Q_BLOCK

    def band(t):
        tp = jnp.pad(kv(t), ((0, 0), (0, 0), (WINDOW, 0), (0, 0))).reshape(B, G, nb + N_BAND - 1, Q_BLOCK, Dh)
        return jnp.concatenate([tp[:, :, o:o + nb] for o in range(N_BAND)], axis=3)

    kw_b, vw_b = band(k_win), band(v_win)
    qb = qh.reshape(B, G, Hg, nb, Q_BLOCK, Dh)
    lgw = jnp.einsum('bghnqd,bgnkd->bghnqk', qb, kw_b).astype(f32) * scale + bias_win[:, :, None]
    pw = jax.nn.softmax(jnp.where(mask_win, lgw, NEG_INF), axis=-1)
    o_win = jnp.einsum('bghnqk,bgnkd->bghnqd', pw, vw_b.astype(f32)).reshape(B, G, Hg, S, Dh)
    gates = jax.nn.sigmoid(gate_logits.astype(f32)).reshape(B, S, G, Hg, 3).transpose(0, 2, 3, 1, 4)
    o = gates[..., 0:1] * o_cmp + gates[..., 1:2] * o_slc + gates[..., 2:3] * o_win
    return o.transpose(0, 3, 1, 2, 4).reshape(B, S, NSA_Q_WIDTH)


def _token_mixer(x, w_in, shift_mu, rw_w0, rw_w2, rw_a0, rw_a2, rw_g2, rw_kk, rw_ka, rw_rk,
                 rw_ln_g, rw_ln_b, cmp_pe_k, cmp_w1_k, cmp_w2_k, cmp_pe_v, cmp_w1_v, cmp_w2_v,
                 w_up_rwkv, w_up_nsa, w_out, rel_bias, pos):
    B, S, D = x.shape
    n_in = w_in.shape[1]
    n_pad = (-n_in) % 512
    w_in_p = jnp.pad(w_in, ((0, 0), (0, n_pad)))
    z = _matmul(x.reshape(B * S, D), w_in_p)[:, :n_in].reshape(B, S, n_in)
    z_shift, z_rest = z[..., :SHIFT_WIDTH], z[..., SHIFT_WIDTH:]
    z_prev = jnp.pad(z_shift, ((0, 0), (1, 0), (0, 0)))[:, :-1]
    z_shift = z_shift + (z_prev - z_shift) * shift_mu
    r, k, v, wl, al, gl = _split_cols(z_shift, SHIFT_SPLITS)
    q, kc, vc, ks, vs, kw, vw, nsa_g, g_rw, g_nsa = _split_cols(z_rest, REST_SPLITS)
    y_rw = _rwkv7_time_mix(r, k, v, wl, al, gl, rw_w0, rw_w2, rw_a0, rw_a2, rw_g2,
                           rw_kk, rw_ka, rw_rk, rw_ln_g, rw_ln_b)
    y_nsa = _nsa_attention(q, kc, vc, ks, vs, kw, vw, nsa_g, cmp_pe_k, cmp_w1_k, cmp_w2_k,
                           cmp_pe_v, cmp_w1_v, cmp_w2_v, rel_bias, pos)
    merged = jax.nn.sigmoid(g_rw) * (y_rw @ w_up_rwkv) + jax.nn.sigmoid(g_nsa) * (y_nsa @ w_up_nsa)
    return merged @ w_out


def _hier_moe(x, wg, bg, we, be, w1, w3, w2):
    B, S, D = x.shape
    N = B * S
    f32 = jnp.float32
    xf = x.reshape(N, D)
    g_prob = jax.nn.softmax((xf @ wg + bg).astype(f32), axis=-1)
    grp = jnp.argmax(g_prob, axis=-1)
    p_grp = jnp.take_along_axis(g_prob, grp[:, None], axis=1)[:, 0]
    e_logits = (xf @ we + be).astype(f32).reshape(N, N_GROUPS, EXPERTS_PER_GROUP)
    e_logits = jnp.take_along_axis(e_logits, grp[:, None, None], axis=1)[:, 0]
    top_v, top_i = lax.top_k(e_logits, TOP_K_INNER)
    top_w = jax.nn.softmax(top_v, axis=-1) * p_grp[:, None]
    eid = (grp[:, None] * EXPERTS_PER_GROUP + top_i).reshape(-1).astype(jnp.int32)
    slot_w = top_w.reshape(-1)
    slot_tok = jnp.repeat(jnp.arange(N, dtype=jnp.int32), TOP_K_INNER)
    n_slots = N * TOP_K_INNER
    order = jnp.argsort(eid)
    eid_s = eid[order]
    counts = jax.ops.segment_sum(jnp.ones_like(eid), eid, num_segments=N_EXPERTS)
    starts = jnp.cumsum(counts) - counts
    pcounts = (counts + MOE_BLOCK - 1) // MOE_BLOCK * MOE_BLOCK
    pends = jnp.cumsum(pcounts)
    pstarts = pends - pcounts
    dest = pstarts[eid_s] + (jnp.arange(n_slots) - starts[eid_s])
    n_rows = n_slots + N_EXPERTS * MOE_BLOCK
    n_blk = n_rows // MOE_BLOCK
    row_tok = jnp.full((n_rows,), N, jnp.int32).at[dest].set(slot_tok[order])
    row_w = jnp.zeros((n_rows,), f32).at[dest].set(slot_w[order])
    blk_exp = jnp.minimum(jnp.sum(jnp.arange(n_blk)[:, None] * MOE_BLOCK >= pends[None, :], axis=1),
                          N_EXPERTS - 1)
    xpad = jnp.concatenate([xf, jnp.zeros((1, D), xf.dtype)], axis=0)
    xs = xpad[row_tok].reshape(n_blk, MOE_BLOCK, D)

    def expert_block(args):
        xb, e = args
        h = jax.nn.silu(xb @ w1[e]) * (xb @ w3[e])
        return h @ w2[e]

    ys = lax.map(expert_block, (xs, blk_exp)).reshape(n_rows, D)
    out = jnp.zeros((N + 1, D), f32).at[row_tok].add(ys.astype(f32) * row_w[:, None])
    return out[:N].reshape(B, S, D)


def kernel(x, rel_bias, w_in, shift_mu, rw_w0, rw_w2, rw_a0, rw_a2, rw_g2, rw_kk, rw_ka, rw_rk,
           rw_ln_g, rw_ln_b, cmp_pe_k, cmp_w1_k, cmp_w2_k, cmp_pe_v, cmp_w1_v, cmp_w2_v,
           w_up_rwkv, w_up_nsa, w_out, ln1_g, ln1_b, router_group_w, router_group_b,
           router_expert_w, router_expert_b, exp_w1, exp_w3, exp_w2, ln2_g, ln2_b):
    pos = _nsa_positional(rel_bias, x.shape[1])
    for l in range(DEPTH):
        h = _token_mixer(x, w_in[l], shift_mu[l], rw_w0[l], rw_w2[l], rw_a0[l], rw_a2[l], rw_g2[l],
                         rw_kk[l], rw_ka[l], rw_rk[l], rw_ln_g[l], rw_ln_b[l],
                         cmp_pe_k[l], cmp_w1_k[l], cmp_w2_k[l], cmp_pe_v[l], cmp_w1_v[l], cmp_w2_v[l],
                         w_up_rwkv[l], w_up_nsa[l], w_out[l], rel_bias, pos)
        x = _layer_norm(ALPHA * x + h, ln1_g[l], ln1_b[l])
        h = _hier_moe(x, router_group_w[l], router_group_b[l], router_expert_w[l], router_expert_b[l],
                      exp_w1[l], exp_w3[l], exp_w2[l])
        x = _layer_norm(ALPHA * x + h, ln2_g[l], ln2_b[l])
    return x
```

```python
import functools
import math

import jax
import jax.numpy as jnp
import numpy as np
from jax import lax
from jax.experimental import pallas as pl
from jax.experimental.pallas import tpu as pltpu

D_MODEL = 1024
DEPTH = 4
N_HEADS_RWKV = 8
HEAD_RWKV = 64
RWKV_WIDTH = N_HEADS_RWKV * HEAD_RWKV
LORA_W = 64
LORA_A = 64
LORA_G = 128
RWKV_GN_EPS = 64e-5
N_HEADS_NSA = 8
N_KV_GROUPS = 2
HEADS_PER_GROUP = N_HEADS_NSA // N_KV_GROUPS
HEAD_NSA = 64
NSA_Q_WIDTH = N_HEADS_NSA * HEAD_NSA
NSA_KV_WIDTH = N_KV_GROUPS * HEAD_NSA
CMP_BLOCK = 32
CMP_STRIDE = 16
CMP_HIDDEN = 128
SEL_BLOCK = 64
N_SELECT = 16
WINDOW = 512
Q_BLOCK = 128
N_BAND = WINDOW // Q_BLOCK + 1
NEG_INF = -1e30
FORCED_SCORE = 1e4
NUM_BUCKETS = 32
MAX_DISTANCE = 1024
N_GROUPS = 4
EXPERTS_PER_GROUP = 8
N_EXPERTS = N_GROUPS * EXPERTS_PER_GROUP
TOP_K_INNER = 2
D_EXPERT = 512
MOE_BLOCK = 128
ALPHA = (2 * DEPTH) ** 0.25
LN_EPS = 1e-5
SHIFT_SPLITS = (RWKV_WIDTH, RWKV_WIDTH, RWKV_WIDTH, LORA_W, LORA_A, LORA_G)
SHIFT_WIDTH = 3 * RWKV_WIDTH + LORA_W + LORA_A + LORA_G
REST_SPLITS = (NSA_Q_WIDTH,) + (NSA_KV_WIDTH,) * 6 + (3 * N_HEADS_NSA, D_MODEL, D_MODEL)

VMEM_LIMIT = 48 * 1024 * 1024


def _mm_body(x_ref, w_ref, o_ref):
    o_ref[...] = jnp.dot(x_ref[...].astype(jnp.bfloat16), w_ref[...].astype(jnp.bfloat16),
                         preferred_element_type=jnp.float32)


def _matmul(x, w, tm=512, tn=512):
    m, k = x.shape
    n = w.shape[1]
    assert m % tm == 0 and n % tn == 0
    return pl.pallas_call(
        _mm_body,
        grid=(n // tn, m // tm),
        in_specs=[pl.BlockSpec((tm, k), lambda j, i: (i, 0)),
                  pl.BlockSpec((k, tn), lambda j, i: (0, j))],
        out_specs=pl.BlockSpec((tm, tn), lambda j, i: (i, j)),
        out_shape=jax.ShapeDtypeStruct((m, n), jnp.float32),
        compiler_params=pltpu.CompilerParams(dimension_semantics=("arbitrary", "arbitrary")),
        name="matmul",
    )(x, w)


def _topk_mask_body(n_sel, s_ref, o_ref):
    s = s_ref[0, 0]
    ns = s.shape[0]
    jidx = lax.broadcasted_iota(jnp.int32, s.shape, 0)
    cnt = jnp.zeros(s.shape, jnp.float32)
    for jp in range(ns):
        row = s[jp:jp + 1, :]
        tie = jnp.where(jidx > jp, 1.0, 0.0)
        cnt = cnt + jnp.where(row > s, 1.0, jnp.where(row == s, tie, 0.0))
    o_ref[0, 0] = jnp.where(cnt < n_sel, 1.0, 0.0)


def _topk_mask(scores_t, n_sel, tq=512):
    B, G, NS, S = scores_t.shape
    tq = min(tq, S)
    return pl.pallas_call(
        functools.partial(_topk_mask_body, float(n_sel)),
        grid=(B, G, S // tq),
        in_specs=[pl.BlockSpec((1, 1, NS, tq), lambda b, g, i: (b, g, 0, i))],
        out_specs=pl.BlockSpec((1, 1, NS, tq), lambda b, g, i: (b, g, 0, i)),
        out_shape=jax.ShapeDtypeStruct((B, G, NS, S), jnp.float32),
        compiler_params=pltpu.CompilerParams(dimension_semantics=("arbitrary",) * 3),
        name="topk_mask",
    )(scores_t)


def _sel_attn_body(n_delta, q_ref, k_ref, v_ref, m_ref, t_ref, o_ref, acc_ref):
    i = pl.program_id(2)
    hg, dh, qb = q_ref.shape[2], q_ref.shape[3], q_ref.shape[4]
    ns = m_ref.shape[2]
    maskf = m_ref[0, 0].astype(jnp.bfloat16)
    key = lax.broadcasted_iota(jnp.int32, (qb, ns), 0)
    blk = lax.broadcasted_iota(jnp.int32, (qb, ns), 1)
    per = qb // SEL_BLOCK
    acc_ref[...] = jnp.zeros_like(acc_ref)

    def body(j, carry):
        ms, ls = carry
        off = pl.multiple_of(j * qb, qb)
        k = k_ref[0, 0, pl.ds(off, qb), :]
        v = v_ref[0, 0, :, pl.ds(off, qb)]
        expand = jnp.where(blk == j * per + key // SEL_BLOCK, 1.0, 0.0).astype(jnp.bfloat16)
        selm = jnp.dot(expand, maskf, preferred_element_type=jnp.float32) > 0.5
        d = jnp.minimum(i - j, n_delta)
        new_ms, new_ls = [], []
        for h in range(hg):
            lg = jnp.dot(k, q_ref[0, 0, h], preferred_element_type=jnp.float32) + t_ref[0, h, d]
            lg = jnp.where(selm, lg, NEG_INF)
            m_new = jnp.maximum(ms[h], jnp.max(lg, axis=0, keepdims=True))
            p = jnp.exp(lg - m_new)
            alpha = jnp.exp(ms[h] - m_new)
            new_ls.append(alpha * ls[h] + jnp.sum(p, axis=0, keepdims=True))
            acc_ref[h] = alpha * acc_ref[h] + jnp.dot(v, p.astype(jnp.bfloat16),
                                                      preferred_element_type=jnp.float32)
            new_ms.append(m_new)
        return tuple(new_ms), tuple(new_ls)

    init = (tuple(jnp.full((1, qb), NEG_INF, jnp.float32) for _ in range(hg)),
            tuple(jnp.zeros((1, qb), jnp.float32) for _ in range(hg)))
    ms, ls = lax.fori_loop(0, i + 1, body, init)
    for h in range(hg):
        o_ref[0, 0, h] = acc_ref[h] / ls[h]


def _sel_attention(q_t, ks, vs_t, mask_t, bias_tiles):
    B, G, Hg, Dh, S = q_t.shape
    NS = mask_t.shape[2]
    ND = bias_tiles.shape[2] - 1
    QB = bias_tiles.shape[-1]
    return pl.pallas_call(
        functools.partial(_sel_attn_body, ND),
        grid=(B, G, S // QB),
        in_specs=[pl.BlockSpec((1, 1, Hg, Dh, QB), lambda b, g, i: (b, g, 0, 0, i)),
                  pl.BlockSpec((1, 1, S, Dh), lambda b, g, i: (b, g, 0, 0)),
                  pl.BlockSpec((1, 1, Dh, S), lambda b, g, i: (b, g, 0, 0)),
                  pl.BlockSpec((1, 1, NS, QB), lambda b, g, i: (b, g, 0, i)),
                  pl.BlockSpec((1, Hg, ND + 1, QB, QB), lambda b, g, i: (g, 0, 0, 0, 0))],
        out_specs=pl.BlockSpec((1, 1, Hg, Dh, QB), lambda b, g, i: (b, g, 0, 0, i)),
        out_shape=jax.ShapeDtypeStruct((B, G, Hg, Dh, S), jnp.float32),
        scratch_shapes=[pltpu.VMEM((Hg, Dh, QB), jnp.float32)],
        compiler_params=pltpu.CompilerParams(dimension_semantics=("arbitrary",) * 3,
                                             vmem_limit_bytes=VMEM_LIMIT),
        name="sel_attention",
    )(q_t, ks, vs_t, mask_t, bias_tiles)


def _sel_bias_tiles(rel_bias, seq):
    nd = -(-MAX_DISTANCE // Q_BLOCK)
    nd = min(nd, seq // Q_BLOCK)
    half = NUM_BUCKETS // 2
    min_far = nd * Q_BLOCK - (Q_BLOCK - 1)
    assert half + math.log(min_far / half) / math.log(MAX_DISTANCE / half) * half >= NUM_BUCKETS - 0.75
    kj = np.arange(Q_BLOCK)[:, None]
    qi = np.arange(Q_BLOCK)[None, :]
    dist = np.arange(nd + 1)[:, None, None] * Q_BLOCK + qi - kj
    dist[nd] = max(seq - 1, nd * Q_BLOCK)
    tiles = _bias_from_buckets(rel_bias, _t5_bucket(jnp.asarray(np.maximum(dist, 0))))
    causal = jnp.asarray(dist >= 0)
    return jnp.where(causal, tiles, NEG_INF)


RWKV_CHUNK = 64
RWKV_INV_BASE = 8


def _split_bf16(x):
    hi = x.astype(jnp.bfloat16)
    lo = (x - hi.astype(jnp.float32)).astype(jnp.bfloat16)
    return hi, lo


def _dot3(a, b, dims=(((1,), (0,)), ((), ()))):
    a_hi, a_lo = _split_bf16(a)
    b_hi, b_lo = _split_bf16(b)
    d = functools.partial(lax.dot_general, dimension_numbers=dims, preferred_element_type=jnp.float32)
    return d(a_hi, b_hi) + (d(a_lo, b_hi) + d(a_hi, b_lo))


_NT = (((1,), (1,)), ((), ()))
_TN = (((0,), (0,)), ((), ()))


def _rwkv_scan_body(n_heads, r_ref, lw_ref, k_ref, v_ref, kk_ref, lr_ref, y_ref, st_ref):
    c = pl.program_id(1)
    C = r_ref.shape[1]
    N = r_ref.shape[2] // n_heads
    H = range(n_heads)
    f32 = jnp.float32

    @pl.when(c == 0)
    def _():
        st_ref[...] = jnp.zeros_like(st_ref)

    row = lax.broadcasted_iota(jnp.int32, (C, C), 0)
    col = lax.broadcasted_iota(jnp.int32, (C, C), 1)
    strict = col < row
    incl = col <= row
    eye = jnp.where(row == col, 1.0, 0.0).astype(f32)
    tri = jnp.where(incl, 1.0, 0.0).astype(jnp.bfloat16)

    lw = lw_ref[0]
    p1 = lw.astype(jnp.bfloat16)
    r1 = lw - p1.astype(f32)
    p2 = r1.astype(jnp.bfloat16)
    p3 = (r1 - p2.astype(f32)).astype(jnp.bfloat16)
    dd = functools.partial(jnp.dot, preferred_element_type=f32)
    cl = dd(tri, p1) + (dd(tri, p2) + dd(tri, p3))

    base = RWKV_INV_BASE
    diag_blk = strict & ((row // base) == (col // base))
    level_masks = []
    s = base
    while s < C:
        level_masks.append(strict & ((row // (2 * s)) == (col // (2 * s))) & ((row // s) != (col // s)))
        s *= 2

    cl_end = cl[C - 1:C, :]
    kk = kk_ref[0]
    bb = kk * lr_ref[0]
    g_inv = jnp.exp(-cl)
    g_rem = jnp.exp(cl_end - cl)
    g_end = jnp.exp(cl_end)
    abar_w = -kk * jnp.exp(cl - lw)
    rbar_w = r_ref[0] * jnp.exp(cl)
    bbar_w = bb * g_inv
    kbar_w = k_ref[0] * g_inv
    bhat_w = bb * g_rem
    khat_w = k_ref[0] * g_rem
    v_w = v_ref[0]
    hs = lambda x, h: x[:, h * N:(h + 1) * N]

    abar = [hs(abar_w, h) for h in H]
    rbar = [hs(rbar_w, h) for h in H]
    v = [hs(v_w, h) for h in H]
    gmat = [_dot3(jnp.concatenate([abar[h], rbar[h]], axis=0),
                  jnp.concatenate([hs(bbar_w, h), hs(kbar_w, h)], axis=0), _NT) for h in H]
    a_ab = [jnp.where(strict, gmat[h][:C, :C], 0.0) for h in H]
    a_ak = [jnp.where(strict, gmat[h][:C, C:], 0.0) for h in H]
    m_rb = [jnp.where(incl, gmat[h][C:, :C], 0.0) for h in H]
    m_rk = [jnp.where(incl, gmat[h][C:, C:], 0.0) for h in H]
    akv = [_dot3(a_ak[h], v[h]) for h in H]
    dp = [jnp.where(diag_blk, a_ab[h], 0.0) for h in H]
    x = [eye + dp[h] for h in H]
    s = 2
    while s < base:
        dp = [_dot3(dp[h], dp[h]) for h in H]
        x = [x[h] + _dot3(x[h], dp[h]) for h in H]
        s *= 2
    for lm in level_masks:
        t = [_dot3(jnp.where(lm, a_ab[h], 0.0), x[h]) for h in H]
        x = [x[h] + _dot3(x[h], t[h]) for h in H]
    xw = [_dot3(x[h], jnp.concatenate([akv[h], abar[h]], axis=1)) for h in H]
    uv = [jnp.concatenate([xw[h][:, :N], v[h]], axis=0) for h in H]
    atil = [xw[h][:, N:] for h in H]
    y_loc = [_dot3(jnp.concatenate([m_rb[h], m_rk[h]], axis=1), uv[h]) for h in H]
    qm = [rbar[h] + _dot3(m_rb[h], atil[h]) for h in H]
    s_loc = [_dot3(jnp.concatenate([hs(bhat_w, h), hs(khat_w, h)], axis=0), uv[h], _TN) for h in H]
    pm = [eye[:N, :N] * hs(g_end, h) + _dot3(hs(bhat_w, h), atil[h], _TN) for h in H]
    s0 = [st_ref[h] for h in H]
    y = [y_loc[h] + _dot3(qm[h], s0[h]) for h in H]
    for h in H:
        st_ref[h] = s_loc[h] + _dot3(pm[h], s0[h])
    y_ref[0] = jnp.concatenate(y, axis=1)


def _rwkv_scan(r, lw, k, v, kk, lr, n_heads):
    B, S, W = r.shape
    C = min(RWKV_CHUNK, S)
    N = W // n_heads
    assert N <= C and S % C == 0
    spec = pl.BlockSpec((1, C, W), lambda b, c: (b, c, 0))
    return pl.pallas_call(
        functools.partial(_rwkv_scan_body, n_heads),
        grid=(B, S // C),
        in_specs=[spec] * 6,
        out_specs=spec,
        out_shape=jax.ShapeDtypeStruct((B, S, W), jnp.float32),
        scratch_shapes=[pltpu.VMEM((n_heads, N, N), jnp.float32)],
        compiler_params=pltpu.CompilerParams(dimension_semantics=("arbitrary", "arbitrary")),
        name="rwkv_scan",
    )(r, lw, k, v, kk, lr)


def _moe_body(be_ref, x_ref, w1_ref, w3_ref, w2_ref, o_ref):
    bf16 = jnp.bfloat16
    x = x_ref[...].astype(bf16)
    h1 = jnp.dot(x, w1_ref[0, 0].astype(bf16), preferred_element_type=jnp.float32)
    h3 = jnp.dot(x, w3_ref[0, 0].astype(bf16), preferred_element_type=jnp.float32)
    h = (h1 * jax.nn.sigmoid(h1)) * h3
    o_ref[...] = jnp.dot(h.astype(bf16), w2_ref[0, 0].astype(bf16), preferred_element_type=jnp.float32)


def _moe_experts(xs, blk_exp, w1, w3, w2, layer, blk):
    n_rows, D = xs.shape
    De = w1.shape[3]
    grid_spec = pltpu.PrefetchScalarGridSpec(
        num_scalar_prefetch=1,
        grid=(n_rows // blk,),
        in_specs=[pl.BlockSpec((blk, D), lambda i, be: (i, 0)),
                  pl.BlockSpec((1, 1, D, De), lambda i, be: (layer, be[i], 0, 0)),
                  pl.BlockSpec((1, 1, D, De), lambda i, be: (layer, be[i], 0, 0)),
                  pl.BlockSpec((1, 1, De, D), lambda i, be: (layer, be[i], 0, 0))],
        out_specs=pl.BlockSpec((blk, D), lambda i, be: (i, 0)))
    return pl.pallas_call(
        _moe_body,
        grid_spec=grid_spec,
        out_shape=jax.ShapeDtypeStruct((n_rows, D), jnp.float32),
        compiler_params=pltpu.CompilerParams(dimension_semantics=("arbitrary",),
                                             vmem_limit_bytes=VMEM_LIMIT),
        name="moe_experts",
    )(blk_exp, xs, w1, w3, w2)


def _split_cols(z, sizes):
    return jnp.split(z, np.cumsum(sizes)[:-1].tolist(), axis=-1)


def _layer_norm(x, g, b, eps=LN_EPS):
    mu = jnp.mean(x, axis=-1, keepdims=True)
    var = jnp.mean(jnp.square(x - mu), axis=-1, keepdims=True)
    return (x - mu) * lax.rsqrt(var + eps) * g + b


def _t5_bucket(dist):
    n = jnp.maximum(dist, 0)
    max_exact = NUM_BUCKETS // 2
    nf = jnp.maximum(n, 1).astype(jnp.float32)
    large = max_exact + (jnp.log(nf / max_exact) / math.log(MAX_DISTANCE / max_exact)
                         * (NUM_BUCKETS - max_exact)).astype(jnp.int32)
    large = jnp.minimum(large, NUM_BUCKETS - 1)
    return jnp.where(n < max_exact, n, large)


def _bias_from_buckets(rel_bias, bucket):
    rb = rel_bias.astype(jnp.float32)
    shape = (N_KV_GROUPS, HEADS_PER_GROUP) + (1,) * bucket.ndim
    out = jnp.zeros((N_KV_GROUPS, HEADS_PER_GROUP) + bucket.shape, jnp.float32)
    for b in range(NUM_BUCKETS):
        out = jnp.where(bucket == b, rb[b].reshape(shape), out)
    return out


def _nsa_positional(rel_bias, seq):
    n_cmp = seq // CMP_STRIDE - CMP_BLOCK // CMP_STRIDE + 1
    n_sel_blocks = seq // SEL_BLOCK
    t = jnp.arange(seq)[:, None]
    c = jnp.arange(n_cmp)[None, :]
    d_cmp = t - (c * CMP_STRIDE + CMP_BLOCK - 1)
    mask_cmp = d_cmp >= 0
    bias_cmp = _bias_from_buckets(rel_bias, _t5_bucket(d_cmp))
    qo = jnp.arange(Q_BLOCK)[:, None]
    m = jnp.arange(WINDOW + Q_BLOCK)[None, :]
    d_win = qo + WINDOW - m
    bias_win = _bias_from_buckets(rel_bias, _t5_bucket(d_win))
    blk = jnp.arange(seq // Q_BLOCK)[:, None, None]
    mask_win = (d_win >= 0) & (d_win < WINDOW) & (blk * Q_BLOCK - WINDOW + m >= 0)
    cs = jnp.arange(n_cmp)[:, None] * CMP_STRIDE
    ss = jnp.arange(n_sel_blocks)[None, :] * SEL_BLOCK
    overlap = jnp.clip(jnp.minimum(cs + CMP_BLOCK, ss + SEL_BLOCK) - jnp.maximum(cs, ss), 0, None)
    cmp_to_sel = overlap.astype(jnp.float32) / CMP_BLOCK
    return bias_cmp, mask_cmp, bias_win, mask_win, cmp_to_sel, _sel_bias_tiles(rel_bias, seq)


def _rwkv7_time_mix(r, k, v, wl, al, gl, w0, w2, a0, a2, g2, k_k, k_a, r_k, ln_g, ln_b):
    B, S, C = r.shape
    H, N = N_HEADS_RWKV, HEAD_RWKV
    f32 = jnp.float32
    logw = -jax.nn.softplus(-(w0 + jnp.tanh(wl) @ w2).astype(f32)) - 0.5
    log_decay = -jnp.exp(logw)
    a = jax.nn.sigmoid((a0 + al @ a2).astype(f32))
    g = jax.nn.sigmoid(gl) @ g2
    heads = lambda t: t.reshape(B, S, H, N)
    kk = heads(k * k_k)
    kk = kk / jnp.maximum(jnp.linalg.norm(kk, axis=-1, keepdims=True), 1e-12)
    k = k * (1.0 + (a - 1.0) * k_a)
    rh, kh, vh = heads(r), heads(k), heads(v)
    y = heads(_rwkv_scan(r, log_decay, k, v, kk.reshape(B, S, C), a, H))
    mu = jnp.mean(y, axis=-1, keepdims=True)
    var = jnp.mean(jnp.square(y - mu), axis=-1, keepdims=True)
    y = ((y - mu) * lax.rsqrt(var + RWKV_GN_EPS)).reshape(B, S, C) * ln_g + ln_b
    bonus = jnp.sum(rh * kh * r_k, axis=-1, keepdims=True) * vh
    return (y + bonus.reshape(B, S, C)) * g


def _compress(t, pe, w1, w2):
    B, S, G, Dh = t.shape
    rep = CMP_BLOCK // CMP_STRIDE
    nc = S // CMP_STRIDE - rep + 1
    sub = t.reshape(B, S // CMP_STRIDE, CMP_STRIDE, G, Dh)
    blk = jnp.concatenate([sub[:, j:j + nc] for j in range(rep)], axis=2)
    blk = blk + pe[:, None, :]
    blk = blk.transpose(0, 1, 3, 2, 4).reshape(B, nc, G, CMP_BLOCK * Dh)
    out = jax.nn.gelu(blk @ w1) @ w2
    return out.transpose(0, 2, 1, 3)


def _nsa_attention(q, k_cmp, v_cmp, k_slc, v_slc, k_win, v_win, gate_logits,
                   pe_k, w1_k, w2_k, pe_v, w1_v, w2_v, rel_bias, pos):
    bias_cmp, mask_cmp, bias_win, mask_win, cmp_to_sel, sel_tiles = pos
    B, S, _ = q.shape
    G, Hg, Dh = N_KV_GROUPS, HEADS_PER_GROUP, HEAD_NSA
    f32 = jnp.float32
    bf16 = jnp.bfloat16
    scale = HEAD_NSA ** -0.5
    assert math.frexp(scale)[0] == 0.5
    qh = q.reshape(B, S, G, Hg, Dh).transpose(0, 2, 3, 1, 4)
    kv = lambda t: t.reshape(B, S, G, Dh).transpose(0, 2, 1, 3)
    kc = _compress(k_cmp.reshape(B, S, G, Dh), pe_k, w1_k, w2_k)
    vc = _compress(v_cmp.reshape(B, S, G, Dh), pe_v, w1_v, w2_v)
    lg = jnp.einsum('bghqd,bgcd->bghqc', qh, kc).astype(f32) * scale + bias_cmp
    p_cmp = jax.nn.softmax(jnp.where(mask_cmp, lg, NEG_INF), axis=-1) * mask_cmp
    o_cmp = jnp.einsum('bghqc,bgcd->bghqd', p_cmp, vc.astype(f32))
    n_sel_blocks = S // SEL_BLOCK
    n_sel = min(N_SELECT, n_sel_blocks)
    score_t = jnp.einsum('bgqc,cj->bgjq', jnp.sum(p_cmp, axis=2), cmp_to_sel)
    t = jnp.arange(S)[None, :]
    j = jnp.arange(n_sel_blocks)[:, None]
    cur = t // SEL_BLOCK
    forced = (j == 0) | (j == cur) | (j == cur - 1)
    score_t = jnp.where(forced, FORCED_SCORE, jnp.where(j <= cur, score_t, -1.0))
    mask_t = _topk_mask(score_t, n_sel)
    q_t = (q * scale).reshape(B, S, G, Hg, Dh).transpose(0, 2, 3, 4, 1).astype(bf16)
    vs_t = v_slc.reshape(B, S, G, Dh).transpose(0, 2, 3, 1).astype(bf16)
    o_slc = _sel_attention(q_t, kv(k_slc).astype(bf16), vs_t, mask_t, sel_tiles).transpose(0, 1, 2, 4, 3)
    nb = S // Q_BLOCK

    def band(t):
        tp = jnp.pad(kv(t), ((0, 0), (0, 0), (WINDOW, 0), (0, 0))).reshape(B, G, nb + N_BAND - 1, Q_BLOCK, Dh)
        return jnp.concatenate([tp[:, :, o:o + nb] for o in range(N_BAND)], axis=3)

    kw_b, vw_b = band(k_win), band(v_win)
    qb = qh.reshape(B, G, Hg, nb, Q_BLOCK, Dh)
    lgw = jnp.einsum('bghnqd,bgnkd->bghnqk', qb, kw_b).astype(f32) * scale + bias_win[:, :, None]
    pw = jax.nn.softmax(jnp.where(mask_win, lgw, NEG_INF), axis=-1)
    o_win = jnp.einsum('bghnqk,bgnkd->bghnqd', pw, vw_b.astype(f32)).reshape(B, G, Hg, S, Dh)
    gates = jax.nn.sigmoid(gate_logits.astype(f32)).reshape(B, S, G, Hg, 3).transpose(0, 2, 3, 1, 4)
    o = gates[..., 0:1] * o_cmp + gates[..., 1:2] * o_slc + gates[..., 2:3] * o_win
    return o.transpose(0, 3, 1, 2, 4).reshape(B, S, NSA_Q_WIDTH)


def _token_mixer(x, w_in, shift_mu, rw_w0, rw_w2, rw_a0, rw_a2, rw_g2, rw_kk, rw_ka, rw_rk,
                 rw_ln_g, rw_ln_b, cmp_pe_k, cmp_w1_k, cmp_w2_k, cmp_pe_v, cmp_w1_v, cmp_w2_v,
                 w_up_rwkv, w_up_nsa, w_out, rel_bias, pos):
    B, S, D = x.shape
    n_in = w_in.shape[1]
    n_pad = (-n_in) % 512
    w_in_p = jnp.pad(w_in, ((0, 0), (0, n_pad)))
    z = _matmul(x.reshape(B * S, D), w_in_p)[:, :n_in].reshape(B, S, n_in)
    z_shift, z_rest = z[..., :SHIFT_WIDTH], z[..., SHIFT_WIDTH:]
    z_prev = jnp.pad(z_shift, ((0, 0), (1, 0), (0, 0)))[:, :-1]
    z_shift = z_shift + (z_prev - z_shift) * shift_mu
    r, k, v, wl, al, gl = _split_cols(z_shift, SHIFT_SPLITS)
    q, kc, vc, ks, vs, kw, vw, nsa_g, g_rw, g_nsa = _split_cols(z_rest, REST_SPLITS)
    y_rw = _rwkv7_time_mix(r, k, v, wl, al, gl, rw_w0, rw_w2, rw_a0, rw_a2, rw_g2,
                           rw_kk, rw_ka, rw_rk, rw_ln_g, rw_ln_b)
    y_nsa = _nsa_attention(q, kc, vc, ks, vs, kw, vw, nsa_g, cmp_pe_k, cmp_w1_k, cmp_w2_k,
                           cmp_pe_v, cmp_w1_v, cmp_w2_v, rel_bias, pos)
    merged = jax.nn.sigmoid(g_rw) * (y_rw @ w_up_rwkv) + jax.nn.sigmoid(g_nsa) * (y_nsa @ w_up_nsa)
    return merged @ w_out


def _hier_moe(x, wg, bg, we, be, w1, w3, w2, layer):
    B, S, D = x.shape
    N = B * S
    f32 = jnp.float32
    xf = x.reshape(N, D)
    g_prob = jax.nn.softmax((xf @ wg + bg).astype(f32), axis=-1)
    grp = jnp.argmax(g_prob, axis=-1)
    p_grp = jnp.take_along_axis(g_prob, grp[:, None], axis=1)[:, 0]
    e_logits = (xf @ we + be).astype(f32).reshape(N, N_GROUPS, EXPERTS_PER_GROUP)
    e_logits = jnp.take_along_axis(e_logits, grp[:, None, None], axis=1)[:, 0]
    top_v, top_i = lax.top_k(e_logits, TOP_K_INNER)
    top_w = jax.nn.softmax(top_v, axis=-1) * p_grp[:, None]
    eid = (grp[:, None] * EXPERTS_PER_GROUP + top_i).reshape(-1).astype(jnp.int32)
    slot_w = top_w.reshape(-1)
    slot_tok = jnp.repeat(jnp.arange(N, dtype=jnp.int32), TOP_K_INNER)
    n_slots = N * TOP_K_INNER
    order = jnp.argsort(eid)
    eid_s = eid[order]
    counts = jax.ops.segment_sum(jnp.ones_like(eid), eid, num_segments=N_EXPERTS)
    starts = jnp.cumsum(counts) - counts
    pcounts = (counts + MOE_BLOCK - 1) // MOE_BLOCK * MOE_BLOCK
    pends = jnp.cumsum(pcounts)
    pstarts = pends - pcounts
    dest = pstarts[eid_s] + (jnp.arange(n_slots) - starts[eid_s])
    n_rows = n_slots + N_EXPERTS * MOE_BLOCK
    n_blk = n_rows // MOE_BLOCK
    row_tok = jnp.full((n_rows,), N, jnp.int32).at[dest].set(slot_tok[order])
    row_w = jnp.zeros((n_rows,), f32).at[dest].set(slot_w[order])
    blk_exp = jnp.minimum(jnp.sum(jnp.arange(n_blk)[:, None] * MOE_BLOCK >= pends[None, :], axis=1),
                          N_EXPERTS - 1).astype(jnp.int32)
    xpad = jnp.concatenate([xf, jnp.zeros((1, D), xf.dtype)], axis=0)
    xs = xpad[row_tok]
    ys = _moe_experts(xs, blk_exp, w1, w3, w2, layer, MOE_BLOCK)
    out = jnp.zeros((N + 1, D), f32).at[row_tok].add(ys * row_w[:, None])
    return out[:N].reshape(B, S, D)


def kernel(x, rel_bias, w_in, shift_mu, rw_w0, rw_w2, rw_a0, rw_a2, rw_g2, rw_kk, rw_ka, rw_rk,
           rw_ln_g, rw_ln_b, cmp_pe_k, cmp_w1_k, cmp_w2_k, cmp_pe_v, cmp_w1_v, cmp_w2_v,
           w_up_rwkv, w_up_nsa, w_out, ln1_g, ln1_b, router_group_w, router_group_b,
           router_expert_w, router_expert_b, exp_w1, exp_w3, exp_w2, ln2_g, ln2_b):
    pos = _nsa_positional(rel_bias, x.shape[1])
    for l in range(DEPTH):
        h = _token_mixer(x, w_in[l], shift_mu[l], rw_w0[l], rw_w2[l], rw_a0[l], rw_a2[l], rw_g2[l],
                         rw_kk[l], rw_ka[l], rw_rk[l], rw_ln_g[l], rw_ln_b[l],
                         cmp_pe_k[l], cmp_w1_k[l], cmp_w2_k[l], cmp_pe_v[l], cmp_w1_v[l], cmp_w2_v[l],
                         w_up_rwkv[l], w_up_nsa[l], w_out[l], rel_bias, pos)
        x = _layer_norm(ALPHA * x + h, ln1_g[l], ln1_b[l])
        h = _hier_moe(x, router_group_w[l], router_group_b[l], router_expert_w[l], router_expert_b[l],
                      exp_w1, exp_w3, exp_w2, l)
        x = _layer_norm(ALPHA * x + h, ln2_g[l], ln2_b[l])
    return x
```

```python
import functools
import math

import jax
import jax.numpy as jnp
import numpy as np
from jax import lax
from jax.experimental import pallas as pl
from jax.experimental.pallas import tpu as pltpu

D_MODEL = 1024
DEPTH = 4
N_HEADS_RWKV = 8
HEAD_RWKV = 64
RWKV_WIDTH = N_HEADS_RWKV * HEAD_RWKV
LORA_W = 64
LORA_A = 64
LORA_G = 128
RWKV_GN_EPS = 64e-5
N_HEADS_NSA = 8
N_KV_GROUPS = 2
HEADS_PER_GROUP = N_HEADS_NSA // N_KV_GROUPS
HEAD_NSA = 64
NSA_Q_WIDTH = N_HEADS_NSA * HEAD_NSA
NSA_KV_WIDTH = N_KV_GROUPS * HEAD_NSA
CMP_BLOCK = 32
CMP_STRIDE = 16
CMP_HIDDEN = 128
SEL_BLOCK = 64
N_SELECT = 16
WINDOW = 512
Q_BLOCK = 128
N_BAND = WINDOW // Q_BLOCK + 1
NEG_INF = -1e30
FORCED_SCORE = 1e4
NUM_BUCKETS = 32
MAX_DISTANCE = 1024
N_GROUPS = 4
EXPERTS_PER_GROUP = 8
N_EXPERTS = N_GROUPS * EXPERTS_PER_GROUP
TOP_K_INNER = 2
D_EXPERT = 512
MOE_BLOCK = 128
ALPHA = (2 * DEPTH) ** 0.25
LN_EPS = 1e-5
SHIFT_SPLITS = (RWKV_WIDTH, RWKV_WIDTH, RWKV_WIDTH, LORA_W, LORA_A, LORA_G)
SHIFT_WIDTH = 3 * RWKV_WIDTH + LORA_W + LORA_A + LORA_G
REST_SPLITS = (NSA_Q_WIDTH,) + (NSA_KV_WIDTH,) * 6 + (3 * N_HEADS_NSA, D_MODEL, D_MODEL)

VMEM_LIMIT = 48 * 1024 * 1024


IN_PROJ_TM = 512
IN_PROJ_TN = 256


def _mm_body(x_ref, w_ref, o_ref, xb_ref):
    @pl.when(pl.program_id(1) == 0)
    def _():
        xb_ref[...] = x_ref[...].astype(jnp.bfloat16)

    o_ref[...] = jnp.dot(xb_ref[...], w_ref[...], preferred_element_type=jnp.float32)


def _matmul(x, w, tm, tn):
    m, k = x.shape
    n = w.shape[1]
    assert m % tm == 0 and n % tn == 0
    return pl.pallas_call(
        _mm_body,
        grid=(m // tm, n // tn),
        in_specs=[pl.BlockSpec((tm, k), lambda i, j: (i, 0)),
                  pl.BlockSpec((k, tn), lambda i, j: (0, j))],
        out_specs=pl.BlockSpec((tm, tn), lambda i, j: (i, j)),
        out_shape=jax.ShapeDtypeStruct((m, n), jnp.float32),
        scratch_shapes=[pltpu.VMEM((tm, k), jnp.bfloat16)],
        compiler_params=pltpu.CompilerParams(dimension_semantics=("arbitrary", "arbitrary")),
        name="in_proj",
    )(x, w)


MERGE_TM = 256


def _merge_body(yr_ref, yn_ref, grw_ref, gns_ref, x_ref, wur_ref, wun_ref, wo_ref, g_ref, b_ref, o_ref):
    f32, bf16 = jnp.float32, jnp.bfloat16
    up_r = jnp.dot(yr_ref[...].astype(bf16), wur_ref[...], preferred_element_type=f32)
    up_n = jnp.dot(yn_ref[...].astype(bf16), wun_ref[...], preferred_element_type=f32)
    merged = jax.nn.sigmoid(grw_ref[...]) * up_r + jax.nn.sigmoid(gns_ref[...]) * up_n
    y = ALPHA * x_ref[...] + jnp.dot(merged.astype(bf16), wo_ref[...], preferred_element_type=f32)
    mu = jnp.mean(y, axis=-1, keepdims=True)
    var = jnp.mean(jnp.square(y - mu), axis=-1, keepdims=True)
    o_ref[...] = (y - mu) * lax.rsqrt(var + LN_EPS) * g_ref[...] + b_ref[...]


def _merge_out(y_rw, y_nsa, z, gate_blk, x, w_up_rwkv, w_up_nsa, w_out, ln_g, ln_b):
    m, d = x.shape
    tm = MERGE_TM
    assert m % tm == 0
    row = lambda width: pl.BlockSpec((tm, width), lambda i: (i, 0))
    full = lambda a: pl.BlockSpec(a.shape, lambda i: (0,) * a.ndim)
    ln_g, ln_b = ln_g.reshape(1, d), ln_b.reshape(1, d)
    return pl.pallas_call(
        _merge_body,
        grid=(m // tm,),
        in_specs=[row(y_rw.shape[1]), row(y_nsa.shape[1]),
                  pl.BlockSpec((tm, d), lambda i: (i, gate_blk)),
                  pl.BlockSpec((tm, d), lambda i: (i, gate_blk + 1)),
                  row(d), full(w_up_rwkv), full(w_up_nsa), full(w_out), full(ln_g), full(ln_b)],
        out_specs=row(d),
        out_shape=jax.ShapeDtypeStruct((m, d), jnp.float32),
        compiler_params=pltpu.CompilerParams(dimension_semantics=("arbitrary",),
                                             vmem_limit_bytes=VMEM_LIMIT),
        name="merge_out",
    )(y_rw, y_nsa, z, z, x, w_up_rwkv, w_up_nsa, w_out, ln_g, ln_b)


def _topk_mask_body(n_sel, s_ref, o_ref):
    s = s_ref[0, 0]
    ns = s.shape[0]
    jidx = lax.broadcasted_iota(jnp.int32, s.shape, 0)
    cnt = jnp.zeros(s.shape, jnp.float32)
    for jp in range(ns):
        row = s[jp:jp + 1, :]
        tie = jnp.where(jidx > jp, 1.0, 0.0)
        cnt = cnt + jnp.where(row > s, 1.0, jnp.where(row == s, tie, 0.0))
    o_ref[0, 0] = jnp.where(cnt < n_sel, 1.0, 0.0)


def _topk_mask(scores_t, n_sel, tq=512):
    B, G, NS, S = scores_t.shape
    tq = min(tq, S)
    return pl.pallas_call(
        functools.partial(_topk_mask_body, float(n_sel)),
        grid=(B, G, S // tq),
        in_specs=[pl.BlockSpec((1, 1, NS, tq), lambda b, g, i: (b, g, 0, i))],
        out_specs=pl.BlockSpec((1, 1, NS, tq), lambda b, g, i: (b, g, 0, i)),
        out_shape=jax.ShapeDtypeStruct((B, G, NS, S), jnp.float32),
        compiler_params=pltpu.CompilerParams(dimension_semantics=("arbitrary",) * 3),
        name="topk_mask",
    )(scores_t)


def _sel_attn_body(n_delta, q_ref, k_ref, v_ref, m_ref, t_ref, o_ref, acc_ref):
    i = pl.program_id(2)
    hg, dh, qb = q_ref.shape[2], q_ref.shape[3], q_ref.shape[4]
    ns = m_ref.shape[2]
    maskf = m_ref[0, 0].astype(jnp.bfloat16)
    key = lax.broadcasted_iota(jnp.int32, (qb, ns), 0)
    blk = lax.broadcasted_iota(jnp.int32, (qb, ns), 1)
    per = qb // SEL_BLOCK
    acc_ref[...] = jnp.zeros_like(acc_ref)

    def body(j, carry):
        ms, ls = carry
        off = pl.multiple_of(j * qb, qb)
        k = k_ref[0, 0, pl.ds(off, qb), :]
        v = v_ref[0, 0, :, pl.ds(off, qb)]
        expand = jnp.where(blk == j * per + key // SEL_BLOCK, 1.0, 0.0).astype(jnp.bfloat16)
        selm = jnp.dot(expand, maskf, preferred_element_type=jnp.float32) > 0.5
        d = jnp.minimum(i - j, n_delta)
        new_ms, new_ls = [], []
        for h in range(hg):
            lg = jnp.dot(k, q_ref[0, 0, h], preferred_element_type=jnp.float32) + t_ref[0, h, d]
            lg = jnp.where(selm, lg, NEG_INF)
            m_new = jnp.maximum(ms[h], jnp.max(lg, axis=0, keepdims=True))
            p = jnp.exp(lg - m_new)
            alpha = jnp.exp(ms[h] - m_new)
            new_ls.append(alpha * ls[h] + jnp.sum(p, axis=0, keepdims=True))
            acc_ref[h] = alpha * acc_ref[h] + jnp.dot(v, p.astype(jnp.bfloat16),
                                                      preferred_element_type=jnp.float32)
            new_ms.append(m_new)
        return tuple(new_ms), tuple(new_ls)

    init = (tuple(jnp.full((1, qb), NEG_INF, jnp.float32) for _ in range(hg)),
            tuple(jnp.zeros((1, qb), jnp.float32) for _ in range(hg)))
    ms, ls = lax.fori_loop(0, i + 1, body, init)
    for h in range(hg):
        o_ref[0, 0, h] = acc_ref[h] / ls[h]


def _sel_attention(q_t, ks, vs_t, mask_t, bias_tiles):
    B, G, Hg, Dh, S = q_t.shape
    NS = mask_t.shape[2]
    ND = bias_tiles.shape[2] - 1
    QB = bias_tiles.shape[-1]
    return pl.pallas_call(
        functools.partial(_sel_attn_body, ND),
        grid=(B, G, S // QB),
        in_specs=[pl.BlockSpec((1, 1, Hg, Dh, QB), lambda b, g, i: (b, g, 0, 0, i)),
                  pl.BlockSpec((1, 1, S, Dh), lambda b, g, i: (b, g, 0, 0)),
                  pl.BlockSpec((1, 1, Dh, S), lambda b, g, i: (b, g, 0, 0)),
                  pl.BlockSpec((1, 1, NS, QB), lambda b, g, i: (b, g, 0, i)),
                  pl.BlockSpec((1, Hg, ND + 1, QB, QB), lambda b, g, i: (g, 0, 0, 0, 0))],
        out_specs=pl.BlockSpec((1, 1, Hg, Dh, QB), lambda b, g, i: (b, g, 0, 0, i)),
        out_shape=jax.ShapeDtypeStruct((B, G, Hg, Dh, S), jnp.float32),
        scratch_shapes=[pltpu.VMEM((Hg, Dh, QB), jnp.float32)],
        compiler_params=pltpu.CompilerParams(dimension_semantics=("arbitrary",) * 3,
                                             vmem_limit_bytes=VMEM_LIMIT),
        name="sel_attention",
    )(q_t, ks, vs_t, mask_t, bias_tiles)


def _sel_bias_tiles(rel_bias, seq):
    nd = -(-MAX_DISTANCE // Q_BLOCK)
    nd = min(nd, seq // Q_BLOCK)
    half = NUM_BUCKETS // 2
    min_far = nd * Q_BLOCK - (Q_BLOCK - 1)
    assert half + math.log(min_far / half) / math.log(MAX_DISTANCE / half) * half >= NUM_BUCKETS - 0.75
    kj = np.arange(Q_BLOCK)[:, None]
    qi = np.arange(Q_BLOCK)[None, :]
    dist = np.arange(nd + 1)[:, None, None] * Q_BLOCK + qi - kj
    dist[nd] = max(seq - 1, nd * Q_BLOCK)
    tiles = _bias_from_buckets(rel_bias, _t5_bucket(jnp.asarray(np.maximum(dist, 0))))
    causal = jnp.asarray(dist >= 0)
    return jnp.where(causal, tiles, NEG_INF)


SEL_KEY_TILE = 256


def _flash_update(h, lgs, v_ts, ms, ls, acc_ref):
    m_new = ms[h]
    for lg in lgs:
        m_new = jnp.maximum(m_new, jnp.max(lg, axis=0, keepdims=True))
    alpha = jnp.exp(ms[h] - m_new)
    l_new = alpha * ls[h]
    acc = alpha * acc_ref[h]
    for lg, v_t in zip(lgs, v_ts):
        p = jnp.exp(lg - m_new)
        l_new = l_new + jnp.sum(p, axis=0, keepdims=True)
        acc = acc + jnp.dot(v_t, p.astype(jnp.bfloat16), preferred_element_type=jnp.float32)
    acc_ref[h] = acc
    return m_new, l_new


def _nsa_body(n_sel, n_cmp, n_delta, q_ref, kc_ref, vc_ref, ks_ref, vs_ref, kw_ref, vw_ref, gl_ref,
              bc_ref, ts_ref, tw_ref, c2s_ref, o_ref, acc_s, acc_w, ocmp_ref):
    i = pl.program_id(2)
    hg, dh, qb = q_ref.shape[2], q_ref.shape[3], q_ref.shape[4]
    ncp = kc_ref.shape[2]
    ns = c2s_ref.shape[0]
    f32, bf16 = jnp.float32, jnp.bfloat16
    H = range(hg)

    cidx = lax.broadcasted_iota(jnp.int32, (ncp, qb), 0)
    tpos = lax.broadcasted_iota(jnp.int32, (ncp, qb), 1) + i * qb
    valid = (tpos - (cidx * CMP_STRIDE + CMP_BLOCK - 1)) >= 0
    fill = jnp.where(cidx < n_cmp, NEG_INF, 2.0 * NEG_INF)
    kc = kc_ref[0, 0]
    vc_t = vc_ref[0, 0]
    psum = jnp.zeros((ncp, qb), f32)
    for h in H:
        lg = jnp.dot(kc, q_ref[0, 0, h], preferred_element_type=f32) + bc_ref[0, h]
        lg = jnp.where(valid, lg, fill)
        m = jnp.max(lg, axis=0, keepdims=True)
        p = jnp.exp(lg - m)
        inv = 1.0 / jnp.sum(p, axis=0, keepdims=True)
        p = jnp.where(valid, p * inv, 0.0)
        psum = psum + p
        ocmp_ref[h] = jnp.dot(vc_t, p.astype(bf16), preferred_element_type=f32)

    score = jnp.dot(c2s_ref[...], psum.astype(bf16), preferred_element_type=f32)
    jblk = lax.broadcasted_iota(jnp.int32, (ns, qb), 0)
    cur = (lax.broadcasted_iota(jnp.int32, (ns, qb), 1) + i * qb) // SEL_BLOCK
    forced = (jblk == 0) | (jblk == cur) | (jblk == cur - 1)
    score = jnp.where(forced, FORCED_SCORE, jnp.where(jblk <= cur, score, -1.0))
    cnt = jnp.zeros((ns, qb), f32)
    for jp in range(ns):
        row = score[jp:jp + 1, :]
        tie = jnp.where(jblk > jp, 1.0, 0.0)
        cnt = cnt + jnp.where(row > score, 1.0, jnp.where(row == score, tie, 0.0))
    maskf = jnp.where(cnt < n_sel, 1.0, 0.0).astype(bf16)

    kt = SEL_KEY_TILE
    per = kt // qb
    key = lax.broadcasted_iota(jnp.int32, (kt, ns), 0)
    blk = lax.broadcasted_iota(jnp.int32, (kt, ns), 1)
    last_tile = i // per
    acc_s[...] = jnp.zeros_like(acc_s)

    def sel_step(jj, carry):
        ms, ls = carry
        ks_, vs_, selms, tiles = [], [], [], []
        for u in range(2):
            j = 2 * jj + u
            off = pl.multiple_of(jnp.minimum(j, last_tile) * kt, kt)
            ks_.append(ks_ref[0, 0, pl.ds(off, kt), :])
            vs_.append(vs_ref[0, 0, :, pl.ds(off, kt)])
            expand = jnp.where(blk == j * (kt // SEL_BLOCK) + key // SEL_BLOCK, 1.0, 0.0).astype(bf16)
            selms.append(jnp.dot(expand, maskf, preferred_element_type=f32) > 0.5)
            tiles.append([jnp.clip(i - (j * per + w), -1, n_delta) + 1 for w in range(per)])
        new_ms, new_ls = [], []
        for h in H:
            lgs = []
            for u in range(2):
                bias = jnp.concatenate([ts_ref[0, h, tiles[u][w]] for w in range(per)], axis=0)
                lg = jnp.dot(ks_[u], q_ref[0, 0, h], preferred_element_type=f32) + bias
                lgs.append(jnp.where(selms[u], lg, NEG_INF))
            m_new, l_new = _flash_update(h, lgs, vs_, ms, ls, acc_s)
            new_ms.append(m_new)
            new_ls.append(l_new)
        return tuple(new_ms), tuple(new_ls)

    init = (tuple(jnp.full((1, qb), NEG_INF, f32) for _ in H), tuple(jnp.zeros((1, qb), f32) for _ in H))
    _, ls_s = lax.fori_loop(0, last_tile // 2 + 1, sel_step, init)

    acc_w[...] = jnp.zeros_like(acc_w)
    kws, vws, widx = [], [], []
    for u in range(N_BAND):
        j = i - (N_BAND - 1) + u
        off = pl.multiple_of(jnp.maximum(j, 0) * qb, qb)
        kws.append(kw_ref[0, 0, pl.ds(off, qb), :])
        vws.append(vw_ref[0, 0, :, pl.ds(off, qb)])
        widx.append(jnp.where(j >= 0, N_BAND - 1 - u, N_BAND))
    ls_w = []
    for h in H:
        lgs = [jnp.dot(kws[u], q_ref[0, 0, h], preferred_element_type=f32) + tw_ref[0, h, widx[u]]
               for u in range(N_BAND)]
        _, l_new = _flash_update(h, lgs, vws, init[0], init[1], acc_w)
        ls_w.append(l_new)

    gates = jax.nn.sigmoid(gl_ref[0, 0])
    for h in H:
        o_ref[0, 0, h] = (gates[3 * h:3 * h + 1] * ocmp_ref[h]
                          + gates[3 * h + 1:3 * h + 2] * (acc_s[h] / ls_s[h])
                          + gates[3 * h + 2:3 * h + 3] * (acc_w[h] / ls_w[h]))


def _nsa_fused(q_t, kc, vc_t, ks, vs_t, kw, vw_t, gl_t, bias_cmp_t, sel_tiles, win_tiles, c2s_t, n_sel, n_cmp):
    B, G, Hg, Dh, S = q_t.shape
    NCP = kc.shape[2]
    NS = c2s_t.shape[0]
    QB = Q_BLOCK
    ND = sel_tiles.shape[2] - 2
    assert S % SEL_KEY_TILE == 0 and SEL_KEY_TILE % QB == 0
    bg = lambda *blk: pl.BlockSpec((1, 1) + blk, lambda b, g, i: (b, g) + (0,) * len(blk))
    gt = lambda arr: pl.BlockSpec((1,) + arr.shape[1:], lambda b, g, i: (g,) + (0,) * (arr.ndim - 1))
    return pl.pallas_call(
        functools.partial(_nsa_body, float(n_sel), n_cmp, ND),
        grid=(B, G, S // QB),
        in_specs=[pl.BlockSpec((1, 1, Hg, Dh, QB), lambda b, g, i: (b, g, 0, 0, i)),
                  bg(NCP, Dh), bg(Dh, NCP), bg(S, Dh), bg(Dh, S), bg(S, Dh), bg(Dh, S),
                  pl.BlockSpec((1, 1, 3 * Hg, QB), lambda b, g, i: (b, g, 0, i)),
                  pl.BlockSpec((1, Hg, NCP, QB), lambda b, g, i: (g, 0, 0, i)),
                  gt(sel_tiles), gt(win_tiles),
                  pl.BlockSpec(c2s_t.shape, lambda b, g, i: (0, 0))],
        out_specs=pl.BlockSpec((1, 1, Hg, Dh, QB), lambda b, g, i: (b, g, 0, 0, i)),
        out_shape=jax.ShapeDtypeStruct((B, G, Hg, Dh, S), jnp.float32),
        scratch_shapes=[pltpu.VMEM((Hg, Dh, QB), jnp.float32)] * 3,
        compiler_params=pltpu.CompilerParams(dimension_semantics=("arbitrary",) * 3,
                                             vmem_limit_bytes=VMEM_LIMIT),
        name="nsa_fused",
    )(q_t, kc, vc_t, ks, vs_t, kw, vw_t, gl_t, bias_cmp_t, sel_tiles, win_tiles, c2s_t)


def _nsa_tables(rel_bias, seq):
    n_cmp = seq // CMP_STRIDE - CMP_BLOCK // CMP_STRIDE + 1
    ncp = -(-n_cmp // 128) * 128
    c = jnp.arange(ncp)[:, None]
    t = jnp.arange(seq)[None, :]
    bias_cmp_t = _bias_from_buckets(rel_bias, _t5_bucket(t - (c * CMP_STRIDE + CMP_BLOCK - 1)))
    base = _sel_bias_tiles(rel_bias, seq)
    sel_tiles = jnp.concatenate([jnp.full_like(base[:, :, :1], NEG_INF), base], axis=2)
    kj = np.arange(Q_BLOCK)[:, None]
    qi = np.arange(Q_BLOCK)[None, :]
    dwin = np.arange(N_BAND)[:, None, None] * Q_BLOCK + qi - kj
    nwin = min(N_BAND, base.shape[2])
    win_tiles = jnp.where(jnp.asarray(dwin[:nwin] < WINDOW), base[:, :, :nwin], NEG_INF)
    win_tiles = jnp.concatenate(
        [win_tiles, jnp.full(win_tiles.shape[:2] + (N_BAND + 1 - nwin,) + win_tiles.shape[3:], NEG_INF)], axis=2)
    n_sel_blocks = seq // SEL_BLOCK
    cs = np.arange(ncp)[None, :] * CMP_STRIDE
    ss = np.arange(n_sel_blocks)[:, None] * SEL_BLOCK
    overlap = np.clip(np.minimum(cs + CMP_BLOCK, ss + SEL_BLOCK) - np.maximum(cs, ss), 0, None)
    overlap = np.where(np.arange(ncp)[None, :] < n_cmp, overlap, 0)
    c2s_t = jnp.asarray(overlap.astype(np.float32) / CMP_BLOCK, jnp.bfloat16)
    return bias_cmp_t, sel_tiles, win_tiles, c2s_t, n_cmp


RWKV_CHUNK = 64
RWKV_INV_BASE = 8


def _split_bf16(x):
    hi = x.astype(jnp.bfloat16)
    lo = (x - hi.astype(jnp.float32)).astype(jnp.bfloat16)
    return hi, lo


def _dot3(a, b, dims=(((1,), (0,)), ((), ()))):
    a_hi, a_lo = _split_bf16(a)
    b_hi, b_lo = _split_bf16(b)
    d = functools.partial(lax.dot_general, dimension_numbers=dims, preferred_element_type=jnp.float32)
    return d(a_hi, b_hi) + (d(a_lo, b_hi) + d(a_hi, b_lo))


_NT = (((1,), (1,)), ((), ()))
_TN = (((0,), (0,)), ((), ()))


def _rwkv_scan_body(n_heads, r_ref, lw_ref, k_ref, v_ref, kk_ref, lr_ref, y_ref, st_ref):
    c = pl.program_id(1)
    C = r_ref.shape[1]
    N = r_ref.shape[2] // n_heads
    H = range(n_heads)
    f32 = jnp.float32

    @pl.when(c == 0)
    def _():
        st_ref[...] = jnp.zeros_like(st_ref)

    row = lax.broadcasted_iota(jnp.int32, (C, C), 0)
    col = lax.broadcasted_iota(jnp.int32, (C, C), 1)
    strict = col < row
    incl = col <= row
    eye = jnp.where(row == col, 1.0, 0.0).astype(f32)
    tri = jnp.where(incl, 1.0, 0.0).astype(jnp.bfloat16)

    lw = lw_ref[0]
    p1 = lw.astype(jnp.bfloat16)
    r1 = lw - p1.astype(f32)
    p2 = r1.astype(jnp.bfloat16)
    p3 = (r1 - p2.astype(f32)).astype(jnp.bfloat16)
    dd = functools.partial(jnp.dot, preferred_element_type=f32)
    cl = dd(tri, p1) + (dd(tri, p2) + dd(tri, p3))

    base = RWKV_INV_BASE
    diag_blk = strict & ((row // base) == (col // base))
    level_masks = []
    s = base
    while s < C:
        level_masks.append(strict & ((row // (2 * s)) == (col // (2 * s))) & ((row // s) != (col // s)))
        s *= 2

    cl_end = cl[C - 1:C, :]
    kk = kk_ref[0]
    bb = kk * lr_ref[0]
    g_inv = jnp.exp(-cl)
    g_rem = jnp.exp(cl_end - cl)
    g_end = jnp.exp(cl_end)
    abar_w = -kk * jnp.exp(cl - lw)
    rbar_w = r_ref[0] * jnp.exp(cl)
    bbar_w = bb * g_inv
    kbar_w = k_ref[0] * g_inv
    bhat_w = bb * g_rem
    khat_w = k_ref[0] * g_rem
    v_w = v_ref[0]
    hs = lambda x, h: x[:, h * N:(h + 1) * N]

    abar = [hs(abar_w, h) for h in H]
    rbar = [hs(rbar_w, h) for h in H]
    v = [hs(v_w, h) for h in H]
    gmat = [_dot3(jnp.concatenate([abar[h], rbar[h]], axis=0),
                  jnp.concatenate([hs(bbar_w, h), hs(kbar_w, h)], axis=0), _NT) for h in H]
    a_ab = [jnp.where(strict, gmat[h][:C, :C], 0.0) for h in H]
    a_ak = [jnp.where(strict, gmat[h][:C, C:], 0.0) for h in H]
    m_rb = [jnp.where(incl, gmat[h][C:, :C], 0.0) for h in H]
    m_rk = [jnp.where(incl, gmat[h][C:, C:], 0.0) for h in H]
    akv = [_dot3(a_ak[h], v[h]) for h in H]
    dp = [jnp.where(diag_blk, a_ab[h], 0.0) for h in H]
    x = [eye + dp[h] for h in H]
    s = 2
    while s < base:
        dp = [_dot3(dp[h], dp[h]) for h in H]
        x = [x[h] + _dot3(x[h], dp[h]) for h in H]
        s *= 2
    for lm in level_masks:
        t = [_dot3(jnp.where(lm, a_ab[h], 0.0), x[h]) for h in H]
        x = [x[h] + _dot3(x[h], t[h]) for h in H]
    xw = [_dot3(x[h], jnp.concatenate([akv[h], abar[h]], axis=1)) for h in H]
    uv = [jnp.concatenate([xw[h][:, :N], v[h]], axis=0) for h in H]
    atil = [xw[h][:, N:] for h in H]
    y_loc = [_dot3(jnp.concatenate([m_rb[h], m_rk[h]], axis=1), uv[h]) for h in H]
    qm = [rbar[h] + _dot3(m_rb[h], atil[h]) for h in H]
    s_loc = [_dot3(jnp.concatenate([hs(bhat_w, h), hs(khat_w, h)], axis=0), uv[h], _TN) for h in H]
    pm = [eye[:N, :N] * hs(g_end, h) + _dot3(hs(bhat_w, h), atil[h], _TN) for h in H]
    s0 = [st_ref[h] for h in H]
    y = [y_loc[h] + _dot3(qm[h], s0[h]) for h in H]
    for h in H:
        st_ref[h] = s_loc[h] + _dot3(pm[h], s0[h])
    y_ref[0] = jnp.concatenate(y, axis=1)


def _rwkv_scan(r, lw, k, v, kk, lr, n_heads):
    B, S, W = r.shape
    C = min(RWKV_CHUNK, S)
    N = W // n_heads
    assert N <= C and S % C == 0
    spec = pl.BlockSpec((1, C, W), lambda b, c: (b, c, 0))
    return pl.pallas_call(
        functools.partial(_rwkv_scan_body, n_heads),
        grid=(B, S // C),
        in_specs=[spec] * 6,
        out_specs=spec,
        out_shape=jax.ShapeDtypeStruct((B, S, W), jnp.float32),
        scratch_shapes=[pltpu.VMEM((n_heads, N, N), jnp.float32)],
        compiler_params=pltpu.CompilerParams(dimension_semantics=("arbitrary", "arbitrary")),
        name="rwkv_scan",
    )(r, lw, k, v, kk, lr)


def _moe_body(be_ref, x_ref, w1_ref, w3_ref, w2_ref, o_ref):
    bf16 = jnp.bfloat16
    x = x_ref[...].astype(bf16)
    h1 = jnp.dot(x, w1_ref[0, 0].astype(bf16), preferred_element_type=jnp.float32)
    h3 = jnp.dot(x, w3_ref[0, 0].astype(bf16), preferred_element_type=jnp.float32)
    h = (h1 * jax.nn.sigmoid(h1)) * h3
    o_ref[...] = jnp.dot(h.astype(bf16), w2_ref[0, 0].astype(bf16), preferred_element_type=jnp.float32)


def _moe_experts(xs, blk_exp, w1, w3, w2, layer, blk):
    n_rows, D = xs.shape
    De = w1.shape[3]
    grid_spec = pltpu.PrefetchScalarGridSpec(
        num_scalar_prefetch=1,
        grid=(n_rows // blk,),
        in_specs=[pl.BlockSpec((blk, D), lambda i, be: (i, 0)),
                  pl.BlockSpec((1, 1, D, De), lambda i, be: (layer, be[i], 0, 0)),
                  pl.BlockSpec((1, 1, D, De), lambda i, be: (layer, be[i], 0, 0)),
                  pl.BlockSpec((1, 1, De, D), lambda i, be: (layer, be[i], 0, 0))],
        out_specs=pl.BlockSpec((blk, D), lambda i, be: (i, 0)))
    return pl.pallas_call(
        _moe_body,
        grid_spec=grid_spec,
        out_shape=jax.ShapeDtypeStruct((n_rows, D), jnp.float32),
        compiler_params=pltpu.CompilerParams(dimension_semantics=("arbitrary",),
                                             vmem_limit_bytes=VMEM_LIMIT),
        name="moe_experts",
    )(blk_exp, xs, w1, w3, w2)


def _split_cols(z, sizes):
    return jnp.split(z, np.cumsum(sizes)[:-1].tolist(), axis=-1)


def _layer_norm(x, g, b, eps=LN_EPS):
    mu = jnp.mean(x, axis=-1, keepdims=True)
    var = jnp.mean(jnp.square(x - mu), axis=-1, keepdims=True)
    return (x - mu) * lax.rsqrt(var + eps) * g + b


def _t5_bucket(dist):
    n = jnp.maximum(dist, 0)
    max_exact = NUM_BUCKETS // 2
    nf = jnp.maximum(n, 1).astype(jnp.float32)
    large = max_exact + (jnp.log(nf / max_exact) / math.log(MAX_DISTANCE / max_exact)
                         * (NUM_BUCKETS - max_exact)).astype(jnp.int32)
    large = jnp.minimum(large, NUM_BUCKETS - 1)
    return jnp.where(n < max_exact, n, large)


def _bias_from_buckets(rel_bias, bucket):
    rb = rel_bias.astype(jnp.float32)
    shape = (N_KV_GROUPS, HEADS_PER_GROUP) + (1,) * bucket.ndim
    out = jnp.zeros((N_KV_GROUPS, HEADS_PER_GROUP) + bucket.shape, jnp.float32)
    for b in range(NUM_BUCKETS):
        out = jnp.where(bucket == b, rb[b].reshape(shape), out)
    return out


def _nsa_positional(rel_bias, seq):
    n_cmp = seq // CMP_STRIDE - CMP_BLOCK // CMP_STRIDE + 1
    n_sel_blocks = seq // SEL_BLOCK
    t = jnp.arange(seq)[:, None]
    c = jnp.arange(n_cmp)[None, :]
    d_cmp = t - (c * CMP_STRIDE + CMP_BLOCK - 1)
    mask_cmp = d_cmp >= 0
    bias_cmp = _bias_from_buckets(rel_bias, _t5_bucket(d_cmp))
    qo = jnp.arange(Q_BLOCK)[:, None]
    m = jnp.arange(WINDOW + Q_BLOCK)[None, :]
    d_win = qo + WINDOW - m
    bias_win = _bias_from_buckets(rel_bias, _t5_bucket(d_win))
    blk = jnp.arange(seq // Q_BLOCK)[:, None, None]
    mask_win = (d_win >= 0) & (d_win < WINDOW) & (blk * Q_BLOCK - WINDOW + m >= 0)
    cs = jnp.arange(n_cmp)[:, None] * CMP_STRIDE
    ss = jnp.arange(n_sel_blocks)[None, :] * SEL_BLOCK
    overlap = jnp.clip(jnp.minimum(cs + CMP_BLOCK, ss + SEL_BLOCK) - jnp.maximum(cs, ss), 0, None)
    cmp_to_sel = overlap.astype(jnp.float32) / CMP_BLOCK
    return bias_cmp, mask_cmp, bias_win, mask_win, cmp_to_sel, _sel_bias_tiles(rel_bias, seq)


def _rwkv7_time_mix(r, k, v, wl, al, gl, w0, w2, a0, a2, g2, k_k, k_a, r_k, ln_g, ln_b):
    B, S, C = r.shape
    H, N = N_HEADS_RWKV, HEAD_RWKV
    f32 = jnp.float32
    logw = -jax.nn.softplus(-(w0 + jnp.tanh(wl) @ w2).astype(f32)) - 0.5
    log_decay = -jnp.exp(logw)
    a = jax.nn.sigmoid((a0 + al @ a2).astype(f32))
    g = jax.nn.sigmoid(gl) @ g2
    heads = lambda t: t.reshape(B, S, H, N)
    kk = heads(k * k_k)
    kk = kk / jnp.maximum(jnp.linalg.norm(kk, axis=-1, keepdims=True), 1e-12)
    k = k * (1.0 + (a - 1.0) * k_a)
    rh, kh, vh = heads(r), heads(k), heads(v)
    y = heads(_rwkv_scan(r, log_decay, k, v, kk.reshape(B, S, C), a, H))
    mu = jnp.mean(y, axis=-1, keepdims=True)
    var = jnp.mean(jnp.square(y - mu), axis=-1, keepdims=True)
    y = ((y - mu) * lax.rsqrt(var + RWKV_GN_EPS)).reshape(B, S, C) * ln_g + ln_b
    bonus = jnp.sum(rh * kh * r_k, axis=-1, keepdims=True) * vh
    return (y + bonus.reshape(B, S, C)) * g


def _compress(t, pe, w1, w2):
    B, S, G, Dh = t.shape
    rep = CMP_BLOCK // CMP_STRIDE
    nc = S // CMP_STRIDE - rep + 1
    sub = t.reshape(B, S // CMP_STRIDE, CMP_STRIDE, G, Dh)
    blk = jnp.concatenate([sub[:, j:j + nc] for j in range(rep)], axis=2)
    blk = blk + pe[:, None, :]
    blk = blk.transpose(0, 1, 3, 2, 4).reshape(B, nc, G, CMP_BLOCK * Dh)
    out = jax.nn.gelu(blk @ w1) @ w2
    return out.transpose(0, 2, 1, 3)


def _nsa_attention(q, k_cmp, v_cmp, k_slc, v_slc, k_win, v_win, gate_logits,
                   pe_k, w1_k, w2_k, pe_v, w1_v, w2_v, rel_bias, tables):
    bias_cmp_t, sel_tiles, win_tiles, c2s_t, n_cmp = tables
    B, S, _ = q.shape
    G, Hg, Dh = N_KV_GROUPS, HEADS_PER_GROUP, HEAD_NSA
    bf16 = jnp.bfloat16
    scale = HEAD_NSA ** -0.5
    assert math.frexp(scale)[0] == 0.5
    rows = lambda t: t.reshape(B, S, G, Dh).transpose(0, 2, 1, 3).astype(bf16)
    cols = lambda t: t.reshape(B, S, G, Dh).transpose(0, 2, 3, 1).astype(bf16)
    ncp = bias_cmp_t.shape[2]
    kc = _compress(k_cmp.reshape(B, S, G, Dh), pe_k, w1_k, w2_k)
    vc = _compress(v_cmp.reshape(B, S, G, Dh), pe_v, w1_v, w2_v)
    pad = ((0, 0), (0, 0), (0, ncp - n_cmp), (0, 0))
    kc_p = jnp.pad(kc, pad).astype(bf16)
    vc_t = jnp.pad(vc, pad).transpose(0, 1, 3, 2).astype(bf16)
    q_t = (q * scale).reshape(B, S, G, Hg, Dh).transpose(0, 2, 3, 4, 1).astype(bf16)
    gl_t = gate_logits.astype(jnp.float32).reshape(B, S, G, Hg * 3).transpose(0, 2, 3, 1)
    n_sel = min(N_SELECT, S // SEL_BLOCK)
    o_t = _nsa_fused(q_t, kc_p, vc_t, rows(k_slc), cols(v_slc), rows(k_win), cols(v_win), gl_t,
                     bias_cmp_t, sel_tiles, win_tiles, c2s_t, n_sel, n_cmp)
    return o_t.transpose(0, 4, 1, 2, 3).reshape(B, S, NSA_Q_WIDTH)


def _token_mixer(x, w_in, shift_mu, rw_w0, rw_w2, rw_a0, rw_a2, rw_g2, rw_kk, rw_ka, rw_rk,
                 rw_ln_g, rw_ln_b, cmp_pe_k, cmp_w1_k, cmp_w2_k, cmp_pe_v, cmp_w1_v, cmp_w2_v,
                 w_up_rwkv, w_up_nsa, w_out, ln_g, ln_b, rel_bias, pos):
    B, S, D = x.shape
    n_in = w_in.shape[1]
    n_gate = 3 * N_HEADS_NSA
    g_off = n_in - 2 * D_MODEL - n_gate
    lead_pad = (-g_off) % D_MODEL
    tail_pad = (-(g_off + lead_pad + 2 * D_MODEL + n_gate)) % IN_PROJ_TN
    zeros = lambda n: jnp.zeros((D, n), w_in.dtype)
    w_in_p = jnp.concatenate([w_in[:, :g_off], zeros(lead_pad), w_in[:, g_off + n_gate:],
                              w_in[:, g_off:g_off + n_gate], zeros(tail_pad)], axis=1).astype(jnp.bfloat16)
    gate_blk = (g_off + lead_pad) // D_MODEL
    z = _matmul(x.reshape(B * S, D), w_in_p, IN_PROJ_TM, IN_PROJ_TN)
    z3 = z.reshape(B, S, -1)
    z_shift, z_rest = z3[..., :SHIFT_WIDTH], z3[..., SHIFT_WIDTH:g_off]
    nsa_g = z3[..., (gate_blk + 2) * D_MODEL:(gate_blk + 2) * D_MODEL + n_gate]
    z_prev = jnp.pad(z_shift, ((0, 0), (1, 0), (0, 0)))[:, :-1]
    z_shift = z_shift + (z_prev - z_shift) * shift_mu
    r, k, v, wl, al, gl = _split_cols(z_shift, SHIFT_SPLITS)
    q, kc, vc, ks, vs, kw, vw = _split_cols(z_rest, REST_SPLITS[:7])
    y_rw = _rwkv7_time_mix(r, k, v, wl, al, gl, rw_w0, rw_w2, rw_a0, rw_a2, rw_g2,
                           rw_kk, rw_ka, rw_rk, rw_ln_g, rw_ln_b)
    y_nsa = _nsa_attention(q, kc, vc, ks, vs, kw, vw, nsa_g, cmp_pe_k, cmp_w1_k, cmp_w2_k,
                           cmp_pe_v, cmp_w1_v, cmp_w2_v, rel_bias, pos)
    bf16 = jnp.bfloat16
    return _merge_out(y_rw.reshape(B * S, -1), y_nsa.reshape(B * S, -1), z, gate_blk, x.reshape(B * S, D),
                      w_up_rwkv.astype(bf16), w_up_nsa.astype(bf16), w_out.astype(bf16),
                      ln_g, ln_b).reshape(B, S, D)


def _hier_moe(x, wg, bg, we, be, w1, w3, w2, layer):
    B, S, D = x.shape
    N = B * S
    f32 = jnp.float32
    xf = x.reshape(N, D)
    g_prob = jax.nn.softmax((xf @ wg + bg).astype(f32), axis=-1)
    grp = jnp.argmax(g_prob, axis=-1)
    p_grp = jnp.take_along_axis(g_prob, grp[:, None], axis=1)[:, 0]
    e_logits = (xf @ we + be).astype(f32).reshape(N, N_GROUPS, EXPERTS_PER_GROUP)
    e_logits = jnp.take_along_axis(e_logits, grp[:, None, None], axis=1)[:, 0]
    top_v, top_i = lax.top_k(e_logits, TOP_K_INNER)
    top_w = jax.nn.softmax(top_v, axis=-1) * p_grp[:, None]
    eid = (grp[:, None] * EXPERTS_PER_GROUP + top_i).reshape(-1).astype(jnp.int32)
    slot_w = top_w.reshape(-1)
    slot_tok = jnp.repeat(jnp.arange(N, dtype=jnp.int32), TOP_K_INNER)
    n_slots = N * TOP_K_INNER
    order = jnp.argsort(eid)
    eid_s = eid[order]
    counts = jax.ops.segment_sum(jnp.ones_like(eid), eid, num_segments=N_EXPERTS)
    starts = jnp.cumsum(counts) - counts
    pcounts = (counts + MOE_BLOCK - 1) // MOE_BLOCK * MOE_BLOCK
    pends = jnp.cumsum(pcounts)
    pstarts = pends - pcounts
    dest = pstarts[eid_s] + (jnp.arange(n_slots) - starts[eid_s])
    n_rows = n_slots + N_EXPERTS * MOE_BLOCK
    n_blk = n_rows // MOE_BLOCK
    row_tok = jnp.full((n_rows,), N, jnp.int32).at[dest].set(slot_tok[order])
    row_w = jnp.zeros((n_rows,), f32).at[dest].set(slot_w[order])
    blk_exp = jnp.minimum(jnp.sum(jnp.arange(n_blk)[:, None] * MOE_BLOCK >= pends[None, :], axis=1),
                          N_EXPERTS - 1).astype(jnp.int32)
    xpad = jnp.concatenate([xf, jnp.zeros((1, D), xf.dtype)], axis=0)
    xs = xpad[row_tok]
    ys = _moe_experts(xs, blk_exp, w1, w3, w2, layer, MOE_BLOCK)
    out = jnp.zeros((N + 1, D), f32).at[row_tok].add(ys * row_w[:, None])
    return out[:N].reshape(B, S, D)


def kernel(x, rel_bias, w_in, shift_mu, rw_w0, rw_w2, rw_a0, rw_a2, rw_g2, rw_kk, rw_ka, rw_rk,
           rw_ln_g, rw_ln_b, cmp_pe_k, cmp_w1_k, cmp_w2_k, cmp_pe_v, cmp_w1_v, cmp_w2_v,
           w_up_rwkv, w_up_nsa, w_out, ln1_g, ln1_b, router_group_w, router_group_b,
           router_expert_w, router_expert_b, exp_w1, exp_w3, exp_w2, ln2_g, ln2_b):
    pos = _nsa_tables(rel_bias, x.shape[1])
    for l in range(DEPTH):
        x = _token_mixer(x, w_in[l], shift_mu[l], rw_w0[l], rw_w2[l], rw_a0[l], rw_a2[l], rw_g2[l],
                         rw_kk[l], rw_ka[l], rw_rk[l], rw_ln_g[l], rw_ln_b[l],
                         cmp_pe_k[l], cmp_w1_k[l], cmp_w2_k[l], cmp_pe_v[l], cmp_w1_v[l], cmp_w2_v[l],
                         w_up_rwkv[l], w_up_nsa[l], w_out[l], ln1_g[l], ln1_b[l], rel_bias, pos)
        h = _hier_moe(x, router_group_w[l], router_group_b[l], router_expert_w[l], router_expert_b[l],
                      exp_w1, exp_w3, exp_w2, l)
        x = _layer_norm(ALPHA * x + h, ln2_g[l], ln2_b[l])
    return x
```

```python
import functools
import math

import jax
import jax.numpy as jnp
import numpy as np
from jax import lax
from jax.experimental import pallas as pl
from jax.experimental.pallas import tpu as pltpu

D_MODEL = 1024
DEPTH = 4
N_HEADS_RWKV = 8
HEAD_RWKV = 64
RWKV_WIDTH = N_HEADS_RWKV * HEAD_RWKV
LORA_W = 64
LORA_A = 64
LORA_G = 128
RWKV_GN_EPS = 64e-5
N_HEADS_NSA = 8
N_KV_GROUPS = 2
HEADS_PER_GROUP = N_HEADS_NSA // N_KV_GROUPS
HEAD_NSA = 64
NSA_Q_WIDTH = N_HEADS_NSA * HEAD_NSA
NSA_KV_WIDTH = N_KV_GROUPS * HEAD_NSA
CMP_BLOCK = 32
CMP_STRIDE = 16
CMP_HIDDEN = 128
SEL_BLOCK = 64
N_SELECT = 16
WINDOW = 512
Q_BLOCK = 128
N_BAND = WINDOW // Q_BLOCK + 1
NEG_INF = -1e30
FORCED_SCORE = 1e4
NUM_BUCKETS = 32
MAX_DISTANCE = 1024
N_GROUPS = 4
EXPERTS_PER_GROUP = 8
N_EXPERTS = N_GROUPS * EXPERTS_PER_GROUP
TOP_K_INNER = 2
D_EXPERT = 512
MOE_BLOCK = 128
ALPHA = (2 * DEPTH) ** 0.25
LN_EPS = 1e-5
SHIFT_SPLITS = (RWKV_WIDTH, RWKV_WIDTH, RWKV_WIDTH, LORA_W, LORA_A, LORA_G)
SHIFT_WIDTH = 3 * RWKV_WIDTH + LORA_W + LORA_A + LORA_G
REST_SPLITS = (NSA_Q_WIDTH,) + (NSA_KV_WIDTH,) * 6 + (3 * N_HEADS_NSA, D_MODEL, D_MODEL)

VMEM_LIMIT = 48 * 1024 * 1024


IN_PROJ_TM = 512
IN_PROJ_TN = 256


def _mm_body(x_ref, w_ref, o_ref, xb_ref):
    @pl.when(pl.program_id(1) == 0)
    def _():
        xb_ref[...] = x_ref[...].astype(jnp.bfloat16)

    o_ref[...] = jnp.dot(xb_ref[...], w_ref[...], preferred_element_type=jnp.float32)


def _matmul(x, w, tm, tn):
    m, k = x.shape
    n = w.shape[1]
    assert m % tm == 0 and n % tn == 0
    return pl.pallas_call(
        _mm_body,
        grid=(m // tm, n // tn),
        in_specs=[pl.BlockSpec((tm, k), lambda i, j: (i, 0)),
                  pl.BlockSpec((k, tn), lambda i, j: (0, j))],
        out_specs=pl.BlockSpec((tm, tn), lambda i, j: (i, j)),
        out_shape=jax.ShapeDtypeStruct((m, n), jnp.float32),
        scratch_shapes=[pltpu.VMEM((tm, k), jnp.bfloat16)],
        compiler_params=pltpu.CompilerParams(dimension_semantics=("arbitrary", "arbitrary")),
        name="in_proj",
    )(x, w)


MERGE_TM = 256


def _merge_body(yr_ref, yn_ref, grw_ref, gns_ref, x_ref, wur_ref, wun_ref, wo_ref, g_ref, b_ref, o_ref):
    f32, bf16 = jnp.float32, jnp.bfloat16
    up_r = jnp.dot(yr_ref[...].astype(bf16), wur_ref[...], preferred_element_type=f32)
    up_n = jnp.dot(yn_ref[...].astype(bf16), wun_ref[...], preferred_element_type=f32)
    merged = jax.nn.sigmoid(grw_ref[...]) * up_r + jax.nn.sigmoid(gns_ref[...]) * up_n
    y = ALPHA * x_ref[...] + jnp.dot(merged.astype(bf16), wo_ref[...], preferred_element_type=f32)
    mu = jnp.mean(y, axis=-1, keepdims=True)
    var = jnp.mean(jnp.square(y - mu), axis=-1, keepdims=True)
    o_ref[...] = (y - mu) * lax.rsqrt(var + LN_EPS) * g_ref[...] + b_ref[...]


def _merge_out(y_rw, y_nsa, z, gate_blk, x, w_up_rwkv, w_up_nsa, w_out, ln_g, ln_b):
    m, d = x.shape
    tm = MERGE_TM
    assert m % tm == 0
    row = lambda width: pl.BlockSpec((tm, width), lambda i: (i, 0))
    full = lambda a: pl.BlockSpec(a.shape, lambda i: (0,) * a.ndim)
    ln_g, ln_b = ln_g.reshape(1, d), ln_b.reshape(1, d)
    return pl.pallas_call(
        _merge_body,
        grid=(m // tm,),
        in_specs=[row(y_rw.shape[1]), row(y_nsa.shape[1]),
                  pl.BlockSpec((tm, d), lambda i: (i, gate_blk)),
                  pl.BlockSpec((tm, d), lambda i: (i, gate_blk + 1)),
                  row(d), full(w_up_rwkv), full(w_up_nsa), full(w_out), full(ln_g), full(ln_b)],
        out_specs=row(d),
        out_shape=jax.ShapeDtypeStruct((m, d), jnp.float32),
        compiler_params=pltpu.CompilerParams(dimension_semantics=("arbitrary",),
                                             vmem_limit_bytes=VMEM_LIMIT),
        name="merge_out",
    )(y_rw, y_nsa, z, z, x, w_up_rwkv, w_up_nsa, w_out, ln_g, ln_b)


def _topk_mask_body(n_sel, s_ref, o_ref):
    s = s_ref[0, 0]
    ns = s.shape[0]
    jidx = lax.broadcasted_iota(jnp.int32, s.shape, 0)
    cnt = jnp.zeros(s.shape, jnp.float32)
    for jp in range(ns):
        row = s[jp:jp + 1, :]
        tie = jnp.where(jidx > jp, 1.0, 0.0)
        cnt = cnt + jnp.where(row > s, 1.0, jnp.where(row == s, tie, 0.0))
    o_ref[0, 0] = jnp.where(cnt < n_sel, 1.0, 0.0)


def _topk_mask(scores_t, n_sel, tq=512):
    B, G, NS, S = scores_t.shape
    tq = min(tq, S)
    return pl.pallas_call(
        functools.partial(_topk_mask_body, float(n_sel)),
        grid=(B, G, S // tq),
        in_specs=[pl.BlockSpec((1, 1, NS, tq), lambda b, g, i: (b, g, 0, i))],
        out_specs=pl.BlockSpec((1, 1, NS, tq), lambda b, g, i: (b, g, 0, i)),
        out_shape=jax.ShapeDtypeStruct((B, G, NS, S), jnp.float32),
        compiler_params=pltpu.CompilerParams(dimension_semantics=("arbitrary",) * 3),
        name="topk_mask",
    )(scores_t)


def _sel_attn_body(n_delta, q_ref, k_ref, v_ref, m_ref, t_ref, o_ref, acc_ref):
    i = pl.program_id(2)
    hg, dh, qb = q_ref.shape[2], q_ref.shape[3], q_ref.shape[4]
    ns = m_ref.shape[2]
    maskf = m_ref[0, 0].astype(jnp.bfloat16)
    key = lax.broadcasted_iota(jnp.int32, (qb, ns), 0)
    blk = lax.broadcasted_iota(jnp.int32, (qb, ns), 1)
    per = qb // SEL_BLOCK
    acc_ref[...] = jnp.zeros_like(acc_ref)

    def body(j, carry):
        ms, ls = carry
        off = pl.multiple_of(j * qb, qb)
        k = k_ref[0, 0, pl.ds(off, qb), :]
        v = v_ref[0, 0, :, pl.ds(off, qb)]
        expand = jnp.where(blk == j * per + key // SEL_BLOCK, 1.0, 0.0).astype(jnp.bfloat16)
        selm = jnp.dot(expand, maskf, preferred_element_type=jnp.float32) > 0.5
        d = jnp.minimum(i - j, n_delta)
        new_ms, new_ls = [], []
        for h in range(hg):
            lg = jnp.dot(k, q_ref[0, 0, h], preferred_element_type=jnp.float32) + t_ref[0, h, d]
            lg = jnp.where(selm, lg, NEG_INF)
            m_new = jnp.maximum(ms[h], jnp.max(lg, axis=0, keepdims=True))
            p = jnp.exp(lg - m_new)
            alpha = jnp.exp(ms[h] - m_new)
            new_ls.append(alpha * ls[h] + jnp.sum(p, axis=0, keepdims=True))
            acc_ref[h] = alpha * acc_ref[h] + jnp.dot(v, p.astype(jnp.bfloat16),
                                                      preferred_element_type=jnp.float32)
            new_ms.append(m_new)
        return tuple(new_ms), tuple(new_ls)

    init = (tuple(jnp.full((1, qb), NEG_INF, jnp.float32) for _ in range(hg)),
            tuple(jnp.zeros((1, qb), jnp.float32) for _ in range(hg)))
    ms, ls = lax.fori_loop(0, i + 1, body, init)
    for h in range(hg):
        o_ref[0, 0, h] = acc_ref[h] / ls[h]


def _sel_attention(q_t, ks, vs_t, mask_t, bias_tiles):
    B, G, Hg, Dh, S = q_t.shape
    NS = mask_t.shape[2]
    ND = bias_tiles.shape[2] - 1
    QB = bias_tiles.shape[-1]
    return pl.pallas_call(
        functools.partial(_sel_attn_body, ND),
        grid=(B, G, S // QB),
        in_specs=[pl.BlockSpec((1, 1, Hg, Dh, QB), lambda b, g, i: (b, g, 0, 0, i)),
                  pl.BlockSpec((1, 1, S, Dh), lambda b, g, i: (b, g, 0, 0)),
                  pl.BlockSpec((1, 1, Dh, S), lambda b, g, i: (b, g, 0, 0)),
                  pl.BlockSpec((1, 1, NS, QB), lambda b, g, i: (b, g, 0, i)),
                  pl.BlockSpec((1, Hg, ND + 1, QB, QB), lambda b, g, i: (g, 0, 0, 0, 0))],
        out_specs=pl.BlockSpec((1, 1, Hg, Dh, QB), lambda b, g, i: (b, g, 0, 0, i)),
        out_shape=jax.ShapeDtypeStruct((B, G, Hg, Dh, S), jnp.float32),
        scratch_shapes=[pltpu.VMEM((Hg, Dh, QB), jnp.float32)],
        compiler_params=pltpu.CompilerParams(dimension_semantics=("arbitrary",) * 3,
                                             vmem_limit_bytes=VMEM_LIMIT),
        name="sel_attention",
    )(q_t, ks, vs_t, mask_t, bias_tiles)


def _sel_bias_tiles(rel_bias, seq):
    nd = -(-MAX_DISTANCE // Q_BLOCK)
    nd = min(nd, seq // Q_BLOCK)
    half = NUM_BUCKETS // 2
    min_far = nd * Q_BLOCK - (Q_BLOCK - 1)
    assert half + math.log(min_far / half) / math.log(MAX_DISTANCE / half) * half >= NUM_BUCKETS - 0.75
    kj = np.arange(Q_BLOCK)[:, None]
    qi = np.arange(Q_BLOCK)[None, :]
    dist = np.arange(nd + 1)[:, None, None] * Q_BLOCK + qi - kj
    dist[nd] = max(seq - 1, nd * Q_BLOCK)
    tiles = _bias_from_buckets(rel_bias, _t5_bucket(jnp.asarray(np.maximum(dist, 0))))
    causal = jnp.asarray(dist >= 0)
    return jnp.where(causal, tiles, NEG_INF)


SEL_KEY_TILE = 256


def _flash_update(lgs, v_ts, ms, ls, acc_ref):
    H = range(len(lgs))
    m_new = []
    for h in H:
        m = ms[h]
        for lg in lgs[h]:
            m = jnp.maximum(m, jnp.max(lg, axis=0, keepdims=True))
        m_new.append(m)
    ps = [[jnp.exp(lg - m_new[h]) for lg in lgs[h]] for h in H]
    alpha = [jnp.exp(ms[h] - m_new[h]) for h in H]
    l_new = []
    for h in H:
        l = alpha[h] * ls[h]
        for p in ps[h]:
            l = l + jnp.sum(p, axis=0, keepdims=True)
        l_new.append(l)
    pv = [[jnp.dot(v_t, p.astype(jnp.bfloat16), preferred_element_type=jnp.float32)
           for p, v_t in zip(ps[h], v_ts)] for h in H]
    for h in H:
        acc = alpha[h] * acc_ref[h]
        for o in pv[h]:
            acc = acc + o
        acc_ref[h] = acc
    return tuple(m_new), tuple(l_new)


def _nsa_body(n_sel, n_cmp, n_delta, q_ref, kc_ref, vc_ref, ks_ref, vs_ref, kw_ref, vw_ref, gl_ref,
              bc_ref, ts_ref, tw_ref, c2s_ref, o_ref, acc_s, acc_w, ocmp_ref):
    i = pl.program_id(2)
    hg, dh, qb = q_ref.shape[2], q_ref.shape[3], q_ref.shape[4]
    ncp = kc_ref.shape[2]
    ns = c2s_ref.shape[0]
    f32, bf16 = jnp.float32, jnp.bfloat16
    H = range(hg)

    cidx = lax.broadcasted_iota(jnp.int32, (ncp, qb), 0)
    tpos = lax.broadcasted_iota(jnp.int32, (ncp, qb), 1) + i * qb
    valid = (tpos - (cidx * CMP_STRIDE + CMP_BLOCK - 1)) >= 0
    fill = jnp.where(cidx < n_cmp, NEG_INF, 2.0 * NEG_INF)
    kc = kc_ref[0, 0]
    vc_t = vc_ref[0, 0]
    psum = jnp.zeros((ncp, qb), f32)
    for h in H:
        lg = jnp.dot(kc, q_ref[0, 0, h], preferred_element_type=f32) + bc_ref[0, h]
        lg = jnp.where(valid, lg, fill)
        m = jnp.max(lg, axis=0, keepdims=True)
        p = jnp.exp(lg - m)
        inv = 1.0 / jnp.sum(p, axis=0, keepdims=True)
        p = jnp.where(valid, p * inv, 0.0)
        psum = psum + p
        ocmp_ref[h] = jnp.dot(vc_t, p.astype(bf16), preferred_element_type=f32)

    score = jnp.dot(c2s_ref[...], psum.astype(bf16), preferred_element_type=f32)
    jblk = lax.broadcasted_iota(jnp.int32, (ns, qb), 0)
    cur = (lax.broadcasted_iota(jnp.int32, (ns, qb), 1) + i * qb) // SEL_BLOCK
    forced = (jblk == 0) | (jblk == cur) | (jblk == cur - 1)
    score = jnp.where(forced, FORCED_SCORE, jnp.where(jblk <= cur, score, -1.0))
    cnt = jnp.zeros((ns, qb), f32)
    for jp in range(ns):
        row = score[jp:jp + 1, :]
        tie = jnp.where(jblk > jp, 1.0, 0.0)
        cnt = cnt + jnp.where(row > score, 1.0, jnp.where(row == score, tie, 0.0))
    maskneg = jnp.where(cnt < n_sel, 0.0, NEG_INF).astype(bf16)
    q_aug = [jnp.concatenate([q_ref[0, 0, h], maskneg], axis=0) for h in H]

    kt = SEL_KEY_TILE
    per = kt // qb
    last_tile = i // per
    acc_s[...] = jnp.zeros_like(acc_s)

    def sel_step(jj, carry):
        ms, ls = carry
        ks_, vs_, tiles = [], [], []
        for u in range(2):
            j = 2 * jj + u
            off = pl.multiple_of(jnp.minimum(j, last_tile) * kt, kt)
            ks_.append(ks_ref[0, 0, pl.ds(off, kt), :])
            vs_.append(vs_ref[0, 0, :, pl.ds(off, kt)])
            tiles.append([jnp.clip(i - (j * per + w), -1, n_delta) + 1 for w in range(per)])
        qk = [[jnp.dot(ks_[u], q_aug[h], preferred_element_type=f32) for u in range(2)] for h in H]
        lgs = [[qk[h][u] + jnp.concatenate([ts_ref[0, h, tiles[u][w]] for w in range(per)], axis=0)
                for u in range(2)] for h in H]
        return _flash_update(lgs, vs_, ms, ls, acc_s)

    init = (tuple(jnp.full((1, qb), NEG_INF, f32) for _ in H), tuple(jnp.zeros((1, qb), f32) for _ in H))
    _, ls_s = lax.fori_loop(0, last_tile // 2 + 1, sel_step, init)

    acc_w[...] = jnp.zeros_like(acc_w)
    kws, vws, widx = [], [], []
    for u in range(N_BAND):
        j = i - (N_BAND - 1) + u
        off = pl.multiple_of(jnp.maximum(j, 0) * qb, qb)
        kws.append(kw_ref[0, 0, pl.ds(off, qb), :])
        vws.append(vw_ref[0, 0, :, pl.ds(off, qb)])
        widx.append(jnp.where(j >= 0, N_BAND - 1 - u, N_BAND))
    qk = [[jnp.dot(kws[u], q_ref[0, 0, h], preferred_element_type=f32) for u in range(N_BAND)] for h in H]
    lgs = [[qk[h][u] + tw_ref[0, h, widx[u]] for u in range(N_BAND)] for h in H]
    _, ls_w = _flash_update(lgs, vws, init[0], init[1], acc_w)

    gates = jax.nn.sigmoid(gl_ref[0, 0])
    for h in H:
        o_ref[0, 0, h] = (gates[3 * h:3 * h + 1] * ocmp_ref[h]
                          + gates[3 * h + 1:3 * h + 2] * (acc_s[h] / ls_s[h])
                          + gates[3 * h + 2:3 * h + 3] * (acc_w[h] / ls_w[h]))


def _nsa_fused(q_t, kc, vc_t, ks, vs_t, kw, vw_t, gl_t, bias_cmp_t, sel_tiles, win_tiles, c2s_t, n_sel, n_cmp):
    B, G, Hg, Dh, S = q_t.shape
    NCP = kc.shape[2]
    NS = c2s_t.shape[0]
    QB = Q_BLOCK
    ND = sel_tiles.shape[2] - 2
    assert S % SEL_KEY_TILE == 0 and SEL_KEY_TILE % QB == 0
    bg = lambda *blk: pl.BlockSpec((1, 1) + blk, lambda b, g, i: (b, g) + (0,) * len(blk))
    gt = lambda arr: pl.BlockSpec((1,) + arr.shape[1:], lambda b, g, i: (g,) + (0,) * (arr.ndim - 1))
    return pl.pallas_call(
        functools.partial(_nsa_body, float(n_sel), n_cmp, ND),
        grid=(B, G, S // QB),
        in_specs=[pl.BlockSpec((1, 1, Hg, Dh, QB), lambda b, g, i: (b, g, 0, 0, i)),
                  bg(NCP, Dh), bg(Dh, NCP), bg(S, Dh + NS), bg(Dh, S), bg(S, Dh), bg(Dh, S),
                  pl.BlockSpec((1, 1, 3 * Hg, QB), lambda b, g, i: (b, g, 0, i)),
                  pl.BlockSpec((1, Hg, NCP, QB), lambda b, g, i: (g, 0, 0, i)),
                  gt(sel_tiles), gt(win_tiles),
                  pl.BlockSpec(c2s_t.shape, lambda b, g, i: (0, 0))],
        out_specs=pl.BlockSpec((1, 1, Hg, Dh, QB), lambda b, g, i: (b, g, 0, 0, i)),
        out_shape=jax.ShapeDtypeStruct((B, G, Hg, Dh, S), jnp.float32),
        scratch_shapes=[pltpu.VMEM((Hg, Dh, QB), jnp.float32)] * 3,
        compiler_params=pltpu.CompilerParams(dimension_semantics=("arbitrary",) * 3,
                                             vmem_limit_bytes=VMEM_LIMIT),
        name="nsa_fused",
    )(q_t, kc, vc_t, ks, vs_t, kw, vw_t, gl_t, bias_cmp_t, sel_tiles, win_tiles, c2s_t)


def _nsa_tables(rel_bias, seq):
    n_cmp = seq // CMP_STRIDE - CMP_BLOCK // CMP_STRIDE + 1
    ncp = -(-n_cmp // 128) * 128
    c = jnp.arange(ncp)[:, None]
    t = jnp.arange(seq)[None, :]
    bias_cmp_t = _bias_from_buckets(rel_bias, _t5_bucket(t - (c * CMP_STRIDE + CMP_BLOCK - 1)))
    base = _sel_bias_tiles(rel_bias, seq)
    sel_tiles = jnp.concatenate([jnp.full_like(base[:, :, :1], NEG_INF), base], axis=2)
    kj = np.arange(Q_BLOCK)[:, None]
    qi = np.arange(Q_BLOCK)[None, :]
    dwin = np.arange(N_BAND)[:, None, None] * Q_BLOCK + qi - kj
    nwin = min(N_BAND, base.shape[2])
    win_tiles = jnp.where(jnp.asarray(dwin[:nwin] < WINDOW), base[:, :, :nwin], NEG_INF)
    win_tiles = jnp.concatenate(
        [win_tiles, jnp.full(win_tiles.shape[:2] + (N_BAND + 1 - nwin,) + win_tiles.shape[3:], NEG_INF)], axis=2)
    n_sel_blocks = seq // SEL_BLOCK
    cs = np.arange(ncp)[None, :] * CMP_STRIDE
    ss = np.arange(n_sel_blocks)[:, None] * SEL_BLOCK
    overlap = np.clip(np.minimum(cs + CMP_BLOCK, ss + SEL_BLOCK) - np.maximum(cs, ss), 0, None)
    overlap = np.where(np.arange(ncp)[None, :] < n_cmp, overlap, 0)
    c2s_t = jnp.asarray(overlap.astype(np.float32) / CMP_BLOCK, jnp.bfloat16)
    return bias_cmp_t, sel_tiles, win_tiles, c2s_t, n_cmp


RWKV_CHUNK = 64
RWKV_INV_BASE = 8


def _split_bf16(x):
    hi = x.astype(jnp.bfloat16)
    lo = (x - hi.astype(jnp.float32)).astype(jnp.bfloat16)
    return hi, lo


def _dot3(a, b, dims=(((1,), (0,)), ((), ()))):
    a_hi, a_lo = _split_bf16(a)
    b_hi, b_lo = _split_bf16(b)
    d = functools.partial(lax.dot_general, dimension_numbers=dims, preferred_element_type=jnp.float32)
    return d(a_hi, b_hi) + (d(a_lo, b_hi) + d(a_hi, b_lo))


_NT = (((1,), (1,)), ((), ()))
_TN = (((0,), (0,)), ((), ()))


def _rwkv_chunk(r_w, lw, k_w, v_w, kk, lr, st_ref, n_heads):
    C = r_w.shape[0]
    N = r_w.shape[1] // n_heads
    H = range(n_heads)
    f32 = jnp.float32
    row = lax.broadcasted_iota(jnp.int32, (C, C), 0)
    col = lax.broadcasted_iota(jnp.int32, (C, C), 1)
    strict = col < row
    incl = col <= row
    eye = jnp.where(row == col, 1.0, 0.0).astype(f32)
    tri = jnp.where(incl, 1.0, 0.0).astype(jnp.bfloat16)

    cl = _dot_exact_lhs(tri, lw)

    base = RWKV_INV_BASE
    diag_blk = strict & ((row // base) == (col // base))
    level_masks = []
    s = base
    while s < C:
        level_masks.append(strict & ((row // (2 * s)) == (col // (2 * s))) & ((row // s) != (col // s)))
        s *= 2

    cl_end = cl[C - 1:C, :]
    bb = kk * lr
    g_inv = jnp.exp(-cl)
    g_rem = jnp.exp(cl_end - cl)
    g_end = jnp.exp(cl_end)
    abar_w = -kk * jnp.exp(cl - lw)
    rbar_w = r_w * jnp.exp(cl)
    bbar_w = bb * g_inv
    kbar_w = k_w * g_inv
    bhat_w = bb * g_rem
    khat_w = k_w * g_rem
    hs = lambda x, h: x[:, h * N:(h + 1) * N]

    abar = [hs(abar_w, h) for h in H]
    rbar = [hs(rbar_w, h) for h in H]
    v = [hs(v_w, h) for h in H]
    gmat = [_dot3(jnp.concatenate([abar[h], rbar[h]], axis=0),
                  jnp.concatenate([hs(bbar_w, h), hs(kbar_w, h)], axis=0), _NT) for h in H]
    a_ab = [jnp.where(strict, gmat[h][:C, :C], 0.0) for h in H]
    a_ak = [jnp.where(strict, gmat[h][:C, C:], 0.0) for h in H]
    m_rb = [jnp.where(incl, gmat[h][C:, :C], 0.0) for h in H]
    m_rk = [jnp.where(incl, gmat[h][C:, C:], 0.0) for h in H]
    akv = [_dot3(a_ak[h], v[h]) for h in H]
    dp = [jnp.where(diag_blk, a_ab[h], 0.0) for h in H]
    x = [eye + dp[h] for h in H]
    s = 2
    while s < base:
        dp = [_dot3(dp[h], dp[h]) for h in H]
        x = [x[h] + _dot3(x[h], dp[h]) for h in H]
        s *= 2
    for lm in level_masks:
        t = [_dot3(jnp.where(lm, a_ab[h], 0.0), x[h]) for h in H]
        x = [x[h] + _dot3(x[h], t[h]) for h in H]
    xw = [_dot3(x[h], jnp.concatenate([akv[h], abar[h]], axis=1)) for h in H]
    uv = [jnp.concatenate([xw[h][:, :N], v[h]], axis=0) for h in H]
    atil = [xw[h][:, N:] for h in H]
    y_loc = [_dot3(jnp.concatenate([m_rb[h], m_rk[h]], axis=1), uv[h]) for h in H]
    qm = [rbar[h] + _dot3(m_rb[h], atil[h]) for h in H]
    s_loc = [_dot3(jnp.concatenate([hs(bhat_w, h), hs(khat_w, h)], axis=0), uv[h], _TN) for h in H]
    pm = [eye[:N, :N] * hs(g_end, h) + _dot3(hs(bhat_w, h), atil[h], _TN) for h in H]
    s0 = [st_ref[h] for h in H]
    y = [y_loc[h] + _dot3(qm[h], s0[h]) for h in H]
    for h in H:
        st_ref[h] = s_loc[h] + _dot3(pm[h], s0[h])
    return jnp.concatenate(y, axis=1)


def _bf16_pieces(x):
    f32 = jnp.float32
    p1 = x.astype(jnp.bfloat16)
    r1 = x - p1.astype(f32)
    p2 = r1.astype(jnp.bfloat16)
    p3 = (r1 - p2.astype(f32)).astype(jnp.bfloat16)
    return p1, p2, p3


def _dot_exact_lhs(a_bf16, x):
    dd = functools.partial(jnp.dot, preferred_element_type=jnp.float32)
    p1, p2, p3 = _bf16_pieces(x)
    return dd(a_bf16, p1) + (dd(a_bf16, p2) + dd(a_bf16, p3))


def _dot_exact_rhs(x, b_bf16):
    dd = functools.partial(jnp.dot, preferred_element_type=jnp.float32)
    p1, p2, p3 = _bf16_pieces(x)
    return dd(p1, b_bf16) + (dd(p2, b_bf16) + dd(p3, b_bf16))


def _rwkv_body(n_heads, z_ref, mu_ref, w0_ref, w2_ref, a0_ref, a2_ref, g2_ref, kk_ref, ka_ref, rk_ref,
               lng_ref, lnb_ref, y_ref, st_ref, prev_ref):
    c = pl.program_id(1)
    C = z_ref.shape[0]
    W = y_ref.shape[1]
    N = W // n_heads
    f32, bf16 = jnp.float32, jnp.bfloat16

    @pl.when(c == 0)
    def _():
        st_ref[...] = jnp.zeros_like(st_ref)
        prev_ref[...] = jnp.zeros_like(prev_ref)

    z = z_ref[...]
    z_prev = jnp.concatenate([prev_ref[...], z[:C - 1]], axis=0)
    prev_ref[...] = z[C - 1:C]
    zs = z + (z_prev - z) * mu_ref[...]
    o = np.cumsum((0,) + SHIFT_SPLITS)
    r, k, v, wl, al, gl = (zs[:, o[i]:o[i + 1]] for i in range(6))
    dd = functools.partial(jnp.dot, preferred_element_type=f32)
    yw = w0_ref[...] + dd(jnp.tanh(wl).astype(bf16), w2_ref[...])
    logw = -(jnp.maximum(-yw, 0.0) + jnp.log1p(jnp.exp(-jnp.abs(yw)))) - 0.5
    lw = -jnp.exp(logw)
    lr = jax.nn.sigmoid(a0_ref[...] + dd(al.astype(bf16), a2_ref[...]))
    g = dd(jax.nn.sigmoid(gl).astype(bf16), g2_ref[...])
    hrow = lax.broadcasted_iota(jnp.int32, (W, W), 0) // N
    hcol = lax.broadcasted_iota(jnp.int32, (W, W), 1) // N
    seg = jnp.where(hrow == hcol, 1.0, 0.0).astype(bf16)
    kk = k * kk_ref[...]
    kk = kk / jnp.maximum(jnp.sqrt(_dot_exact_rhs(kk * kk, seg)), 1e-12)
    k = k * (1.0 + (lr - 1.0) * ka_ref[...])
    y = _rwkv_chunk(r, lw, k, v, kk, lr, st_ref, n_heads)
    mu = _dot_exact_rhs(y, seg) * (1.0 / N)
    yc = y - mu
    var = _dot_exact_rhs(yc * yc, seg) * (1.0 / N)
    yn = yc * lax.rsqrt(var + RWKV_GN_EPS) * lng_ref[...] + lnb_ref[...]
    bonus = _dot_exact_rhs(r * k * rk_ref[...], seg) * v
    y_ref[...] = (yn + bonus) * g


def _rwkv_fused(z, batch, seq, shift_mu, w0, w2, a0, a2, g2, k_k, k_a, r_k, ln_g, ln_b):
    H, N, W = N_HEADS_RWKV, HEAD_RWKV, RWKV_WIDTH
    C = min(RWKV_CHUNK, seq)
    assert N <= C and seq % C == 0 and SHIFT_WIDTH % 128 == 0
    nc = seq // C
    bf16 = jnp.bfloat16
    row = lambda a: a.reshape(1, -1).astype(jnp.float32)
    params = [row(shift_mu), row(w0), w2.astype(bf16), row(a0), a2.astype(bf16), g2.astype(bf16),
              row(k_k), row(k_a), row(r_k), row(ln_g), row(ln_b)]
    full = lambda a: pl.BlockSpec(a.shape, lambda b, c: (0, 0))
    return pl.pallas_call(
        functools.partial(_rwkv_body, H),
        grid=(batch, nc),
        in_specs=[pl.BlockSpec((C, SHIFT_WIDTH), lambda b, c: (b * nc + c, 0))] + [full(p) for p in params],
        out_specs=pl.BlockSpec((C, W), lambda b, c: (b * nc + c, 0)),
        out_shape=jax.ShapeDtypeStruct((batch * seq, W), jnp.float32),
        scratch_shapes=[pltpu.VMEM((H, N, N), jnp.float32), pltpu.VMEM((1, SHIFT_WIDTH), jnp.float32)],
        compiler_params=pltpu.CompilerParams(dimension_semantics=("arbitrary", "arbitrary")),
        name="rwkv_fused",
    )(z, *params)


def _moe_body(be_ref, x_ref, w1_ref, w3_ref, w2_ref, o_ref):
    bf16 = jnp.bfloat16
    x = x_ref[...].astype(bf16)
    h1 = jnp.dot(x, w1_ref[0, 0].astype(bf16), preferred_element_type=jnp.float32)
    h3 = jnp.dot(x, w3_ref[0, 0].astype(bf16), preferred_element_type=jnp.float32)
    h = (h1 * jax.nn.sigmoid(h1)) * h3
    o_ref[...] = jnp.dot(h.astype(bf16), w2_ref[0, 0].astype(bf16), preferred_element_type=jnp.float32)


def _moe_experts(xs, blk_exp, w1, w3, w2, layer, blk):
    n_rows, D = xs.shape
    De = w1.shape[3]
    grid_spec = pltpu.PrefetchScalarGridSpec(
        num_scalar_prefetch=1,
        grid=(n_rows // blk,),
        in_specs=[pl.BlockSpec((blk, D), lambda i, be: (i, 0)),
                  pl.BlockSpec((1, 1, D, De), lambda i, be: (layer, be[i], 0, 0)),
                  pl.BlockSpec((1, 1, D, De), lambda i, be: (layer, be[i], 0, 0)),
                  pl.BlockSpec((1, 1, De, D), lambda i, be: (layer, be[i], 0, 0))],
        out_specs=pl.BlockSpec((blk, D), lambda i, be: (i, 0)))
    return pl.pallas_call(
        _moe_body,
        grid_spec=grid_spec,
        out_shape=jax.ShapeDtypeStruct((n_rows, D), jnp.float32),
        compiler_params=pltpu.CompilerParams(dimension_semantics=("arbitrary",),
                                             vmem_limit_bytes=VMEM_LIMIT),
        name="moe_experts",
    )(blk_exp, xs, w1, w3, w2)


def _split_cols(z, sizes):
    return jnp.split(z, np.cumsum(sizes)[:-1].tolist(), axis=-1)


def _layer_norm(x, g, b, eps=LN_EPS):
    mu = jnp.mean(x, axis=-1, keepdims=True)
    var = jnp.mean(jnp.square(x - mu), axis=-1, keepdims=True)
    return (x - mu) * lax.rsqrt(var + eps) * g + b


def _t5_bucket(dist):
    n = jnp.maximum(dist, 0)
    max_exact = NUM_BUCKETS // 2
    nf = jnp.maximum(n, 1).astype(jnp.float32)
    large = max_exact + (jnp.log(nf / max_exact) / math.log(MAX_DISTANCE / max_exact)
                         * (NUM_BUCKETS - max_exact)).astype(jnp.int32)
    large = jnp.minimum(large, NUM_BUCKETS - 1)
    return jnp.where(n < max_exact, n, large)


def _bias_from_buckets(rel_bias, bucket):
    rb = rel_bias.astype(jnp.float32)
    shape = (N_KV_GROUPS, HEADS_PER_GROUP) + (1,) * bucket.ndim
    out = jnp.zeros((N_KV_GROUPS, HEADS_PER_GROUP) + bucket.shape, jnp.float32)
    for b in range(NUM_BUCKETS):
        out = jnp.where(bucket == b, rb[b].reshape(shape), out)
    return out


def _nsa_positional(rel_bias, seq):
    n_cmp = seq // CMP_STRIDE - CMP_BLOCK // CMP_STRIDE + 1
    n_sel_blocks = seq // SEL_BLOCK
    t = jnp.arange(seq)[:, None]
    c = jnp.arange(n_cmp)[None, :]
    d_cmp = t - (c * CMP_STRIDE + CMP_BLOCK - 1)
    mask_cmp = d_cmp >= 0
    bias_cmp = _bias_from_buckets(rel_bias, _t5_bucket(d_cmp))
    qo = jnp.arange(Q_BLOCK)[:, None]
    m = jnp.arange(WINDOW + Q_BLOCK)[None, :]
    d_win = qo + WINDOW - m
    bias_win = _bias_from_buckets(rel_bias, _t5_bucket(d_win))
    blk = jnp.arange(seq // Q_BLOCK)[:, None, None]
    mask_win = (d_win >= 0) & (d_win < WINDOW) & (blk * Q_BLOCK - WINDOW + m >= 0)
    cs = jnp.arange(n_cmp)[:, None] * CMP_STRIDE
    ss = jnp.arange(n_sel_blocks)[None, :] * SEL_BLOCK
    overlap = jnp.clip(jnp.minimum(cs + CMP_BLOCK, ss + SEL_BLOCK) - jnp.maximum(cs, ss), 0, None)
    cmp_to_sel = overlap.astype(jnp.float32) / CMP_BLOCK
    return bias_cmp, mask_cmp, bias_win, mask_win, cmp_to_sel, _sel_bias_tiles(rel_bias, seq)


def _rwkv7_time_mix(r, k, v, wl, al, gl, w0, w2, a0, a2, g2, k_k, k_a, r_k, ln_g, ln_b):
    B, S, C = r.shape
    H, N = N_HEADS_RWKV, HEAD_RWKV
    f32 = jnp.float32
    logw = -jax.nn.softplus(-(w0 + jnp.tanh(wl) @ w2).astype(f32)) - 0.5
    log_decay = -jnp.exp(logw)
    a = jax.nn.sigmoid((a0 + al @ a2).astype(f32))
    g = jax.nn.sigmoid(gl) @ g2
    heads = lambda t: t.reshape(B, S, H, N)
    kk = heads(k * k_k)
    kk = kk / jnp.maximum(jnp.linalg.norm(kk, axis=-1, keepdims=True), 1e-12)
    k = k * (1.0 + (a - 1.0) * k_a)
    rh, kh, vh = heads(r), heads(k), heads(v)
    y = heads(_rwkv_scan(r, log_decay, k, v, kk.reshape(B, S, C), a, H))
    mu = jnp.mean(y, axis=-1, keepdims=True)
    var = jnp.mean(jnp.square(y - mu), axis=-1, keepdims=True)
    y = ((y - mu) * lax.rsqrt(var + RWKV_GN_EPS)).reshape(B, S, C) * ln_g + ln_b
    bonus = jnp.sum(rh * kh * r_k, axis=-1, keepdims=True) * vh
    return (y + bonus.reshape(B, S, C)) * g


def _compress(t, pe, w1, w2):
    B, S, G, Dh = t.shape
    rep = CMP_BLOCK // CMP_STRIDE
    nc = S // CMP_STRIDE - rep + 1
    sub = t.reshape(B, S // CMP_STRIDE, CMP_STRIDE, G, Dh)
    blk = jnp.concatenate([sub[:, j:j + nc] for j in range(rep)], axis=2)
    blk = blk + pe[:, None, :]
    blk = blk.transpose(0, 1, 3, 2, 4).reshape(B, nc, G, CMP_BLOCK * Dh)
    out = jax.nn.gelu(blk @ w1) @ w2
    return out.transpose(0, 2, 1, 3)


def _nsa_attention(q, k_cmp, v_cmp, k_slc, v_slc, k_win, v_win, gate_logits,
                   pe_k, w1_k, w2_k, pe_v, w1_v, w2_v, rel_bias, tables):
    bias_cmp_t, sel_tiles, win_tiles, c2s_t, n_cmp = tables
    B, S, _ = q.shape
    G, Hg, Dh = N_KV_GROUPS, HEADS_PER_GROUP, HEAD_NSA
    bf16 = jnp.bfloat16
    scale = HEAD_NSA ** -0.5
    assert math.frexp(scale)[0] == 0.5
    rows = lambda t: t.reshape(B, S, G, Dh).transpose(0, 2, 1, 3).astype(bf16)
    cols = lambda t: t.reshape(B, S, G, Dh).transpose(0, 2, 3, 1).astype(bf16)
    ncp = bias_cmp_t.shape[2]
    kc = _compress(k_cmp.reshape(B, S, G, Dh), pe_k, w1_k, w2_k)
    vc = _compress(v_cmp.reshape(B, S, G, Dh), pe_v, w1_v, w2_v)
    pad = ((0, 0), (0, 0), (0, ncp - n_cmp), (0, 0))
    kc_p = jnp.pad(kc, pad).astype(bf16)
    vc_t = jnp.pad(vc, pad).transpose(0, 1, 3, 2).astype(bf16)
    q_t = (q * scale).reshape(B, S, G, Hg, Dh).transpose(0, 2, 3, 4, 1).astype(bf16)
    gl_t = gate_logits.astype(jnp.float32).reshape(B, S, G, Hg * 3).transpose(0, 2, 3, 1)
    n_sel = min(N_SELECT, S // SEL_BLOCK)
    onehot = (jnp.arange(S)[:, None] // SEL_BLOCK == jnp.arange(S // SEL_BLOCK)[None, :]).astype(bf16)
    ks_aug = jnp.concatenate([rows(k_slc), jnp.broadcast_to(onehot, (B, G) + onehot.shape)], axis=-1)
    o_t = _nsa_fused(q_t, kc_p, vc_t, ks_aug, cols(v_slc), rows(k_win), cols(v_win), gl_t,
                     bias_cmp_t, sel_tiles, win_tiles, c2s_t, n_sel, n_cmp)
    return o_t.transpose(0, 4, 1, 2, 3).reshape(B, S, NSA_Q_WIDTH)


def _token_mixer(x, w_in, shift_mu, rw_w0, rw_w2, rw_a0, rw_a2, rw_g2, rw_kk, rw_ka, rw_rk,
                 rw_ln_g, rw_ln_b, cmp_pe_k, cmp_w1_k, cmp_w2_k, cmp_pe_v, cmp_w1_v, cmp_w2_v,
                 w_up_rwkv, w_up_nsa, w_out, ln_g, ln_b, rel_bias, pos):
    B, S, D = x.shape
    n_in = w_in.shape[1]
    n_gate = 3 * N_HEADS_NSA
    g_off = n_in - 2 * D_MODEL - n_gate
    lead_pad = (-g_off) % D_MODEL
    tail_pad = (-(g_off + lead_pad + 2 * D_MODEL + n_gate)) % IN_PROJ_TN
    zeros = lambda n: jnp.zeros((D, n), w_in.dtype)
    w_in_p = jnp.concatenate([w_in[:, :g_off], zeros(lead_pad), w_in[:, g_off + n_gate:],
                              w_in[:, g_off:g_off + n_gate], zeros(tail_pad)], axis=1).astype(jnp.bfloat16)
    gate_blk = (g_off + lead_pad) // D_MODEL
    z = _matmul(x.reshape(B * S, D), w_in_p, IN_PROJ_TM, IN_PROJ_TN)
    z3 = z.reshape(B, S, -1)
    z_rest = z3[..., SHIFT_WIDTH:g_off]
    nsa_g = z3[..., (gate_blk + 2) * D_MODEL:(gate_blk + 2) * D_MODEL + n_gate]
    q, kc, vc, ks, vs, kw, vw = _split_cols(z_rest, REST_SPLITS[:7])
    y_rw = _rwkv_fused(z, B, S, shift_mu, rw_w0, rw_w2, rw_a0, rw_a2, rw_g2,
                       rw_kk, rw_ka, rw_rk, rw_ln_g, rw_ln_b)
    y_nsa = _nsa_attention(q, kc, vc, ks, vs, kw, vw, nsa_g, cmp_pe_k, cmp_w1_k, cmp_w2_k,
                           cmp_pe_v, cmp_w1_v, cmp_w2_v, rel_bias, pos)
    bf16 = jnp.bfloat16
    return _merge_out(y_rw, y_nsa.reshape(B * S, -1), z, gate_blk, x.reshape(B * S, D),
                      w_up_rwkv.astype(bf16), w_up_nsa.astype(bf16), w_out.astype(bf16),
                      ln_g, ln_b).reshape(B, S, D)


def _hier_moe(x, wg, bg, we, be, w1, w3, w2, layer):
    B, S, D = x.shape
    N = B * S
    f32 = jnp.float32
    xf = x.reshape(N, D)
    g_prob = jax.nn.softmax((xf @ wg + bg).astype(f32), axis=-1)
    grp = jnp.argmax(g_prob, axis=-1)
    p_grp = jnp.take_along_axis(g_prob, grp[:, None], axis=1)[:, 0]
    e_logits = (xf @ we + be).astype(f32).reshape(N, N_GROUPS, EXPERTS_PER_GROUP)
    e_logits = jnp.take_along_axis(e_logits, grp[:, None, None], axis=1)[:, 0]
    top_v, top_i = lax.top_k(e_logits, TOP_K_INNER)
    top_w = jax.nn.softmax(top_v, axis=-1) * p_grp[:, None]
    eid = (grp[:, None] * EXPERTS_PER_GROUP + top_i).reshape(-1).astype(jnp.int32)
    slot_w = top_w.reshape(-1)
    slot_tok = jnp.repeat(jnp.arange(N, dtype=jnp.int32), TOP_K_INNER)
    n_slots = N * TOP_K_INNER
    order = jnp.argsort(eid)
    eid_s = eid[order]
    counts = jax.ops.segment_sum(jnp.ones_like(eid), eid, num_segments=N_EXPERTS)
    starts = jnp.cumsum(counts) - counts
    pcounts = (counts + MOE_BLOCK - 1) // MOE_BLOCK * MOE_BLOCK
    pends = jnp.cumsum(pcounts)
    pstarts = pends - pcounts
    dest = pstarts[eid_s] + (jnp.arange(n_slots) - starts[eid_s])
    n_rows = n_slots + N_EXPERTS * MOE_BLOCK
    n_blk = n_rows // MOE_BLOCK
    row_tok = jnp.full((n_rows,), N, jnp.int32).at[dest].set(slot_tok[order])
    row_w = jnp.zeros((n_rows,), f32).at[dest].set(slot_w[order])
    blk_exp = jnp.minimum(jnp.sum(jnp.arange(n_blk)[:, None] * MOE_BLOCK >= pends[None, :], axis=1),
                          N_EXPERTS - 1).astype(jnp.int32)
    xpad = jnp.concatenate([xf, jnp.zeros((1, D), xf.dtype)], axis=0)
    xs = xpad[row_tok]
    ys = _moe_experts(xs, blk_exp, w1, w3, w2, layer, MOE_BLOCK)
    out = jnp.zeros((N + 1, D), f32).at[row_tok].add(ys * row_w[:, None])
    return out[:N].reshape(B, S, D)


def kernel(x, rel_bias, w_in, shift_mu, rw_w0, rw_w2, rw_a0, rw_a2, rw_g2, rw_kk, rw_ka, rw_rk,
           rw_ln_g, rw_ln_b, cmp_pe_k, cmp_w1_k, cmp_w2_k, cmp_pe_v, cmp_w1_v, cmp_w2_v,
           w_up_rwkv, w_up_nsa, w_out, ln1_g, ln1_b, router_group_w, router_group_b,
           router_expert_w, router_expert_b, exp_w1, exp_w3, exp_w2, ln2_g, ln2_b):
    pos = _nsa_tables(rel_bias, x.shape[1])
    for l in range(DEPTH):
        x = _token_mixer(x, w_in[l], shift_mu[l], rw_w0[l], rw_w2[l], rw_a0[l], rw_a2[l], rw_g2[l],
                         rw_kk[l], rw_ka[l], rw_rk[l], rw_ln_g[l], rw_ln_b[l],
                         cmp_pe_k[l], cmp_w1_k[l], cmp_w2_k[l], cmp_pe_v[l], cmp_w1_v[l], cmp_w2_v[l],
                         w_up_rwkv[l], w_up_nsa[l], w_out[l], ln1_g[l], ln1_b[l], rel_bias, pos)
        h = _hier_moe(x, router_group_w[l], router_group_b[l], router_expert_w[l], router_expert_b[l],
                      exp_w1, exp_w3, exp_w2, l)
        x = _layer_norm(ALPHA * x + h, ln2_g[l], ln2_b[l])
    return x
```

```python
import functools
import math

import jax
import jax.numpy as jnp
import numpy as np
from jax import lax
from jax.experimental import pallas as pl
from jax.experimental.pallas import tpu as pltpu

D_MODEL = 1024
DEPTH = 4
N_HEADS_RWKV = 8
HEAD_RWKV = 64
RWKV_WIDTH = N_HEADS_RWKV * HEAD_RWKV
LORA_W = 64
LORA_A = 64
LORA_G = 128
RWKV_GN_EPS = 64e-5
N_HEADS_NSA = 8
N_KV_GROUPS = 2
HEADS_PER_GROUP = N_HEADS_NSA // N_KV_GROUPS
HEAD_NSA = 64
NSA_Q_WIDTH = N_HEADS_NSA * HEAD_NSA
NSA_KV_WIDTH = N_KV_GROUPS * HEAD_NSA
CMP_BLOCK = 32
CMP_STRIDE = 16
CMP_HIDDEN = 128
SEL_BLOCK = 64
N_SELECT = 16
WINDOW = 512
Q_BLOCK = 128
N_BAND = WINDOW // Q_BLOCK + 1
NEG_INF = -1e30
FORCED_SCORE = 1e4
NUM_BUCKETS = 32
MAX_DISTANCE = 1024
N_GROUPS = 4
EXPERTS_PER_GROUP = 8
N_EXPERTS = N_GROUPS * EXPERTS_PER_GROUP
TOP_K_INNER = 2
D_EXPERT = 512
MOE_BLOCK = 128
ALPHA = (2 * DEPTH) ** 0.25
LN_EPS = 1e-5
SHIFT_SPLITS = (RWKV_WIDTH, RWKV_WIDTH, RWKV_WIDTH, LORA_W, LORA_A, LORA_G)
SHIFT_WIDTH = 3 * RWKV_WIDTH + LORA_W + LORA_A + LORA_G
REST_SPLITS = (NSA_Q_WIDTH,) + (NSA_KV_WIDTH,) * 6 + (3 * N_HEADS_NSA, D_MODEL, D_MODEL)

VMEM_LIMIT = 48 * 1024 * 1024


IN_PROJ_TM = 1024
IN_PROJ_TN = 768


def _mm_body(x_ref, w_ref, o_ref, xb_ref):
    @pl.when(pl.program_id(1) == 0)
    def _():
        xb_ref[...] = x_ref[...].astype(jnp.bfloat16)

    o_ref[...] = jnp.dot(xb_ref[...], w_ref[...], preferred_element_type=jnp.float32)


def _matmul(x, w, tm, tn):
    m, k = x.shape
    n = w.shape[1]
    assert m % tm == 0 and n % tn == 0
    return pl.pallas_call(
        _mm_body,
        grid=(m // tm, n // tn),
        in_specs=[pl.BlockSpec((tm, k), lambda i, j: (i, 0)),
                  pl.BlockSpec((k, tn), lambda i, j: (0, j))],
        out_specs=pl.BlockSpec((tm, tn), lambda i, j: (i, j)),
        out_shape=jax.ShapeDtypeStruct((m, n), jnp.float32),
        scratch_shapes=[pltpu.VMEM((tm, k), jnp.bfloat16)],
        compiler_params=pltpu.CompilerParams(dimension_semantics=("arbitrary", "arbitrary"),
                                             vmem_limit_bytes=VMEM_LIMIT),
        name="in_proj",
    )(x, w)


MERGE_TM = 256


def _merge_body(yr_ref, yn_ref, grw_ref, gns_ref, x_ref, wur_ref, wun_ref, wo_ref, g_ref, b_ref, o_ref):
    f32, bf16 = jnp.float32, jnp.bfloat16
    up_r = jnp.dot(yr_ref[...].astype(bf16), wur_ref[...], preferred_element_type=f32)
    up_n = lax.dot_general(yn_ref[0].astype(bf16), wun_ref[...], _TN, preferred_element_type=f32)
    merged = jax.nn.sigmoid(grw_ref[...]) * up_r + jax.nn.sigmoid(gns_ref[...]) * up_n
    y = ALPHA * x_ref[...] + jnp.dot(merged.astype(bf16), wo_ref[...], preferred_element_type=f32)
    mu = jnp.mean(y, axis=-1, keepdims=True)
    var = jnp.mean(jnp.square(y - mu), axis=-1, keepdims=True)
    o_ref[...] = (y - mu) * lax.rsqrt(var + LN_EPS) * g_ref[...] + b_ref[...]


def _merge_out(y_rw, y_nsa_t, z, gate_blk, x, w_up_rwkv, w_up_nsa, w_out, ln_g, ln_b):
    m, d = x.shape
    tm = MERGE_TM
    nb, wn, seq = y_nsa_t.shape
    assert m % tm == 0 and seq % tm == 0
    per_b = seq // tm
    row = lambda width: pl.BlockSpec((tm, width), lambda i: (i, 0))
    full = lambda a: pl.BlockSpec(a.shape, lambda i: (0,) * a.ndim)
    ln_g, ln_b = ln_g.reshape(1, d), ln_b.reshape(1, d)
    return pl.pallas_call(
        _merge_body,
        grid=(m // tm,),
        in_specs=[row(y_rw.shape[1]), pl.BlockSpec((1, wn, tm), lambda i: (i // per_b, 0, i % per_b)),
                  pl.BlockSpec((tm, d), lambda i: (i, gate_blk)),
                  pl.BlockSpec((tm, d), lambda i: (i, gate_blk + 1)),
                  row(d), full(w_up_rwkv), full(w_up_nsa), full(w_out), full(ln_g), full(ln_b)],
        out_specs=row(d),
        out_shape=jax.ShapeDtypeStruct((m, d), jnp.float32),
        compiler_params=pltpu.CompilerParams(dimension_semantics=("arbitrary",),
                                             vmem_limit_bytes=VMEM_LIMIT),
        name="merge_out",
    )(y_rw, y_nsa_t, z, z, x, w_up_rwkv, w_up_nsa, w_out, ln_g, ln_b)


def _topk_mask_body(n_sel, s_ref, o_ref):
    s = s_ref[0, 0]
    ns = s.shape[0]
    jidx = lax.broadcasted_iota(jnp.int32, s.shape, 0)
    cnt = jnp.zeros(s.shape, jnp.float32)
    for jp in range(ns):
        row = s[jp:jp + 1, :]
        tie = jnp.where(jidx > jp, 1.0, 0.0)
        cnt = cnt + jnp.where(row > s, 1.0, jnp.where(row == s, tie, 0.0))
    o_ref[0, 0] = jnp.where(cnt < n_sel, 1.0, 0.0)


def _topk_mask(scores_t, n_sel, tq=512):
    B, G, NS, S = scores_t.shape
    tq = min(tq, S)
    return pl.pallas_call(
        functools.partial(_topk_mask_body, float(n_sel)),
        grid=(B, G, S // tq),
        in_specs=[pl.BlockSpec((1, 1, NS, tq), lambda b, g, i: (b, g, 0, i))],
        out_specs=pl.BlockSpec((1, 1, NS, tq), lambda b, g, i: (b, g, 0, i)),
        out_shape=jax.ShapeDtypeStruct((B, G, NS, S), jnp.float32),
        compiler_params=pltpu.CompilerParams(dimension_semantics=("arbitrary",) * 3),
        name="topk_mask",
    )(scores_t)


def _sel_attn_body(n_delta, q_ref, k_ref, v_ref, m_ref, t_ref, o_ref, acc_ref):
    i = pl.program_id(2)
    hg, dh, qb = q_ref.shape[2], q_ref.shape[3], q_ref.shape[4]
    ns = m_ref.shape[2]
    maskf = m_ref[0, 0].astype(jnp.bfloat16)
    key = lax.broadcasted_iota(jnp.int32, (qb, ns), 0)
    blk = lax.broadcasted_iota(jnp.int32, (qb, ns), 1)
    per = qb // SEL_BLOCK
    acc_ref[...] = jnp.zeros_like(acc_ref)

    def body(j, carry):
        ms, ls = carry
        off = pl.multiple_of(j * qb, qb)
        k = k_ref[0, 0, pl.ds(off, qb), :]
        v = v_ref[0, 0, :, pl.ds(off, qb)]
        expand = jnp.where(blk == j * per + key // SEL_BLOCK, 1.0, 0.0).astype(jnp.bfloat16)
        selm = jnp.dot(expand, maskf, preferred_element_type=jnp.float32) > 0.5
        d = jnp.minimum(i - j, n_delta)
        new_ms, new_ls = [], []
        for h in range(hg):
            lg = jnp.dot(k, q_ref[0, 0, h], preferred_element_type=jnp.float32) + t_ref[0, h, d]
            lg = jnp.where(selm, lg, NEG_INF)
            m_new = jnp.maximum(ms[h], jnp.max(lg, axis=0, keepdims=True))
            p = jnp.exp(lg - m_new)
            alpha = jnp.exp(ms[h] - m_new)
            new_ls.append(alpha * ls[h] + jnp.sum(p, axis=0, keepdims=True))
            acc_ref[h] = alpha * acc_ref[h] + jnp.dot(v, p.astype(jnp.bfloat16),
                                                      preferred_element_type=jnp.float32)
            new_ms.append(m_new)
        return tuple(new_ms), tuple(new_ls)

    init = (tuple(jnp.full((1, qb), NEG_INF, jnp.float32) for _ in range(hg)),
            tuple(jnp.zeros((1, qb), jnp.float32) for _ in range(hg)))
    ms, ls = lax.fori_loop(0, i + 1, body, init)
    for h in range(hg):
        o_ref[0, 0, h] = acc_ref[h] / ls[h]


def _sel_attention(q_t, ks, vs_t, mask_t, bias_tiles):
    B, G, Hg, Dh, S = q_t.shape
    NS = mask_t.shape[2]
    ND = bias_tiles.shape[2] - 1
    QB = bias_tiles.shape[-1]
    return pl.pallas_call(
        functools.partial(_sel_attn_body, ND),
        grid=(B, G, S // QB),
        in_specs=[pl.BlockSpec((1, 1, Hg, Dh, QB), lambda b, g, i: (b, g, 0, 0, i)),
                  pl.BlockSpec((1, 1, S, Dh), lambda b, g, i: (b, g, 0, 0)),
                  pl.BlockSpec((1, 1, Dh, S), lambda b, g, i: (b, g, 0, 0)),
                  pl.BlockSpec((1, 1, NS, QB), lambda b, g, i: (b, g, 0, i)),
                  pl.BlockSpec((1, Hg, ND + 1, QB, QB), lambda b, g, i: (g, 0, 0, 0, 0))],
        out_specs=pl.BlockSpec((1, 1, Hg, Dh, QB), lambda b, g, i: (b, g, 0, 0, i)),
        out_shape=jax.ShapeDtypeStruct((B, G, Hg, Dh, S), jnp.float32),
        scratch_shapes=[pltpu.VMEM((Hg, Dh, QB), jnp.float32)],
        compiler_params=pltpu.CompilerParams(dimension_semantics=("arbitrary",) * 3,
                                             vmem_limit_bytes=VMEM_LIMIT),
        name="sel_attention",
    )(q_t, ks, vs_t, mask_t, bias_tiles)


def _sel_bias_tiles(rel_bias, seq):
    nd = -(-MAX_DISTANCE // Q_BLOCK)
    nd = min(nd, seq // Q_BLOCK)
    half = NUM_BUCKETS // 2
    min_far = nd * Q_BLOCK - (Q_BLOCK - 1)
    assert half + math.log(min_far / half) / math.log(MAX_DISTANCE / half) * half >= NUM_BUCKETS - 0.75
    kj = np.arange(Q_BLOCK)[:, None]
    qi = np.arange(Q_BLOCK)[None, :]
    dist = np.arange(nd + 1)[:, None, None] * Q_BLOCK + qi - kj
    dist[nd] = max(seq - 1, nd * Q_BLOCK)
    tiles = _bias_from_buckets(rel_bias, _t5_bucket(jnp.asarray(np.maximum(dist, 0))))
    causal = jnp.asarray(dist >= 0)
    return jnp.where(causal, tiles, NEG_INF)


SEL_KEY_TILE = 256


def _flash_update(lgs, v_ts, ms, ls, acc_ref):
    H = range(len(lgs))
    m_new = []
    for h in H:
        m = ms[h]
        for lg in lgs[h]:
            m = jnp.maximum(m, jnp.max(lg, axis=0, keepdims=True))
        m_new.append(m)
    ps = [[jnp.exp(lg - m_new[h]) for lg in lgs[h]] for h in H]
    alpha = [jnp.exp(ms[h] - m_new[h]) for h in H]
    l_new = []
    for h in H:
        l = alpha[h] * ls[h]
        for p in ps[h]:
            l = l + jnp.sum(p, axis=0, keepdims=True)
        l_new.append(l)
    pv = [[jnp.dot(v_t, p.astype(jnp.bfloat16), preferred_element_type=jnp.float32)
           for p, v_t in zip(ps[h], v_ts)] for h in H]
    for h in H:
        acc = alpha[h] * acc_ref[h]
        for o in pv[h]:
            acc = acc + o
        acc_ref[h] = acc
    return tuple(m_new), tuple(l_new)


def _nsa_body(n_sel, n_cmp, n_delta, q_ref, kc_ref, vc_ref, ks_ref, vs_ref, kw_ref, vw_ref, gl_ref,
              bc_ref, ts_ref, tw_ref, c2s_ref, o_ref, acc_s, acc_w, ocmp_ref):
    i = pl.program_id(2)
    hg, dh, qb = q_ref.shape[2], q_ref.shape[3], q_ref.shape[4]
    ncp = kc_ref.shape[2]
    ns = c2s_ref.shape[0]
    f32, bf16 = jnp.float32, jnp.bfloat16
    H = range(hg)

    cidx = lax.broadcasted_iota(jnp.int32, (ncp, qb), 0)
    tpos = lax.broadcasted_iota(jnp.int32, (ncp, qb), 1) + i * qb
    valid = (tpos - (cidx * CMP_STRIDE + CMP_BLOCK - 1)) >= 0
    fill = jnp.where(cidx < n_cmp, NEG_INF, 2.0 * NEG_INF)
    kc = kc_ref[0, 0]
    vc_t = vc_ref[0, 0]
    psum = jnp.zeros((ncp, qb), f32)
    for h in H:
        lg = jnp.dot(kc, q_ref[0, 0, h], preferred_element_type=f32) + bc_ref[0, h]
        lg = jnp.where(valid, lg, fill)
        m = jnp.max(lg, axis=0, keepdims=True)
        p = jnp.exp(lg - m)
        inv = 1.0 / jnp.sum(p, axis=0, keepdims=True)
        p = jnp.where(valid, p * inv, 0.0)
        psum = psum + p
        ocmp_ref[h] = jnp.dot(vc_t, p.astype(bf16), preferred_element_type=f32)

    score = jnp.dot(c2s_ref[...], psum.astype(bf16), preferred_element_type=f32)
    jblk = lax.broadcasted_iota(jnp.int32, (ns, qb), 0)
    cur = (lax.broadcasted_iota(jnp.int32, (ns, qb), 1) + i * qb) // SEL_BLOCK
    forced = (jblk == 0) | (jblk == cur) | (jblk == cur - 1)
    score = jnp.where(forced, FORCED_SCORE, jnp.where(jblk <= cur, score, -1.0))
    cnt = jnp.zeros((ns, qb), f32)
    for jp in range(ns):
        row = score[jp:jp + 1, :]
        tie = jnp.where(jblk > jp, 1.0, 0.0)
        cnt = cnt + jnp.where(row > score, 1.0, jnp.where(row == score, tie, 0.0))
    maskneg = jnp.where(cnt < n_sel, 0.0, NEG_INF).astype(bf16)
    q_aug = [jnp.concatenate([q_ref[0, 0, h], maskneg], axis=0) for h in H]

    kt = SEL_KEY_TILE
    per = kt // qb
    last_tile = i // per
    acc_s[...] = jnp.zeros_like(acc_s)

    def sel_step(jj, carry):
        ms, ls = carry
        ks_, vs_, tiles = [], [], []
        for u in range(2):
            j = 2 * jj + u
            off = pl.multiple_of(jnp.minimum(j, last_tile) * kt, kt)
            ks_.append(ks_ref[0, 0, pl.ds(off, kt), :])
            vs_.append(vs_ref[0, 0, :, pl.ds(off, kt)])
            tiles.append([jnp.clip(i - (j * per + w), -1, n_delta) + 1 for w in range(per)])
        qk = [[jnp.dot(ks_[u], q_aug[h], preferred_element_type=f32) for u in range(2)] for h in H]
        lgs = [[qk[h][u] + jnp.concatenate([ts_ref[0, h, tiles[u][w]] for w in range(per)], axis=0)
                for u in range(2)] for h in H]
        return _flash_update(lgs, vs_, ms, ls, acc_s)

    init = (tuple(jnp.full((1, qb), NEG_INF, f32) for _ in H), tuple(jnp.zeros((1, qb), f32) for _ in H))
    _, ls_s = lax.fori_loop(0, last_tile // 2 + 1, sel_step, init)

    acc_w[...] = jnp.zeros_like(acc_w)
    kws, vws, widx = [], [], []
    for u in range(N_BAND):
        j = i - (N_BAND - 1) + u
        off = pl.multiple_of(jnp.maximum(j, 0) * qb, qb)
        kws.append(kw_ref[0, 0, pl.ds(off, qb), :])
        vws.append(vw_ref[0, 0, :, pl.ds(off, qb)])
        widx.append(jnp.where(j >= 0, N_BAND - 1 - u, N_BAND))
    qk = [[jnp.dot(kws[u], q_ref[0, 0, h], preferred_element_type=f32) for u in range(N_BAND)] for h in H]
    lgs = [[qk[h][u] + tw_ref[0, h, widx[u]] for u in range(N_BAND)] for h in H]
    _, ls_w = _flash_update(lgs, vws, init[0], init[1], acc_w)

    gates = jax.nn.sigmoid(gl_ref[0, 0])
    for h in H:
        o_ref[0, 0, h] = (gates[3 * h:3 * h + 1] * ocmp_ref[h]
                          + gates[3 * h + 1:3 * h + 2] * (acc_s[h] / ls_s[h])
                          + gates[3 * h + 2:3 * h + 3] * (acc_w[h] / ls_w[h]))


def _nsa_fused(q_t, kc, vc_t, ks, vs_t, kw, vw_t, gl_t, bias_cmp_t, sel_tiles, win_tiles, c2s_t, n_sel, n_cmp):
    B, G, Hg, Dh, S = q_t.shape
    NCP = kc.shape[2]
    NS = c2s_t.shape[0]
    QB = Q_BLOCK
    ND = sel_tiles.shape[2] - 2
    assert S % SEL_KEY_TILE == 0 and SEL_KEY_TILE % QB == 0
    bg = lambda *blk: pl.BlockSpec((1, 1) + blk, lambda b, g, i: (b, g) + (0,) * len(blk))
    gt = lambda arr: pl.BlockSpec((1,) + arr.shape[1:], lambda b, g, i: (g,) + (0,) * (arr.ndim - 1))
    return pl.pallas_call(
        functools.partial(_nsa_body, float(n_sel), n_cmp, ND),
        grid=(B, G, S // QB),
        in_specs=[pl.BlockSpec((1, 1, Hg, Dh, QB), lambda b, g, i: (b, g, 0, 0, i)),
                  bg(NCP, Dh), bg(Dh, NCP), bg(S, Dh + NS), bg(Dh, S), bg(S, Dh), bg(Dh, S),
                  pl.BlockSpec((1, 1, 3 * Hg, QB), lambda b, g, i: (b, g, 0, i)),
                  pl.BlockSpec((1, Hg, NCP, QB), lambda b, g, i: (g, 0, 0, i)),
                  gt(sel_tiles), gt(win_tiles),
                  pl.BlockSpec(c2s_t.shape, lambda b, g, i: (0, 0))],
        out_specs=pl.BlockSpec((1, 1, Hg, Dh, QB), lambda b, g, i: (b, g, 0, 0, i)),
        out_shape=jax.ShapeDtypeStruct((B, G, Hg, Dh, S), jnp.float32),
        scratch_shapes=[pltpu.VMEM((Hg, Dh, QB), jnp.float32)] * 3,
        compiler_params=pltpu.CompilerParams(dimension_semantics=("arbitrary",) * 3,
                                             vmem_limit_bytes=VMEM_LIMIT),
        name="nsa_fused",
    )(q_t, kc, vc_t, ks, vs_t, kw, vw_t, gl_t, bias_cmp_t, sel_tiles, win_tiles, c2s_t)


def _nsa_tables(rel_bias, seq):
    n_cmp = seq // CMP_STRIDE - CMP_BLOCK // CMP_STRIDE + 1
    ncp = -(-n_cmp // 128) * 128
    c = jnp.arange(ncp)[:, None]
    t = jnp.arange(seq)[None, :]
    bias_cmp_t = _bias_from_buckets(rel_bias, _t5_bucket(t - (c * CMP_STRIDE + CMP_BLOCK - 1)))
    base = _sel_bias_tiles(rel_bias, seq)
    sel_tiles = jnp.concatenate([jnp.full_like(base[:, :, :1], NEG_INF), base], axis=2)
    kj = np.arange(Q_BLOCK)[:, None]
    qi = np.arange(Q_BLOCK)[None, :]
    dwin = np.arange(N_BAND)[:, None, None] * Q_BLOCK + qi - kj
    nwin = min(N_BAND, base.shape[2])
    win_tiles = jnp.where(jnp.asarray(dwin[:nwin] < WINDOW), base[:, :, :nwin], NEG_INF)
    win_tiles = jnp.concatenate(
        [win_tiles, jnp.full(win_tiles.shape[:2] + (N_BAND + 1 - nwin,) + win_tiles.shape[3:], NEG_INF)], axis=2)
    n_sel_blocks = seq // SEL_BLOCK
    cs = np.arange(ncp)[None, :] * CMP_STRIDE
    ss = np.arange(n_sel_blocks)[:, None] * SEL_BLOCK
    overlap = np.clip(np.minimum(cs + CMP_BLOCK, ss + SEL_BLOCK) - np.maximum(cs, ss), 0, None)
    overlap = np.where(np.arange(ncp)[None, :] < n_cmp, overlap, 0)
    c2s_t = jnp.asarray(overlap.astype(np.float32) / CMP_BLOCK, jnp.bfloat16)
    return bias_cmp_t, sel_tiles, win_tiles, c2s_t, n_cmp


RWKV_CHUNK = 64
RWKV_INV_BASE = 8


def _split_bf16(x):
    hi = x.astype(jnp.bfloat16)
    lo = (x - hi.astype(jnp.float32)).astype(jnp.bfloat16)
    return hi, lo


def _dot3(a, b, dims=(((1,), (0,)), ((), ()))):
    a_hi, a_lo = _split_bf16(a)
    b_hi, b_lo = _split_bf16(b)
    d = functools.partial(lax.dot_general, dimension_numbers=dims, preferred_element_type=jnp.float32)
    return d(a_hi, b_hi) + (d(a_lo, b_hi) + d(a_hi, b_lo))


_NT = (((1,), (1,)), ((), ()))
_TN = (((0,), (0,)), ((), ()))


def _rwkv_chunk(r_w, lw, k_w, v_w, kk, lr, st_ref, n_heads):
    C = r_w.shape[0]
    N = r_w.shape[1] // n_heads
    H = range(n_heads)
    f32 = jnp.float32
    row = lax.broadcasted_iota(jnp.int32, (C, C), 0)
    col = lax.broadcasted_iota(jnp.int32, (C, C), 1)
    strict = col < row
    incl = col <= row
    eye = jnp.where(row == col, 1.0, 0.0).astype(f32)
    tri = jnp.where(incl, 1.0, 0.0).astype(jnp.bfloat16)

    cl = _dot_exact_lhs(tri, lw)

    base = RWKV_INV_BASE
    diag_blk = strict & ((row // base) == (col // base))
    level_masks = []
    s = base
    while s < C:
        level_masks.append(strict & ((row // (2 * s)) == (col // (2 * s))) & ((row // s) != (col // s)))
        s *= 2

    cl_end = cl[C - 1:C, :]
    bb = kk * lr
    g_inv = jnp.exp(-cl)
    g_rem = jnp.exp(cl_end - cl)
    g_end = jnp.exp(cl_end)
    abar_w = -kk * jnp.exp(cl - lw)
    rbar_w = r_w * jnp.exp(cl)
    bbar_w = bb * g_inv
    kbar_w = k_w * g_inv
    bhat_w = bb * g_rem
    khat_w = k_w * g_rem
    hs = lambda x, h: x[:, h * N:(h + 1) * N]

    abar = [hs(abar_w, h) for h in H]
    rbar = [hs(rbar_w, h) for h in H]
    v = [hs(v_w, h) for h in H]
    gmat = [_dot3(jnp.concatenate([abar[h], rbar[h]], axis=0),
                  jnp.concatenate([hs(bbar_w, h), hs(kbar_w, h)], axis=0), _NT) for h in H]
    a_ab = [jnp.where(strict, gmat[h][:C, :C], 0.0) for h in H]
    a_ak = [jnp.where(strict, gmat[h][:C, C:], 0.0) for h in H]
    m_rb = [jnp.where(incl, gmat[h][C:, :C], 0.0) for h in H]
    m_rk = [jnp.where(incl, gmat[h][C:, C:], 0.0) for h in H]
    akv = [_dot3(a_ak[h], v[h]) for h in H]
    dp = [jnp.where(diag_blk, a_ab[h], 0.0) for h in H]
    x = [eye + dp[h] for h in H]
    s = 2
    while s < base:
        dp = [_dot3(dp[h], dp[h]) for h in H]
        x = [x[h] + _dot3(x[h], dp[h]) for h in H]
        s *= 2
    for lm in level_masks:
        t = [_dot3(jnp.where(lm, a_ab[h], 0.0), x[h]) for h in H]
        x = [x[h] + _dot3(x[h], t[h]) for h in H]
    xw = [_dot3(x[h], jnp.concatenate([akv[h], abar[h]], axis=1)) for h in H]
    uv = [jnp.concatenate([xw[h][:, :N], v[h]], axis=0) for h in H]
    atil = [xw[h][:, N:] for h in H]
    y_loc = [_dot3(jnp.concatenate([m_rb[h], m_rk[h]], axis=1), uv[h]) for h in H]
    qm = [rbar[h] + _dot3(m_rb[h], atil[h]) for h in H]
    s_loc = [_dot3(jnp.concatenate([hs(bhat_w, h), hs(khat_w, h)], axis=0), uv[h], _TN) for h in H]
    pm = [eye[:N, :N] * hs(g_end, h) + _dot3(hs(bhat_w, h), atil[h], _TN) for h in H]
    s0 = [st_ref[h] for h in H]
    y = [y_loc[h] + _dot3(qm[h], s0[h]) for h in H]
    for h in H:
        st_ref[h] = s_loc[h] + _dot3(pm[h], s0[h])
    return jnp.concatenate(y, axis=1)


def _bf16_pieces(x):
    f32 = jnp.float32
    p1 = x.astype(jnp.bfloat16)
    r1 = x - p1.astype(f32)
    p2 = r1.astype(jnp.bfloat16)
    p3 = (r1 - p2.astype(f32)).astype(jnp.bfloat16)
    return p1, p2, p3


def _dot_exact_lhs(a_bf16, x):
    dd = functools.partial(jnp.dot, preferred_element_type=jnp.float32)
    p1, p2, p3 = _bf16_pieces(x)
    return dd(a_bf16, p1) + (dd(a_bf16, p2) + dd(a_bf16, p3))


def _dot_exact_rhs(x, b_bf16):
    dd = functools.partial(jnp.dot, preferred_element_type=jnp.float32)
    p1, p2, p3 = _bf16_pieces(x)
    return dd(p1, b_bf16) + (dd(p2, b_bf16) + dd(p3, b_bf16))


def _rwkv_body(n_heads, z_ref, mu_ref, w0_ref, w2_ref, a0_ref, a2_ref, g2_ref, kk_ref, ka_ref, rk_ref,
               lng_ref, lnb_ref, y_ref, st_ref, prev_ref):
    c = pl.program_id(1)
    C = z_ref.shape[0]
    W = y_ref.shape[1]
    N = W // n_heads
    f32, bf16 = jnp.float32, jnp.bfloat16

    @pl.when(c == 0)
    def _():
        st_ref[...] = jnp.zeros_like(st_ref)
        prev_ref[...] = jnp.zeros_like(prev_ref)

    z = z_ref[...]
    z_prev = jnp.concatenate([prev_ref[...], z[:C - 1]], axis=0)
    prev_ref[...] = z[C - 1:C]
    zs = z + (z_prev - z) * mu_ref[...]
    o = np.cumsum((0,) + SHIFT_SPLITS)
    r, k, v, wl, al, gl = (zs[:, o[i]:o[i + 1]] for i in range(6))
    dd = functools.partial(jnp.dot, preferred_element_type=f32)
    yw = w0_ref[...] + dd(jnp.tanh(wl).astype(bf16), w2_ref[...])
    logw = -(jnp.maximum(-yw, 0.0) + jnp.log1p(jnp.exp(-jnp.abs(yw)))) - 0.5
    lw = -jnp.exp(logw)
    lr = jax.nn.sigmoid(a0_ref[...] + dd(al.astype(bf16), a2_ref[...]))
    g = dd(jax.nn.sigmoid(gl).astype(bf16), g2_ref[...])
    hrow = lax.broadcasted_iota(jnp.int32, (W, W), 0) // N
    hcol = lax.broadcasted_iota(jnp.int32, (W, W), 1) // N
    seg = jnp.where(hrow == hcol, 1.0, 0.0).astype(bf16)
    kk = k * kk_ref[...]
    kk = kk / jnp.maximum(jnp.sqrt(_dot_exact_rhs(kk * kk, seg)), 1e-12)
    k = k * (1.0 + (lr - 1.0) * ka_ref[...])
    y = _rwkv_chunk(r, lw, k, v, kk, lr, st_ref, n_heads)
    mu = _dot_exact_rhs(y, seg) * (1.0 / N)
    yc = y - mu
    var = _dot_exact_rhs(yc * yc, seg) * (1.0 / N)
    yn = yc * lax.rsqrt(var + RWKV_GN_EPS) * lng_ref[...] + lnb_ref[...]
    bonus = _dot_exact_rhs(r * k * rk_ref[...], seg) * v
    y_ref[...] = (yn + bonus) * g


def _rwkv_fused(z, batch, seq, shift_mu, w0, w2, a0, a2, g2, k_k, k_a, r_k, ln_g, ln_b):
    H, N, W = N_HEADS_RWKV, HEAD_RWKV, RWKV_WIDTH
    C = min(RWKV_CHUNK, seq)
    assert N <= C and seq % C == 0 and SHIFT_WIDTH % 128 == 0
    nc = seq // C
    bf16 = jnp.bfloat16
    row = lambda a: a.reshape(1, -1).astype(jnp.float32)
    params = [row(shift_mu), row(w0), w2.astype(bf16), row(a0), a2.astype(bf16), g2.astype(bf16),
              row(k_k), row(k_a), row(r_k), row(ln_g), row(ln_b)]
    full = lambda a: pl.BlockSpec(a.shape, lambda b, c: (0, 0))
    return pl.pallas_call(
        functools.partial(_rwkv_body, H),
        grid=(batch, nc),
        in_specs=[pl.BlockSpec((C, SHIFT_WIDTH), lambda b, c: (b * nc + c, 0))] + [full(p) for p in params],
        out_specs=pl.BlockSpec((C, W), lambda b, c: (b * nc + c, 0)),
        out_shape=jax.ShapeDtypeStruct((batch * seq, W), jnp.float32),
        scratch_shapes=[pltpu.VMEM((H, N, N), jnp.float32), pltpu.VMEM((1, SHIFT_WIDTH), jnp.float32)],
        compiler_params=pltpu.CompilerParams(dimension_semantics=("arbitrary", "arbitrary")),
        name="rwkv_fused",
    )(z, *params)


def _moe_body(be_ref, x_ref, w1_ref, w3_ref, w2_ref, o_ref):
    bf16 = jnp.bfloat16
    x = x_ref[...].astype(bf16)
    h1 = jnp.dot(x, w1_ref[0, 0].astype(bf16), preferred_element_type=jnp.float32)
    h3 = jnp.dot(x, w3_ref[0, 0].astype(bf16), preferred_element_type=jnp.float32)
    h = (h1 * jax.nn.sigmoid(h1)) * h3
    o_ref[...] = jnp.dot(h.astype(bf16), w2_ref[0, 0].astype(bf16), preferred_element_type=jnp.float32)


def _moe_experts(xs, blk_exp, w1, w3, w2, layer, blk):
    n_rows, D = xs.shape
    De = w1.shape[3]
    grid_spec = pltpu.PrefetchScalarGridSpec(
        num_scalar_prefetch=1,
        grid=(n_rows // blk,),
        in_specs=[pl.BlockSpec((blk, D), lambda i, be: (i, 0)),
                  pl.BlockSpec((1, 1, D, De), lambda i, be: (layer, be[i], 0, 0)),
                  pl.BlockSpec((1, 1, D, De), lambda i, be: (layer, be[i], 0, 0)),
                  pl.BlockSpec((1, 1, De, D), lambda i, be: (layer, be[i], 0, 0))],
        out_specs=pl.BlockSpec((blk, D), lambda i, be: (i, 0)))
    return pl.pallas_call(
        _moe_body,
        grid_spec=grid_spec,
        out_shape=jax.ShapeDtypeStruct((n_rows, D), jnp.float32),
        compiler_params=pltpu.CompilerParams(dimension_semantics=("arbitrary",),
                                             vmem_limit_bytes=VMEM_LIMIT),
        name="moe_experts",
    )(blk_exp, xs, w1, w3, w2)


MOE_DEST_CHUNK = 512


def _moe_dest_body(n_exp, blk, eid_ref, dest_ref, cnt_ref, run_ref, pst_ref):
    ph, c = pl.program_id(0), pl.program_id(1)
    T = eid_ref.shape[1]
    f32, bf16 = jnp.float32, jnp.bfloat16
    onehot = jnp.where(lax.broadcasted_iota(jnp.int32, (n_exp, T), 0) == eid_ref[...], 1.0, 0.0)
    here = jnp.sum(onehot, axis=1, keepdims=True)

    @pl.when((ph == 0) & (c == 0))
    def _():
        run_ref[...] = jnp.zeros_like(run_ref)

    @pl.when(ph == 0)
    def _():
        run_ref[...] = run_ref[...] + here
        dest_ref[...] = jnp.zeros_like(dest_ref)

    @pl.when((ph == 1) & (c == 0))
    def _():
        counts = run_ref[...]
        cnt_ref[...] = counts
        padded = jnp.floor((counts + (blk - 1)) * (1.0 / blk)) * blk
        er = lax.broadcasted_iota(jnp.int32, (n_exp, n_exp), 0)
        ec = lax.broadcasted_iota(jnp.int32, (n_exp, n_exp), 1)
        pst_ref[...] = _dot_exact_lhs(jnp.where(ec < er, 1.0, 0.0).astype(bf16), padded)
        run_ref[...] = jnp.zeros_like(run_ref)

    @pl.when(ph == 1)
    def _():
        sr = lax.broadcasted_iota(jnp.int32, (T, T), 0)
        sc = lax.broadcasted_iota(jnp.int32, (T, T), 1)
        earlier = jnp.dot(onehot.astype(bf16), jnp.where(sr < sc, 1.0, 0.0).astype(bf16),
                          preferred_element_type=f32)
        base = pst_ref[:, 0:1] + run_ref[:, 0:1]
        dest = jnp.sum(onehot * (earlier + base), axis=0, keepdims=True)
        dest_ref[...] = dest.astype(jnp.int32)
        run_ref[...] = run_ref[...] + here


def _moe_dest(eid_row, n_exp, blk):
    n_slots = eid_row.shape[1]
    T = min(MOE_DEST_CHUNK, n_slots)
    assert n_slots % T == 0 and n_slots + n_exp * blk < 2 ** 24
    dest, cnt = pl.pallas_call(
        functools.partial(_moe_dest_body, n_exp, blk),
        grid=(2, n_slots // T),
        in_specs=[pl.BlockSpec((1, T), lambda p, c: (0, c))],
        out_specs=[pl.BlockSpec((1, T), lambda p, c: (0, c * p)),
                   pl.BlockSpec((n_exp, 128), lambda p, c: (0, 0))],
        out_shape=[jax.ShapeDtypeStruct((1, n_slots), jnp.int32),
                   jax.ShapeDtypeStruct((n_exp, 128), jnp.float32)],
        scratch_shapes=[pltpu.VMEM((n_exp, 128), jnp.float32), pltpu.VMEM((n_exp, 128), jnp.float32)],
        compiler_params=pltpu.CompilerParams(dimension_semantics=("arbitrary", "arbitrary")),
        name="moe_dest",
    )(eid_row)
    return dest, cnt[:, 0].astype(jnp.int32)


MOE_TOKEN_TILE = 256


def _row_copy(src_ref, src_row, dst_ref, dst_row, sem):
    return pltpu.make_async_copy(src_ref.at[pl.ds(src_row, 1)], dst_ref.at[pl.ds(dst_row, 1)], sem)


def _moe_dispatch_body(dest_ref, x_ref, init_ref, xs_ref, sem):
    del init_ref
    n_choice, T = dest_ref.shape

    def start(r, carry):
        for k in range(n_choice):
            _row_copy(x_ref, r, xs_ref, dest_ref[k, r], sem).start()
        return carry

    def wait(r, carry):
        for k in range(n_choice):
            _row_copy(x_ref, 0, xs_ref, 0, sem).wait()
        return carry

    lax.fori_loop(0, T, start, 0)
    lax.fori_loop(0, T, wait, 0)


def _moe_dispatch(x, dest2, n_rows):
    N, D = x.shape
    T = min(MOE_TOKEN_TILE, N)
    assert N % T == 0
    return pl.pallas_call(
        _moe_dispatch_body,
        grid=(N // T,),
        in_specs=[pl.BlockSpec((dest2.shape[0], T), lambda i: (0, i), memory_space=pltpu.SMEM),
                  pl.BlockSpec((T, D), lambda i: (i, 0)),
                  pl.BlockSpec(memory_space=pl.ANY)],
        out_specs=pl.BlockSpec(memory_space=pl.ANY),
        out_shape=jax.ShapeDtypeStruct((n_rows, D), x.dtype),
        scratch_shapes=[pltpu.SemaphoreType.DMA(())],
        input_output_aliases={2: 0},
        compiler_params=pltpu.CompilerParams(dimension_semantics=("arbitrary",)),
        name="moe_dispatch",
    )(dest2, x, jnp.zeros((n_rows, D), x.dtype))


def _moe_combine_body(dest_ref, ys_ref, w_ref, x_ref, g_ref, b_ref, o_ref, buf_ref, sem):
    n_choice, T = dest_ref.shape

    def start(r, carry):
        for k in range(n_choice):
            _row_copy(ys_ref, dest_ref[k, r], buf_ref.at[k], r, sem).start()
        return carry

    def wait(r, carry):
        for k in range(n_choice):
            _row_copy(ys_ref, 0, buf_ref.at[k], 0, sem).wait()
        return carry

    lax.fori_loop(0, T, start, 0)
    lax.fori_loop(0, T, wait, 0)
    w = w_ref[...]
    h = w[:, 0:1] * buf_ref[0]
    for k in range(1, n_choice):
        h = h + w[:, k:k + 1] * buf_ref[k]
    y = ALPHA * x_ref[...] + h
    mu = jnp.mean(y, axis=-1, keepdims=True)
    var = jnp.mean(jnp.square(y - mu), axis=-1, keepdims=True)
    o_ref[...] = (y - mu) * lax.rsqrt(var + LN_EPS) * g_ref[...] + b_ref[...]


def _moe_combine(ys, dest2, w, x, ln_g, ln_b):
    N, D = x.shape
    n_choice = dest2.shape[0]
    T = min(MOE_TOKEN_TILE, N)
    assert N % T == 0
    ln_g, ln_b = ln_g.reshape(1, D), ln_b.reshape(1, D)
    return pl.pallas_call(
        _moe_combine_body,
        grid=(N // T,),
        in_specs=[pl.BlockSpec((n_choice, T), lambda i: (0, i), memory_space=pltpu.SMEM),
                  pl.BlockSpec(memory_space=pl.ANY),
                  pl.BlockSpec((T, n_choice), lambda i: (i, 0)),
                  pl.BlockSpec((T, D), lambda i: (i, 0)),
                  pl.BlockSpec((1, D), lambda i: (0, 0)),
                  pl.BlockSpec((1, D), lambda i: (0, 0))],
        out_specs=pl.BlockSpec((T, D), lambda i: (i, 0)),
        out_shape=jax.ShapeDtypeStruct((N, D), jnp.float32),
        scratch_shapes=[pltpu.VMEM((n_choice, T, D), jnp.float32), pltpu.SemaphoreType.DMA(())],
        compiler_params=pltpu.CompilerParams(dimension_semantics=("arbitrary",)),
        name="moe_combine",
    )(dest2, ys, w, x, ln_g, ln_b)


def _split_cols(z, sizes):
    return jnp.split(z, np.cumsum(sizes)[:-1].tolist(), axis=-1)


def _layer_norm(x, g, b, eps=LN_EPS):
    mu = jnp.mean(x, axis=-1, keepdims=True)
    var = jnp.mean(jnp.square(x - mu), axis=-1, keepdims=True)
    return (x - mu) * lax.rsqrt(var + eps) * g + b


def _t5_bucket(dist):
    n = jnp.maximum(dist, 0)
    max_exact = NUM_BUCKETS // 2
    nf = jnp.maximum(n, 1).astype(jnp.float32)
    large = max_exact + (jnp.log(nf / max_exact) / math.log(MAX_DISTANCE / max_exact)
                         * (NUM_BUCKETS - max_exact)).astype(jnp.int32)
    large = jnp.minimum(large, NUM_BUCKETS - 1)
    return jnp.where(n < max_exact, n, large)


def _bias_from_buckets(rel_bias, bucket):
    rb = rel_bias.astype(jnp.float32)
    shape = (N_KV_GROUPS, HEADS_PER_GROUP) + (1,) * bucket.ndim
    out = jnp.zeros((N_KV_GROUPS, HEADS_PER_GROUP) + bucket.shape, jnp.float32)
    for b in range(NUM_BUCKETS):
        out = jnp.where(bucket == b, rb[b].reshape(shape), out)
    return out


def _nsa_positional(rel_bias, seq):
    n_cmp = seq // CMP_STRIDE - CMP_BLOCK // CMP_STRIDE + 1
    n_sel_blocks = seq // SEL_BLOCK
    t = jnp.arange(seq)[:, None]
    c = jnp.arange(n_cmp)[None, :]
    d_cmp = t - (c * CMP_STRIDE + CMP_BLOCK - 1)
    mask_cmp = d_cmp >= 0
    bias_cmp = _bias_from_buckets(rel_bias, _t5_bucket(d_cmp))
    qo = jnp.arange(Q_BLOCK)[:, None]
    m = jnp.arange(WINDOW + Q_BLOCK)[None, :]
    d_win = qo + WINDOW - m
    bias_win = _bias_from_buckets(rel_bias, _t5_bucket(d_win))
    blk = jnp.arange(seq // Q_BLOCK)[:, None, None]
    mask_win = (d_win >= 0) & (d_win < WINDOW) & (blk * Q_BLOCK - WINDOW + m >= 0)
    cs = jnp.arange(n_cmp)[:, None] * CMP_STRIDE
    ss = jnp.arange(n_sel_blocks)[None, :] * SEL_BLOCK
    overlap = jnp.clip(jnp.minimum(cs + CMP_BLOCK, ss + SEL_BLOCK) - jnp.maximum(cs, ss), 0, None)
    cmp_to_sel = overlap.astype(jnp.float32) / CMP_BLOCK
    return bias_cmp, mask_cmp, bias_win, mask_win, cmp_to_sel, _sel_bias_tiles(rel_bias, seq)


def _rwkv7_time_mix(r, k, v, wl, al, gl, w0, w2, a0, a2, g2, k_k, k_a, r_k, ln_g, ln_b):
    B, S, C = r.shape
    H, N = N_HEADS_RWKV, HEAD_RWKV
    f32 = jnp.float32
    logw = -jax.nn.softplus(-(w0 + jnp.tanh(wl) @ w2).astype(f32)) - 0.5
    log_decay = -jnp.exp(logw)
    a = jax.nn.sigmoid((a0 + al @ a2).astype(f32))
    g = jax.nn.sigmoid(gl) @ g2
    heads = lambda t: t.reshape(B, S, H, N)
    kk = heads(k * k_k)
    kk = kk / jnp.maximum(jnp.linalg.norm(kk, axis=-1, keepdims=True), 1e-12)
    k = k * (1.0 + (a - 1.0) * k_a)
    rh, kh, vh = heads(r), heads(k), heads(v)
    y = heads(_rwkv_scan(r, log_decay, k, v, kk.reshape(B, S, C), a, H))
    mu = jnp.mean(y, axis=-1, keepdims=True)
    var = jnp.mean(jnp.square(y - mu), axis=-1, keepdims=True)
    y = ((y - mu) * lax.rsqrt(var + RWKV_GN_EPS)).reshape(B, S, C) * ln_g + ln_b
    bonus = jnp.sum(rh * kh * r_k, axis=-1, keepdims=True) * vh
    return (y + bonus.reshape(B, S, C)) * g


def _compress(t, pe, w1, w2):
    B, S, G, Dh = t.shape
    rep = CMP_BLOCK // CMP_STRIDE
    nc = S // CMP_STRIDE - rep + 1
    sub = t.reshape(B, S // CMP_STRIDE, CMP_STRIDE, G, Dh)
    blk = jnp.concatenate([sub[:, j:j + nc] for j in range(rep)], axis=2)
    blk = blk + pe[:, None, :]
    blk = blk.transpose(0, 1, 3, 2, 4).reshape(B, nc, G, CMP_BLOCK * Dh)
    out = jax.nn.gelu(blk @ w1) @ w2
    return out.transpose(0, 2, 1, 3)


def _nsa_attention(q, k_cmp, v_cmp, k_slc, v_slc, k_win, v_win, gate_logits,
                   pe_k, w1_k, w2_k, pe_v, w1_v, w2_v, rel_bias, tables):
    bias_cmp_t, sel_tiles, win_tiles, c2s_t, n_cmp = tables
    B, S, _ = q.shape
    G, Hg, Dh = N_KV_GROUPS, HEADS_PER_GROUP, HEAD_NSA
    bf16 = jnp.bfloat16
    scale = HEAD_NSA ** -0.5
    assert math.frexp(scale)[0] == 0.5
    rows = lambda t: t.reshape(B, S, G, Dh).transpose(0, 2, 1, 3).astype(bf16)
    cols = lambda t: t.reshape(B, S, G, Dh).transpose(0, 2, 3, 1).astype(bf16)
    ncp = bias_cmp_t.shape[2]
    kc = _compress(k_cmp.reshape(B, S, G, Dh), pe_k, w1_k, w2_k)
    vc = _compress(v_cmp.reshape(B, S, G, Dh), pe_v, w1_v, w2_v)
    pad = ((0, 0), (0, 0), (0, ncp - n_cmp), (0, 0))
    kc_p = jnp.pad(kc, pad).astype(bf16)
    vc_t = jnp.pad(vc, pad).transpose(0, 1, 3, 2).astype(bf16)
    q_t = (q * scale).reshape(B, S, G, Hg, Dh).transpose(0, 2, 3, 4, 1).astype(bf16)
    gl_t = gate_logits.astype(jnp.float32).reshape(B, S, G, Hg * 3).transpose(0, 2, 3, 1)
    n_sel = min(N_SELECT, S // SEL_BLOCK)
    onehot = (jnp.arange(S)[:, None] // SEL_BLOCK == jnp.arange(S // SEL_BLOCK)[None, :]).astype(bf16)
    ks_aug = jnp.concatenate([rows(k_slc), jnp.broadcast_to(onehot, (B, G) + onehot.shape)], axis=-1)
    o_t = _nsa_fused(q_t, kc_p, vc_t, ks_aug, cols(v_slc), rows(k_win), cols(v_win), gl_t,
                     bias_cmp_t, sel_tiles, win_tiles, c2s_t, n_sel, n_cmp)
    return o_t.reshape(B, NSA_Q_WIDTH, S)


def _token_mixer(x, w_in, shift_mu, rw_w0, rw_w2, rw_a0, rw_a2, rw_g2, rw_kk, rw_ka, rw_rk,
                 rw_ln_g, rw_ln_b, cmp_pe_k, cmp_w1_k, cmp_w2_k, cmp_pe_v, cmp_w1_v, cmp_w2_v,
                 w_up_rwkv, w_up_nsa, w_out, ln_g, ln_b, rel_bias, pos):
    B, S, D = x.shape
    n_in = w_in.shape[1]
    n_gate = 3 * N_HEADS_NSA
    g_off = n_in - 2 * D_MODEL - n_gate
    lead_pad = (-g_off) % D_MODEL
    tail_pad = (-(g_off + lead_pad + 2 * D_MODEL + n_gate)) % IN_PROJ_TN
    zeros = lambda n: jnp.zeros((D, n), w_in.dtype)
    w_in_p = jnp.concatenate([w_in[:, :g_off], zeros(lead_pad), w_in[:, g_off + n_gate:],
                              w_in[:, g_off:g_off + n_gate], zeros(tail_pad)], axis=1).astype(jnp.bfloat16)
    gate_blk = (g_off + lead_pad) // D_MODEL
    z = _matmul(x.reshape(B * S, D), w_in_p, IN_PROJ_TM, IN_PROJ_TN)
    z3 = z.reshape(B, S, -1)
    z_rest = z3[..., SHIFT_WIDTH:g_off]
    nsa_g = z3[..., (gate_blk + 2) * D_MODEL:(gate_blk + 2) * D_MODEL + n_gate]
    q, kc, vc, ks, vs, kw, vw = _split_cols(z_rest, REST_SPLITS[:7])
    y_rw = _rwkv_fused(z, B, S, shift_mu, rw_w0, rw_w2, rw_a0, rw_a2, rw_g2,
                       rw_kk, rw_ka, rw_rk, rw_ln_g, rw_ln_b)
    y_nsa_t = _nsa_attention(q, kc, vc, ks, vs, kw, vw, nsa_g, cmp_pe_k, cmp_w1_k, cmp_w2_k,
                             cmp_pe_v, cmp_w1_v, cmp_w2_v, rel_bias, pos)
    bf16 = jnp.bfloat16
    return _merge_out(y_rw, y_nsa_t, z, gate_blk, x.reshape(B * S, D),
                      w_up_rwkv.astype(bf16), w_up_nsa.astype(bf16), w_out.astype(bf16),
                      ln_g, ln_b).reshape(B, S, D)


def _hier_moe(x, wg, bg, we, be, w1, w3, w2, layer, ln_g, ln_b):
    B, S, D = x.shape
    N = B * S
    f32 = jnp.float32
    xf = x.reshape(N, D)
    g_prob = jax.nn.softmax((xf @ wg + bg).astype(f32), axis=-1)
    grp = jnp.argmax(g_prob, axis=-1)
    p_grp = jnp.take_along_axis(g_prob, grp[:, None], axis=1)[:, 0]
    e_logits = (xf @ we + be).astype(f32).reshape(N, N_GROUPS, EXPERTS_PER_GROUP)
    e_logits = jnp.take_along_axis(e_logits, grp[:, None, None], axis=1)[:, 0]
    top_v, top_i = lax.top_k(e_logits, TOP_K_INNER)
    top_w = jax.nn.softmax(top_v, axis=-1) * p_grp[:, None]
    eid = (grp[:, None] * EXPERTS_PER_GROUP + top_i).astype(jnp.int32)
    n_slots = N * TOP_K_INNER
    n_rows = n_slots + N_EXPERTS * MOE_BLOCK
    n_blk = n_rows // MOE_BLOCK
    dest, counts = _moe_dest(eid.T.reshape(1, n_slots), N_EXPERTS, MOE_BLOCK)
    dest2 = dest.reshape(TOP_K_INNER, N)
    pcounts = (counts + MOE_BLOCK - 1) // MOE_BLOCK * MOE_BLOCK
    pends = jnp.cumsum(pcounts)
    blk_exp = jnp.minimum(jnp.sum(jnp.arange(n_blk)[:, None] * MOE_BLOCK >= pends[None, :], axis=1),
                          N_EXPERTS - 1).astype(jnp.int32)
    xs = _moe_dispatch(xf, dest2, n_rows)
    ys = _moe_experts(xs, blk_exp, w1, w3, w2, layer, MOE_BLOCK)
    return _moe_combine(ys, dest2, top_w, xf, ln_g, ln_b).reshape(B, S, D)


def kernel(x, rel_bias, w_in, shift_mu, rw_w0, rw_w2, rw_a0, rw_a2, rw_g2, rw_kk, rw_ka, rw_rk,
           rw_ln_g, rw_ln_b, cmp_pe_k, cmp_w1_k, cmp_w2_k, cmp_pe_v, cmp_w1_v, cmp_w2_v,
           w_up_rwkv, w_up_nsa, w_out, ln1_g, ln1_b, router_group_w, router_group_b,
           router_expert_w, router_expert_b, exp_w1, exp_w3, exp_w2, ln2_g, ln2_b):
    pos = _nsa_tables(rel_bias, x.shape[1])
    for l in range(DEPTH):
        x = _token_mixer(x, w_in[l], shift_mu[l], rw_w0[l], rw_w2[l], rw_a0[l], rw_a2[l], rw_g2[l],
                         rw_kk[l], rw_ka[l], rw_rk[l], rw_ln_g[l], rw_ln_b[l],
                         cmp_pe_k[l], cmp_w1_k[l], cmp_w2_k[l], cmp_pe_v[l], cmp_w1_v[l], cmp_w2_v[l],
                         w_up_rwkv[l], w_up_nsa[l], w_out[l], ln1_g[l], ln1_b[l], rel_bias, pos)
        x = _hier_moe(x, router_group_w[l], router_group_b[l], router_expert_w[l], router_expert_b[l],
                      exp_w1, exp_w3, exp_w2, l, ln2_g[l], ln2_b[l])
    return x
```

```python
import functools
import math

import jax
import jax.numpy as jnp
import numpy as np
from jax import lax
from jax.experimental import pallas as pl
from jax.experimental.pallas import tpu as pltpu

D_MODEL = 1024
DEPTH = 4
N_HEADS_RWKV = 8
HEAD_RWKV = 64
RWKV_WIDTH = N_HEADS_RWKV * HEAD_RWKV
LORA_W = 64
LORA_A = 64
LORA_G = 128
RWKV_GN_EPS = 64e-5
N_HEADS_NSA = 8
N_KV_GROUPS = 2
HEADS_PER_GROUP = N_HEADS_NSA // N_KV_GROUPS
HEAD_NSA = 64
NSA_Q_WIDTH = N_HEADS_NSA * HEAD_NSA
NSA_KV_WIDTH = N_KV_GROUPS * HEAD_NSA
CMP_BLOCK = 32
CMP_STRIDE = 16
CMP_HIDDEN = 128
SEL_BLOCK = 64
N_SELECT = 16
WINDOW = 512
Q_BLOCK = 128
N_BAND = WINDOW // Q_BLOCK + 1
NEG_INF = -1e30
FORCED_SCORE = 1e4
NUM_BUCKETS = 32
MAX_DISTANCE = 1024
N_GROUPS = 4
EXPERTS_PER_GROUP = 8
N_EXPERTS = N_GROUPS * EXPERTS_PER_GROUP
TOP_K_INNER = 2
D_EXPERT = 512
MOE_BLOCK = 128
ALPHA = (2 * DEPTH) ** 0.25
LN_EPS = 1e-5
SHIFT_SPLITS = (RWKV_WIDTH, RWKV_WIDTH, RWKV_WIDTH, LORA_W, LORA_A, LORA_G)
SHIFT_WIDTH = 3 * RWKV_WIDTH + LORA_W + LORA_A + LORA_G
REST_SPLITS = (NSA_Q_WIDTH,) + (NSA_KV_WIDTH,) * 6 + (3 * N_HEADS_NSA, D_MODEL, D_MODEL)

VMEM_LIMIT = 48 * 1024 * 1024


IN_PROJ_TM = 1024
IN_PROJ_TN = 1024


def _mm_body(x_ref, w_ref, o_ref, xb_ref):
    @pl.when(pl.program_id(1) == 0)
    def _():
        xb_ref[...] = x_ref[...].astype(jnp.bfloat16)

    o_ref[...] = jnp.dot(xb_ref[...], w_ref[...], preferred_element_type=jnp.float32)


def _matmul(x, w, tm, tn):
    m, k = x.shape
    n = w.shape[1]
    assert m % tm == 0 and n % tn == 0
    return pl.pallas_call(
        _mm_body,
        grid=(m // tm, n // tn),
        in_specs=[pl.BlockSpec((tm, k), lambda i, j: (i, 0)),
                  pl.BlockSpec((k, tn), lambda i, j: (0, j))],
        out_specs=pl.BlockSpec((tm, tn), lambda i, j: (i, j)),
        out_shape=jax.ShapeDtypeStruct((m, n), jnp.float32),
        scratch_shapes=[pltpu.VMEM((tm, k), jnp.bfloat16)],
        compiler_params=pltpu.CompilerParams(dimension_semantics=("arbitrary", "arbitrary"),
                                             vmem_limit_bytes=VMEM_LIMIT),
        name="in_proj",
    )(x, w)


IN_PROJ_T_TM = 512
NSA_GATE_ROWS = 16


def _mm_t_body(n_main, x_ref, wt_ref, zt_ref, gt_ref):
    xb = x_ref[...].astype(jnp.bfloat16)
    out = lax.dot_general(wt_ref[...], xb, (((1,), (1,)), ((), ())), preferred_element_type=jnp.float32)
    zt_ref[0] = out[:n_main].astype(zt_ref.dtype)
    gt_ref[0] = out[n_main:]


def _matmul_t(x, wt, n_main, batch, seq):
    m, k = x.shape
    r = wt.shape[0]
    tm = min(IN_PROJ_T_TM, seq)
    assert seq % tm == 0 and m == batch * seq
    per_b = seq // tm
    return pl.pallas_call(
        functools.partial(_mm_t_body, n_main),
        grid=(m // tm,),
        in_specs=[pl.BlockSpec((tm, k), lambda i: (i, 0)),
                  pl.BlockSpec((r, k), lambda i: (0, 0))],
        out_specs=[pl.BlockSpec((1, n_main, tm), lambda i: (i // per_b, 0, i % per_b)),
                   pl.BlockSpec((1, r - n_main, tm), lambda i: (i // per_b, 0, i % per_b))],
        out_shape=[jax.ShapeDtypeStruct((batch, n_main, seq), jnp.bfloat16),
                   jax.ShapeDtypeStruct((batch, r - n_main, seq), jnp.float32)],
        compiler_params=pltpu.CompilerParams(dimension_semantics=("arbitrary",),
                                             vmem_limit_bytes=VMEM_LIMIT),
        name="in_proj_t",
    )(x, wt)


MERGE_TM = 256


def _merge_body(yr_ref, yn_ref, grw_ref, gns_ref, x_ref, wur_ref, wun_ref, wo_ref, g_ref, b_ref, o_ref):
    f32, bf16 = jnp.float32, jnp.bfloat16
    up_r = jnp.dot(yr_ref[...].astype(bf16), wur_ref[...], preferred_element_type=f32)
    up_n = lax.dot_general(yn_ref[0].astype(bf16), wun_ref[...], _TN, preferred_element_type=f32)
    merged = jax.nn.sigmoid(grw_ref[...]) * up_r + jax.nn.sigmoid(gns_ref[...]) * up_n
    y = ALPHA * x_ref[...] + jnp.dot(merged.astype(bf16), wo_ref[...], preferred_element_type=f32)
    mu = jnp.mean(y, axis=-1, keepdims=True)
    var = jnp.mean(jnp.square(y - mu), axis=-1, keepdims=True)
    o_ref[...] = (y - mu) * lax.rsqrt(var + LN_EPS) * g_ref[...] + b_ref[...]


def _merge_out(y_rw, y_nsa_t, z, gate_blk, x, w_up_rwkv, w_up_nsa, w_out, ln_g, ln_b):
    m, d = x.shape
    tm = MERGE_TM
    nb, wn, seq = y_nsa_t.shape
    assert m % tm == 0 and seq % tm == 0
    per_b = seq // tm
    row = lambda width: pl.BlockSpec((tm, width), lambda i: (i, 0))
    full = lambda a: pl.BlockSpec(a.shape, lambda i: (0,) * a.ndim)
    ln_g, ln_b = ln_g.reshape(1, d), ln_b.reshape(1, d)
    return pl.pallas_call(
        _merge_body,
        grid=(m // tm,),
        in_specs=[row(y_rw.shape[1]), pl.BlockSpec((1, wn, tm), lambda i: (i // per_b, 0, i % per_b)),
                  pl.BlockSpec((tm, d), lambda i: (i, gate_blk)),
                  pl.BlockSpec((tm, d), lambda i: (i, gate_blk + 1)),
                  row(d), full(w_up_rwkv), full(w_up_nsa), full(w_out), full(ln_g), full(ln_b)],
        out_specs=row(d),
        out_shape=jax.ShapeDtypeStruct((m, d), jnp.float32),
        compiler_params=pltpu.CompilerParams(dimension_semantics=("arbitrary",),
                                             vmem_limit_bytes=VMEM_LIMIT),
        name="merge_out",
    )(y_rw, y_nsa_t, z, z, x, w_up_rwkv, w_up_nsa, w_out, ln_g, ln_b)


def _topk_mask_body(n_sel, s_ref, o_ref):
    s = s_ref[0, 0]
    ns = s.shape[0]
    jidx = lax.broadcasted_iota(jnp.int32, s.shape, 0)
    cnt = jnp.zeros(s.shape, jnp.float32)
    for jp in range(ns):
        row = s[jp:jp + 1, :]
        tie = jnp.where(jidx > jp, 1.0, 0.0)
        cnt = cnt + jnp.where(row > s, 1.0, jnp.where(row == s, tie, 0.0))
    o_ref[0, 0] = jnp.where(cnt < n_sel, 1.0, 0.0)


def _topk_mask(scores_t, n_sel, tq=512):
    B, G, NS, S = scores_t.shape
    tq = min(tq, S)
    return pl.pallas_call(
        functools.partial(_topk_mask_body, float(n_sel)),
        grid=(B, G, S // tq),
        in_specs=[pl.BlockSpec((1, 1, NS, tq), lambda b, g, i: (b, g, 0, i))],
        out_specs=pl.BlockSpec((1, 1, NS, tq), lambda b, g, i: (b, g, 0, i)),
        out_shape=jax.ShapeDtypeStruct((B, G, NS, S), jnp.float32),
        compiler_params=pltpu.CompilerParams(dimension_semantics=("arbitrary",) * 3),
        name="topk_mask",
    )(scores_t)


def _sel_attn_body(n_delta, q_ref, k_ref, v_ref, m_ref, t_ref, o_ref, acc_ref):
    i = pl.program_id(2)
    hg, dh, qb = q_ref.shape[2], q_ref.shape[3], q_ref.shape[4]
    ns = m_ref.shape[2]
    maskf = m_ref[0, 0].astype(jnp.bfloat16)
    key = lax.broadcasted_iota(jnp.int32, (qb, ns), 0)
    blk = lax.broadcasted_iota(jnp.int32, (qb, ns), 1)
    per = qb // SEL_BLOCK
    acc_ref[...] = jnp.zeros_like(acc_ref)

    def body(j, carry):
        ms, ls = carry
        off = pl.multiple_of(j * qb, qb)
        k = k_ref[0, 0, pl.ds(off, qb), :]
        v = v_ref[0, 0, :, pl.ds(off, qb)]
        expand = jnp.where(blk == j * per + key // SEL_BLOCK, 1.0, 0.0).astype(jnp.bfloat16)
        selm = jnp.dot(expand, maskf, preferred_element_type=jnp.float32) > 0.5
        d = jnp.minimum(i - j, n_delta)
        new_ms, new_ls = [], []
        for h in range(hg):
            lg = jnp.dot(k, q_ref[0, 0, h], preferred_element_type=jnp.float32) + t_ref[0, h, d]
            lg = jnp.where(selm, lg, NEG_INF)
            m_new = jnp.maximum(ms[h], jnp.max(lg, axis=0, keepdims=True))
            p = jnp.exp(lg - m_new)
            alpha = jnp.exp(ms[h] - m_new)
            new_ls.append(alpha * ls[h] + jnp.sum(p, axis=0, keepdims=True))
            acc_ref[h] = alpha * acc_ref[h] + jnp.dot(v, p.astype(jnp.bfloat16),
                                                      preferred_element_type=jnp.float32)
            new_ms.append(m_new)
        return tuple(new_ms), tuple(new_ls)

    init = (tuple(jnp.full((1, qb), NEG_INF, jnp.float32) for _ in range(hg)),
            tuple(jnp.zeros((1, qb), jnp.float32) for _ in range(hg)))
    ms, ls = lax.fori_loop(0, i + 1, body, init)
    for h in range(hg):
        o_ref[0, 0, h] = acc_ref[h] / ls[h]


def _sel_attention(q_t, ks, vs_t, mask_t, bias_tiles):
    B, G, Hg, Dh, S = q_t.shape
    NS = mask_t.shape[2]
    ND = bias_tiles.shape[2] - 1
    QB = bias_tiles.shape[-1]
    return pl.pallas_call(
        functools.partial(_sel_attn_body, ND),
        grid=(B, G, S // QB),
        in_specs=[pl.BlockSpec((1, 1, Hg, Dh, QB), lambda b, g, i: (b, g, 0, 0, i)),
                  pl.BlockSpec((1, 1, S, Dh), lambda b, g, i: (b, g, 0, 0)),
                  pl.BlockSpec((1, 1, Dh, S), lambda b, g, i: (b, g, 0, 0)),
                  pl.BlockSpec((1, 1, NS, QB), lambda b, g, i: (b, g, 0, i)),
                  pl.BlockSpec((1, Hg, ND + 1, QB, QB), lambda b, g, i: (g, 0, 0, 0, 0))],
        out_specs=pl.BlockSpec((1, 1, Hg, Dh, QB), lambda b, g, i: (b, g, 0, 0, i)),
        out_shape=jax.ShapeDtypeStruct((B, G, Hg, Dh, S), jnp.float32),
        scratch_shapes=[pltpu.VMEM((Hg, Dh, QB), jnp.float32)],
        compiler_params=pltpu.CompilerParams(dimension_semantics=("arbitrary",) * 3,
                                             vmem_limit_bytes=VMEM_LIMIT),
        name="sel_attention",
    )(q_t, ks, vs_t, mask_t, bias_tiles)


def _sel_bias_tiles(rel_bias, seq):
    nd = -(-MAX_DISTANCE // Q_BLOCK)
    nd = min(nd, seq // Q_BLOCK)
    half = NUM_BUCKETS // 2
    min_far = nd * Q_BLOCK - (Q_BLOCK - 1)
    assert half + math.log(min_far / half) / math.log(MAX_DISTANCE / half) * half >= NUM_BUCKETS - 0.75
    kj = np.arange(Q_BLOCK)[:, None]
    qi = np.arange(Q_BLOCK)[None, :]
    dist = np.arange(nd + 1)[:, None, None] * Q_BLOCK + qi - kj
    dist[nd] = max(seq - 1, nd * Q_BLOCK)
    tiles = _bias_from_buckets(rel_bias, _t5_bucket(jnp.asarray(np.maximum(dist, 0))))
    causal = jnp.asarray(dist >= 0)
    return jnp.where(causal, tiles, NEG_INF)


SEL_KEY_TILE = 256


def _flash_update(lgs, v_ts, ms, ls, acc_ref):
    H = range(len(lgs))
    m_new = []
    for h in H:
        m = ms[h]
        for lg in lgs[h]:
            m = jnp.maximum(m, jnp.max(lg, axis=0, keepdims=True))
        m_new.append(m)
    ps = [[jnp.exp(lg - m_new[h]) for lg in lgs[h]] for h in H]
    alpha = [jnp.exp(ms[h] - m_new[h]) for h in H]
    l_new = []
    for h in H:
        l = alpha[h] * ls[h]
        for p in ps[h]:
            l = l + jnp.sum(p, axis=0, keepdims=True)
        l_new.append(l)
    pv = [[jnp.dot(v_t, p.astype(jnp.bfloat16), preferred_element_type=jnp.float32)
           for p, v_t in zip(ps[h], v_ts)] for h in H]
    for h in H:
        acc = alpha[h] * acc_ref[h]
        for o in pv[h]:
            acc = acc + o
        acc_ref[h] = acc
    return tuple(m_new), tuple(l_new)


def _nsa_body(n_sel, n_cmp, n_delta, q_ref, kc_ref, vc_ref, ks_ref, vs_ref, kw_ref, vw_ref, gl_ref,
              bc_ref, ts_ref, tw_ref, c2s_ref, oh_ref, o_ref, acc_s, acc_w, ocmp_ref):
    i = pl.program_id(2)
    hg, dh, qb = q_ref.shape[1], q_ref.shape[2], q_ref.shape[3]
    ncp = kc_ref.shape[2]
    ns = c2s_ref.shape[0]
    f32, bf16 = jnp.float32, jnp.bfloat16
    H = range(hg)

    cidx = lax.broadcasted_iota(jnp.int32, (ncp, qb), 0)
    tpos = lax.broadcasted_iota(jnp.int32, (ncp, qb), 1) + i * qb
    valid = (tpos - (cidx * CMP_STRIDE + CMP_BLOCK - 1)) >= 0
    fill = jnp.where(cidx < n_cmp, NEG_INF, 2.0 * NEG_INF)
    kc = kc_ref[0, 0]
    vc_t = vc_ref[0, 0]
    psum = jnp.zeros((ncp, qb), f32)
    for h in H:
        lg = jnp.dot(kc, q_ref[0, h], preferred_element_type=f32) + bc_ref[0, h]
        lg = jnp.where(valid, lg, fill)
        m = jnp.max(lg, axis=0, keepdims=True)
        p = jnp.exp(lg - m)
        inv = 1.0 / jnp.sum(p, axis=0, keepdims=True)
        p = jnp.where(valid, p * inv, 0.0)
        psum = psum + p
        ocmp_ref[h] = jnp.dot(vc_t, p.astype(bf16), preferred_element_type=f32)

    score = jnp.dot(c2s_ref[...], psum.astype(bf16), preferred_element_type=f32)
    jblk = lax.broadcasted_iota(jnp.int32, (ns, qb), 0)
    cur = (lax.broadcasted_iota(jnp.int32, (ns, qb), 1) + i * qb) // SEL_BLOCK
    forced = (jblk == 0) | (jblk == cur) | (jblk == cur - 1)
    score = jnp.where(forced, FORCED_SCORE, jnp.where(jblk <= cur, score, -1.0))
    cnt = jnp.zeros((ns, qb), f32)
    for jp in range(ns):
        row = score[jp:jp + 1, :]
        tie = jnp.where(jblk > jp, 1.0, 0.0)
        cnt = cnt + jnp.where(row > score, 1.0, jnp.where(row == score, tie, 0.0))
    maskneg = jnp.where(cnt < n_sel, 0.0, NEG_INF).astype(bf16)
    q_aug = [jnp.concatenate([q_ref[0, h], maskneg], axis=0) for h in H]

    kt = SEL_KEY_TILE
    per = kt // qb
    last_tile = i // per
    acc_s[...] = jnp.zeros_like(acc_s)

    def sel_step(jj, carry):
        ms, ls = carry
        ks_, vs_, tiles = [], [], []
        for u in range(2):
            j = 2 * jj + u
            off = pl.multiple_of(jnp.minimum(j, last_tile) * kt, kt)
            ks_.append(jnp.concatenate([ks_ref[0, 0, :, pl.ds(off, kt)], oh_ref[:, pl.ds(off, kt)]], axis=0))
            vs_.append(vs_ref[0, 0, :, pl.ds(off, kt)])
            tiles.append([jnp.clip(i - (j * per + w), -1, n_delta) + 1 for w in range(per)])
        qk = [[lax.dot_general(ks_[u], q_aug[h], _TN, preferred_element_type=f32) for u in range(2)] for h in H]
        lgs = [[qk[h][u] + jnp.concatenate([ts_ref[0, h, tiles[u][w]] for w in range(per)], axis=0)
                for u in range(2)] for h in H]
        return _flash_update(lgs, vs_, ms, ls, acc_s)

    init = (tuple(jnp.full((1, qb), NEG_INF, f32) for _ in H), tuple(jnp.zeros((1, qb), f32) for _ in H))
    _, ls_s = lax.fori_loop(0, last_tile // 2 + 1, sel_step, init)

    acc_w[...] = jnp.zeros_like(acc_w)
    kws, vws, widx = [], [], []
    for u in range(N_BAND):
        j = i - (N_BAND - 1) + u
        off = pl.multiple_of(jnp.maximum(j, 0) * qb, qb)
        kws.append(kw_ref[0, 0, :, pl.ds(off, qb)])
        vws.append(vw_ref[0, 0, :, pl.ds(off, qb)])
        widx.append(jnp.where(j >= 0, N_BAND - 1 - u, N_BAND))
    qk = [[lax.dot_general(kws[u], q_ref[0, h], _TN, preferred_element_type=f32) for u in range(N_BAND)] for h in H]
    lgs = [[qk[h][u] + tw_ref[0, h, widx[u]] for u in range(N_BAND)] for h in H]
    _, ls_w = _flash_update(lgs, vws, init[0], init[1], acc_w)

    gates = jax.nn.sigmoid(gl_ref[0])
    for h in H:
        o_ref[0, 0, h] = (gates[3 * h:3 * h + 1] * ocmp_ref[h]
                          + gates[3 * h + 1:3 * h + 2] * (acc_s[h] / ls_s[h])
                          + gates[3 * h + 2:3 * h + 3] * (acc_w[h] / ls_w[h]))


def _nsa_fused(zt, gates_t, kc, vc_t, bias_cmp_t, sel_tiles, win_tiles, c2s_t, onehot_t, n_sel, n_cmp):
    B, _, S = zt.shape
    G, Hg, Dh = N_KV_GROUPS, HEADS_PER_GROUP, HEAD_NSA
    NCP = kc.shape[2]
    NS = c2s_t.shape[0]
    QB = Q_BLOCK
    ND = sel_tiles.shape[2] - 2
    assert S % SEL_KEY_TILE == 0 and SEL_KEY_TILE % QB == 0
    z4 = zt.reshape(B, -1, Dh, S)
    stream = lambda n: pl.BlockSpec((1, 1, Dh, S), lambda b, g, i: (b, G * Hg + n * G + g, 0, 0))
    bg = lambda *blk: pl.BlockSpec((1, 1) + blk, lambda b, g, i: (b, g) + (0,) * len(blk))
    gt = lambda arr: pl.BlockSpec((1,) + arr.shape[1:], lambda b, g, i: (g,) + (0,) * (arr.ndim - 1))
    return pl.pallas_call(
        functools.partial(_nsa_body, float(n_sel), n_cmp, ND),
        grid=(B, G, S // QB),
        in_specs=[pl.BlockSpec((1, Hg, Dh, QB), lambda b, g, i: (b, g, 0, i)),
                  bg(NCP, Dh), bg(Dh, NCP), stream(0), stream(1), stream(2), stream(3),
                  pl.BlockSpec((1, NSA_GATE_ROWS, QB), lambda b, g, i: (b, g, i)),
                  pl.BlockSpec((1, Hg, NCP, QB), lambda b, g, i: (g, 0, 0, i)),
                  gt(sel_tiles), gt(win_tiles),
                  pl.BlockSpec(c2s_t.shape, lambda b, g, i: (0, 0)),
                  pl.BlockSpec(onehot_t.shape, lambda b, g, i: (0, 0))],
        out_specs=pl.BlockSpec((1, 1, Hg, Dh, QB), lambda b, g, i: (b, g, 0, 0, i)),
        out_shape=jax.ShapeDtypeStruct((B, G, Hg, Dh, S), jnp.float32),
        scratch_shapes=[pltpu.VMEM((Hg, Dh, QB), jnp.float32)] * 3,
        compiler_params=pltpu.CompilerParams(dimension_semantics=("arbitrary",) * 3,
                                             vmem_limit_bytes=VMEM_LIMIT),
        name="nsa_fused",
    )(z4, kc, vc_t, z4, z4, z4, z4, gates_t, bias_cmp_t, sel_tiles, win_tiles, c2s_t, onehot_t)


def _nsa_tables(rel_bias, seq):
    n_cmp = seq // CMP_STRIDE - CMP_BLOCK // CMP_STRIDE + 1
    ncp = -(-n_cmp // 128) * 128
    c = jnp.arange(ncp)[:, None]
    t = jnp.arange(seq)[None, :]
    bias_cmp_t = _bias_from_buckets(rel_bias, _t5_bucket(t - (c * CMP_STRIDE + CMP_BLOCK - 1)))
    base = _sel_bias_tiles(rel_bias, seq)
    sel_tiles = jnp.concatenate([jnp.full_like(base[:, :, :1], NEG_INF), base], axis=2)
    kj = np.arange(Q_BLOCK)[:, None]
    qi = np.arange(Q_BLOCK)[None, :]
    dwin = np.arange(N_BAND)[:, None, None] * Q_BLOCK + qi - kj
    nwin = min(N_BAND, base.shape[2])
    win_tiles = jnp.where(jnp.asarray(dwin[:nwin] < WINDOW), base[:, :, :nwin], NEG_INF)
    win_tiles = jnp.concatenate(
        [win_tiles, jnp.full(win_tiles.shape[:2] + (N_BAND + 1 - nwin,) + win_tiles.shape[3:], NEG_INF)], axis=2)
    n_sel_blocks = seq // SEL_BLOCK
    cs = np.arange(ncp)[None, :] * CMP_STRIDE
    ss = np.arange(n_sel_blocks)[:, None] * SEL_BLOCK
    overlap = np.clip(np.minimum(cs + CMP_BLOCK, ss + SEL_BLOCK) - np.maximum(cs, ss), 0, None)
    overlap = np.where(np.arange(ncp)[None, :] < n_cmp, overlap, 0)
    c2s_t = jnp.asarray(overlap.astype(np.float32) / CMP_BLOCK, jnp.bfloat16)
    onehot_t = jnp.asarray(np.arange(n_sel_blocks)[:, None] == np.arange(seq)[None, :] // SEL_BLOCK, jnp.bfloat16)
    return bias_cmp_t, sel_tiles, win_tiles, c2s_t, onehot_t, n_cmp


RWKV_CHUNK = 64
RWKV_INV_BASE = 8


def _split_bf16(x):
    hi = x.astype(jnp.bfloat16)
    lo = (x - hi.astype(jnp.float32)).astype(jnp.bfloat16)
    return hi, lo


def _dot3(a, b, dims=(((1,), (0,)), ((), ()))):
    a_hi, a_lo = _split_bf16(a)
    b_hi, b_lo = _split_bf16(b)
    d = functools.partial(lax.dot_general, dimension_numbers=dims, preferred_element_type=jnp.float32)
    return d(a_hi, b_hi) + (d(a_lo, b_hi) + d(a_hi, b_lo))


_NT = (((1,), (1,)), ((), ()))
_TN = (((0,), (0,)), ((), ()))


def _rwkv_chunk(r_w, lw, k_w, v_w, kk, lr, st_ref, n_heads):
    C = r_w.shape[0]
    N = r_w.shape[1] // n_heads
    H = range(n_heads)
    f32 = jnp.float32
    row = lax.broadcasted_iota(jnp.int32, (C, C), 0)
    col = lax.broadcasted_iota(jnp.int32, (C, C), 1)
    strict = col < row
    incl = col <= row
    eye = jnp.where(row == col, 1.0, 0.0).astype(f32)
    tri = jnp.where(incl, 1.0, 0.0).astype(jnp.bfloat16)

    cl = _dot_exact_lhs(tri, lw)

    base = RWKV_INV_BASE
    diag_blk = strict & ((row // base) == (col // base))
    level_masks = []
    s = base
    while s < C:
        level_masks.append(strict & ((row // (2 * s)) == (col // (2 * s))) & ((row // s) != (col // s)))
        s *= 2

    cl_end = cl[C - 1:C, :]
    bb = kk * lr
    g_inv = jnp.exp(-cl)
    g_rem = jnp.exp(cl_end - cl)
    g_end = jnp.exp(cl_end)
    abar_w = -kk * jnp.exp(cl - lw)
    rbar_w = r_w * jnp.exp(cl)
    bbar_w = bb * g_inv
    kbar_w = k_w * g_inv
    bhat_w = bb * g_rem
    khat_w = k_w * g_rem
    hs = lambda x, h: x[:, h * N:(h + 1) * N]

    abar = [hs(abar_w, h) for h in H]
    rbar = [hs(rbar_w, h) for h in H]
    v = [hs(v_w, h) for h in H]
    gmat = [_dot3(jnp.concatenate([abar[h], rbar[h]], axis=0),
                  jnp.concatenate([hs(bbar_w, h), hs(kbar_w, h)], axis=0), _NT) for h in H]
    a_ab = [jnp.where(strict, gmat[h][:C, :C], 0.0) for h in H]
    a_ak = [jnp.where(strict, gmat[h][:C, C:], 0.0) for h in H]
    m_rb = [jnp.where(incl, gmat[h][C:, :C], 0.0) for h in H]
    m_rk = [jnp.where(incl, gmat[h][C:, C:], 0.0) for h in H]
    akv = [_dot3(a_ak[h], v[h]) for h in H]
    dp = [jnp.where(diag_blk, a_ab[h], 0.0) for h in H]
    x = [eye + dp[h] for h in H]
    s = 2
    while s < base:
        dp = [_dot3(dp[h], dp[h]) for h in H]
        x = [x[h] + _dot3(x[h], dp[h]) for h in H]
        s *= 2
    for lm in level_masks:
        t = [_dot3(jnp.where(lm, a_ab[h], 0.0), x[h]) for h in H]
        x = [x[h] + _dot3(x[h], t[h]) for h in H]
    xw = [_dot3(x[h], jnp.concatenate([akv[h], abar[h]], axis=1)) for h in H]
    uv = [jnp.concatenate([xw[h][:, :N], v[h]], axis=0) for h in H]
    atil = [xw[h][:, N:] for h in H]
    y_loc = [_dot3(jnp.concatenate([m_rb[h], m_rk[h]], axis=1), uv[h]) for h in H]
    qm = [rbar[h] + _dot3(m_rb[h], atil[h]) for h in H]
    s_loc = [_dot3(jnp.concatenate([hs(bhat_w, h), hs(khat_w, h)], axis=0), uv[h], _TN) for h in H]
    pm = [eye[:N, :N] * hs(g_end, h) + _dot3(hs(bhat_w, h), atil[h], _TN) for h in H]
    s0 = [st_ref[h] for h in H]
    y = [y_loc[h] + _dot3(qm[h], s0[h]) for h in H]
    for h in H:
        st_ref[h] = s_loc[h] + _dot3(pm[h], s0[h])
    return jnp.concatenate(y, axis=1)


def _bf16_pieces(x):
    f32 = jnp.float32
    p1 = x.astype(jnp.bfloat16)
    r1 = x - p1.astype(f32)
    p2 = r1.astype(jnp.bfloat16)
    p3 = (r1 - p2.astype(f32)).astype(jnp.bfloat16)
    return p1, p2, p3


def _dot_exact_lhs(a_bf16, x):
    dd = functools.partial(jnp.dot, preferred_element_type=jnp.float32)
    p1, p2, p3 = _bf16_pieces(x)
    return dd(a_bf16, p1) + (dd(a_bf16, p2) + dd(a_bf16, p3))


def _dot_exact_rhs(x, b_bf16):
    dd = functools.partial(jnp.dot, preferred_element_type=jnp.float32)
    p1, p2, p3 = _bf16_pieces(x)
    return dd(p1, b_bf16) + (dd(p2, b_bf16) + dd(p3, b_bf16))


def _rwkv_body(n_heads, z_ref, mu_ref, w0_ref, w2_ref, a0_ref, a2_ref, g2_ref, kk_ref, ka_ref, rk_ref,
               lng_ref, lnb_ref, y_ref, st_ref, prev_ref):
    c = pl.program_id(1)
    C = z_ref.shape[0]
    W = y_ref.shape[1]
    N = W // n_heads
    f32, bf16 = jnp.float32, jnp.bfloat16

    @pl.when(c == 0)
    def _():
        st_ref[...] = jnp.zeros_like(st_ref)
        prev_ref[...] = jnp.zeros_like(prev_ref)

    z = z_ref[...]
    z_prev = jnp.concatenate([prev_ref[...], z[:C - 1]], axis=0)
    prev_ref[...] = z[C - 1:C]
    zs = z + (z_prev - z) * mu_ref[...]
    o = np.cumsum((0,) + SHIFT_SPLITS)
    r, k, v, wl, al, gl = (zs[:, o[i]:o[i + 1]] for i in range(6))
    dd = functools.partial(jnp.dot, preferred_element_type=f32)
    yw = w0_ref[...] + dd(jnp.tanh(wl).astype(bf16), w2_ref[...])
    logw = -(jnp.maximum(-yw, 0.0) + jnp.log1p(jnp.exp(-jnp.abs(yw)))) - 0.5
    lw = -jnp.exp(logw)
    lr = jax.nn.sigmoid(a0_ref[...] + dd(al.astype(bf16), a2_ref[...]))
    g = dd(jax.nn.sigmoid(gl).astype(bf16), g2_ref[...])
    hrow = lax.broadcasted_iota(jnp.int32, (W, W), 0) // N
    hcol = lax.broadcasted_iota(jnp.int32, (W, W), 1) // N
    seg = jnp.where(hrow == hcol, 1.0, 0.0).astype(bf16)
    kk = k * kk_ref[...]
    kk = kk / jnp.maximum(jnp.sqrt(_dot_exact_rhs(kk * kk, seg)), 1e-12)
    k = k * (1.0 + (lr - 1.0) * ka_ref[...])
    y = _rwkv_chunk(r, lw, k, v, kk, lr, st_ref, n_heads)
    mu = _dot_exact_rhs(y, seg) * (1.0 / N)
    yc = y - mu
    var = _dot_exact_rhs(yc * yc, seg) * (1.0 / N)
    yn = yc * lax.rsqrt(var + RWKV_GN_EPS) * lng_ref[...] + lnb_ref[...]
    bonus = _dot_exact_rhs(r * k * rk_ref[...], seg) * v
    y_ref[...] = (yn + bonus) * g


def _rwkv_fused(z, batch, seq, shift_mu, w0, w2, a0, a2, g2, k_k, k_a, r_k, ln_g, ln_b):
    H, N, W = N_HEADS_RWKV, HEAD_RWKV, RWKV_WIDTH
    C = min(RWKV_CHUNK, seq)
    assert N <= C and seq % C == 0 and SHIFT_WIDTH % 128 == 0
    nc = seq // C
    bf16 = jnp.bfloat16
    row = lambda a: a.reshape(1, -1).astype(jnp.float32)
    params = [row(shift_mu), row(w0), w2.astype(bf16), row(a0), a2.astype(bf16), g2.astype(bf16),
              row(k_k), row(k_a), row(r_k), row(ln_g), row(ln_b)]
    full = lambda a: pl.BlockSpec(a.shape, lambda b, c: (0, 0))
    return pl.pallas_call(
        functools.partial(_rwkv_body, H),
        grid=(batch, nc),
        in_specs=[pl.BlockSpec((C, SHIFT_WIDTH), lambda b, c: (b * nc + c, 0))] + [full(p) for p in params],
        out_specs=pl.BlockSpec((C, W), lambda b, c: (b * nc + c, 0)),
        out_shape=jax.ShapeDtypeStruct((batch * seq, W), jnp.float32),
        scratch_shapes=[pltpu.VMEM((H, N, N), jnp.float32), pltpu.VMEM((1, SHIFT_WIDTH), jnp.float32)],
        compiler_params=pltpu.CompilerParams(dimension_semantics=("arbitrary", "arbitrary")),
        name="rwkv_fused",
    )(z, *params)


def _moe_body(be_ref, x_ref, w1_ref, w3_ref, w2_ref, o_ref):
    bf16 = jnp.bfloat16
    x = x_ref[...].astype(bf16)
    h1 = jnp.dot(x, w1_ref[0, 0].astype(bf16), preferred_element_type=jnp.float32)
    h3 = jnp.dot(x, w3_ref[0, 0].astype(bf16), preferred_element_type=jnp.float32)
    h = (h1 * jax.nn.sigmoid(h1)) * h3
    o_ref[...] = jnp.dot(h.astype(bf16), w2_ref[0, 0].astype(bf16), preferred_element_type=jnp.float32)


def _moe_experts(xs, blk_exp, w1, w3, w2, layer, blk):
    n_rows, D = xs.shape
    De = w1.shape[3]
    grid_spec = pltpu.PrefetchScalarGridSpec(
        num_scalar_prefetch=1,
        grid=(n_rows // blk,),
        in_specs=[pl.BlockSpec((blk, D), lambda i, be: (i, 0)),
                  pl.BlockSpec((1, 1, D, De), lambda i, be: (layer, be[i], 0, 0)),
                  pl.BlockSpec((1, 1, D, De), lambda i, be: (layer, be[i], 0, 0)),
                  pl.BlockSpec((1, 1, De, D), lambda i, be: (layer, be[i], 0, 0))],
        out_specs=pl.BlockSpec((blk, D), lambda i, be: (i, 0)))
    return pl.pallas_call(
        _moe_body,
        grid_spec=grid_spec,
        out_shape=jax.ShapeDtypeStruct((n_rows, D), jnp.float32),
        compiler_params=pltpu.CompilerParams(dimension_semantics=("arbitrary",),
                                             vmem_limit_bytes=VMEM_LIMIT),
        name="moe_experts",
    )(blk_exp, xs, w1, w3, w2)


MOE_DEST_CHUNK = 512


def _moe_dest_body(n_exp, blk, eid_ref, dest_ref, cnt_ref, run_ref, pst_ref):
    ph, c = pl.program_id(0), pl.program_id(1)
    T = eid_ref.shape[1]
    f32, bf16 = jnp.float32, jnp.bfloat16
    onehot = jnp.where(lax.broadcasted_iota(jnp.int32, (n_exp, T), 0) == eid_ref[...], 1.0, 0.0)
    here = jnp.sum(onehot, axis=1, keepdims=True)

    @pl.when((ph == 0) & (c == 0))
    def _():
        run_ref[...] = jnp.zeros_like(run_ref)

    @pl.when(ph == 0)
    def _():
        run_ref[...] = run_ref[...] + here
        dest_ref[...] = jnp.zeros_like(dest_ref)

    @pl.when((ph == 1) & (c == 0))
    def _():
        counts = run_ref[...]
        cnt_ref[...] = counts
        padded = jnp.floor((counts + (blk - 1)) * (1.0 / blk)) * blk
        er = lax.broadcasted_iota(jnp.int32, (n_exp, n_exp), 0)
        ec = lax.broadcasted_iota(jnp.int32, (n_exp, n_exp), 1)
        pst_ref[...] = _dot_exact_lhs(jnp.where(ec < er, 1.0, 0.0).astype(bf16), padded)
        run_ref[...] = jnp.zeros_like(run_ref)

    @pl.when(ph == 1)
    def _():
        sr = lax.broadcasted_iota(jnp.int32, (T, T), 0)
        sc = lax.broadcasted_iota(jnp.int32, (T, T), 1)
        earlier = jnp.dot(onehot.astype(bf16), jnp.where(sr < sc, 1.0, 0.0).astype(bf16),
                          preferred_element_type=f32)
        base = pst_ref[:, 0:1] + run_ref[:, 0:1]
        dest = jnp.sum(onehot * (earlier + base), axis=0, keepdims=True)
        dest_ref[...] = dest.astype(jnp.int32)
        run_ref[...] = run_ref[...] + here


def _moe_dest(eid_row, n_exp, blk):
    n_slots = eid_row.shape[1]
    T = min(MOE_DEST_CHUNK, n_slots)
    assert n_slots % T == 0 and n_slots + n_exp * blk < 2 ** 24
    dest, cnt = pl.pallas_call(
        functools.partial(_moe_dest_body, n_exp, blk),
        grid=(2, n_slots // T),
        in_specs=[pl.BlockSpec((1, T), lambda p, c: (0, c))],
        out_specs=[pl.BlockSpec((1, T), lambda p, c: (0, c * p)),
                   pl.BlockSpec((n_exp, 128), lambda p, c: (0, 0))],
        out_shape=[jax.ShapeDtypeStruct((1, n_slots), jnp.int32),
                   jax.ShapeDtypeStruct((n_exp, 128), jnp.float32)],
        scratch_shapes=[pltpu.VMEM((n_exp, 128), jnp.float32), pltpu.VMEM((n_exp, 128), jnp.float32)],
        compiler_params=pltpu.CompilerParams(dimension_semantics=("arbitrary", "arbitrary")),
        name="moe_dest",
    )(eid_row)
    return dest, cnt[:, 0].astype(jnp.int32)


MOE_TOKEN_TILE = 256


def _row_copy(src_ref, src_row, dst_ref, dst_row, sem):
    return pltpu.make_async_copy(src_ref.at[pl.ds(src_row, 1)], dst_ref.at[pl.ds(dst_row, 1)], sem)


def _moe_dispatch_body(dest_ref, x_ref, init_ref, xs_ref, sem):
    del init_ref
    n_choice, T = dest_ref.shape

    def start(r, carry):
        for k in range(n_choice):
            _row_copy(x_ref, r, xs_ref, dest_ref[k, r], sem).start(priority=k % 2)
        return carry

    def wait(r, carry):
        for k in range(n_choice):
            _row_copy(x_ref, 0, xs_ref, 0, sem).wait()
        return carry

    lax.fori_loop(0, T, start, 0, unroll=8)
    lax.fori_loop(0, T, wait, 0)


def _moe_dispatch(x, dest2, n_rows):
    N, D = x.shape
    T = min(MOE_TOKEN_TILE, N)
    assert N % T == 0
    return pl.pallas_call(
        _moe_dispatch_body,
        grid=(N // T,),
        in_specs=[pl.BlockSpec((dest2.shape[0], T), lambda i: (0, i), memory_space=pltpu.SMEM),
                  pl.BlockSpec((T, D), lambda i: (i, 0)),
                  pl.BlockSpec(memory_space=pl.ANY)],
        out_specs=pl.BlockSpec(memory_space=pl.ANY),
        out_shape=jax.ShapeDtypeStruct((n_rows, D), x.dtype),
        scratch_shapes=[pltpu.SemaphoreType.DMA(())],
        input_output_aliases={2: 0},
        compiler_params=pltpu.CompilerParams(dimension_semantics=("arbitrary",)),
        name="moe_dispatch",
    )(dest2, x, jnp.zeros((n_rows, D), x.dtype))


def _moe_combine_body(dest_ref, ys_ref, w_ref, x_ref, g_ref, b_ref, o_ref, buf_ref, sem):
    n_choice, T = dest_ref.shape

    def start(r, carry):
        for k in range(n_choice):
            _row_copy(ys_ref, dest_ref[k, r], buf_ref.at[k], r, sem).start(priority=k % 2)
        return carry

    def wait(r, carry):
        for k in range(n_choice):
            _row_copy(ys_ref, 0, buf_ref.at[k], 0, sem).wait()
        return carry

    lax.fori_loop(0, T, start, 0, unroll=8)
    lax.fori_loop(0, T, wait, 0)
    w = w_ref[...]
    h = w[:, 0:1] * buf_ref[0]
    for k in range(1, n_choice):
        h = h + w[:, k:k + 1] * buf_ref[k]
    y = ALPHA * x_ref[...] + h
    mu = jnp.mean(y, axis=-1, keepdims=True)
    var = jnp.mean(jnp.square(y - mu), axis=-1, keepdims=True)
    o_ref[...] = (y - mu) * lax.rsqrt(var + LN_EPS) * g_ref[...] + b_ref[...]


def _moe_combine(ys, dest2, w, x, ln_g, ln_b):
    N, D = x.shape
    n_choice = dest2.shape[0]
    T = min(MOE_TOKEN_TILE, N)
    assert N % T == 0
    ln_g, ln_b = ln_g.reshape(1, D), ln_b.reshape(1, D)
    return pl.pallas_call(
        _moe_combine_body,
        grid=(N // T,),
        in_specs=[pl.BlockSpec((n_choice, T), lambda i: (0, i), memory_space=pltpu.SMEM),
                  pl.BlockSpec(memory_space=pl.ANY),
                  pl.BlockSpec((T, n_choice), lambda i: (i, 0)),
                  pl.BlockSpec((T, D), lambda i: (i, 0)),
                  pl.BlockSpec((1, D), lambda i: (0, 0)),
                  pl.BlockSpec((1, D), lambda i: (0, 0))],
        out_specs=pl.BlockSpec((T, D), lambda i: (i, 0)),
        out_shape=jax.ShapeDtypeStruct((N, D), jnp.float32),
        scratch_shapes=[pltpu.VMEM((n_choice, T, D), jnp.float32), pltpu.SemaphoreType.DMA(())],
        compiler_params=pltpu.CompilerParams(dimension_semantics=("arbitrary",)),
        name="moe_combine",
    )(dest2, ys, w, x, ln_g, ln_b)


def _split_cols(z, sizes):
    return jnp.split(z, np.cumsum(sizes)[:-1].tolist(), axis=-1)


def _layer_norm(x, g, b, eps=LN_EPS):
    mu = jnp.mean(x, axis=-1, keepdims=True)
    var = jnp.mean(jnp.square(x - mu), axis=-1, keepdims=True)
    return (x - mu) * lax.rsqrt(var + eps) * g + b


def _t5_bucket(dist):
    n = jnp.maximum(dist, 0)
    max_exact = NUM_BUCKETS // 2
    nf = jnp.maximum(n, 1).astype(jnp.float32)
    large = max_exact + (jnp.log(nf / max_exact) / math.log(MAX_DISTANCE / max_exact)
                         * (NUM_BUCKETS - max_exact)).astype(jnp.int32)
    large = jnp.minimum(large, NUM_BUCKETS - 1)
    return jnp.where(n < max_exact, n, large)


def _bias_from_buckets(rel_bias, bucket):
    rb = rel_bias.astype(jnp.float32)
    shape = (N_KV_GROUPS, HEADS_PER_GROUP) + (1,) * bucket.ndim
    out = jnp.zeros((N_KV_GROUPS, HEADS_PER_GROUP) + bucket.shape, jnp.float32)
    for b in range(NUM_BUCKETS):
        out = jnp.where(bucket == b, rb[b].reshape(shape), out)
    return out


def _nsa_positional(rel_bias, seq):
    n_cmp = seq // CMP_STRIDE - CMP_BLOCK // CMP_STRIDE + 1
    n_sel_blocks = seq // SEL_BLOCK
    t = jnp.arange(seq)[:, None]
    c = jnp.arange(n_cmp)[None, :]
    d_cmp = t - (c * CMP_STRIDE + CMP_BLOCK - 1)
    mask_cmp = d_cmp >= 0
    bias_cmp = _bias_from_buckets(rel_bias, _t5_bucket(d_cmp))
    qo = jnp.arange(Q_BLOCK)[:, None]
    m = jnp.arange(WINDOW + Q_BLOCK)[None, :]
    d_win = qo + WINDOW - m
    bias_win = _bias_from_buckets(rel_bias, _t5_bucket(d_win))
    blk = jnp.arange(seq // Q_BLOCK)[:, None, None]
    mask_win = (d_win >= 0) & (d_win < WINDOW) & (blk * Q_BLOCK - WINDOW + m >= 0)
    cs = jnp.arange(n_cmp)[:, None] * CMP_STRIDE
    ss = jnp.arange(n_sel_blocks)[None, :] * SEL_BLOCK
    overlap = jnp.clip(jnp.minimum(cs + CMP_BLOCK, ss + SEL_BLOCK) - jnp.maximum(cs, ss), 0, None)
    cmp_to_sel = overlap.astype(jnp.float32) / CMP_BLOCK
    return bias_cmp, mask_cmp, bias_win, mask_win, cmp_to_sel, _sel_bias_tiles(rel_bias, seq)


def _rwkv7_time_mix(r, k, v, wl, al, gl, w0, w2, a0, a2, g2, k_k, k_a, r_k, ln_g, ln_b):
    B, S, C = r.shape
    H, N = N_HEADS_RWKV, HEAD_RWKV
    f32 = jnp.float32
    logw = -jax.nn.softplus(-(w0 + jnp.tanh(wl) @ w2).astype(f32)) - 0.5
    log_decay = -jnp.exp(logw)
    a = jax.nn.sigmoid((a0 + al @ a2).astype(f32))
    g = jax.nn.sigmoid(gl) @ g2
    heads = lambda t: t.reshape(B, S, H, N)
    kk = heads(k * k_k)
    kk = kk / jnp.maximum(jnp.linalg.norm(kk, axis=-1, keepdims=True), 1e-12)
    k = k * (1.0 + (a - 1.0) * k_a)
    rh, kh, vh = heads(r), heads(k), heads(v)
    y = heads(_rwkv_scan(r, log_decay, k, v, kk.reshape(B, S, C), a, H))
    mu = jnp.mean(y, axis=-1, keepdims=True)
    var = jnp.mean(jnp.square(y - mu), axis=-1, keepdims=True)
    y = ((y - mu) * lax.rsqrt(var + RWKV_GN_EPS)).reshape(B, S, C) * ln_g + ln_b
    bonus = jnp.sum(rh * kh * r_k, axis=-1, keepdims=True) * vh
    return (y + bonus.reshape(B, S, C)) * g


def _compress(t, pe, w1, w2):
    B, S, G, Dh = t.shape
    rep = CMP_BLOCK // CMP_STRIDE
    nc = S // CMP_STRIDE - rep + 1
    sub = t.reshape(B, S // CMP_STRIDE, CMP_STRIDE, G, Dh)
    blk = jnp.concatenate([sub[:, j:j + nc] for j in range(rep)], axis=2)
    blk = blk + pe[:, None, :]
    blk = blk.transpose(0, 1, 3, 2, 4).reshape(B, nc, G, CMP_BLOCK * Dh)
    out = jax.nn.gelu(blk @ w1) @ w2
    return out.transpose(0, 2, 1, 3)


def _nsa_attention(zt, gates_t, k_cmp, v_cmp, pe_k, w1_k, w2_k, pe_v, w1_v, w2_v, tables):
    bias_cmp_t, sel_tiles, win_tiles, c2s_t, onehot_t, n_cmp = tables
    B, S, _ = k_cmp.shape
    G, Dh = N_KV_GROUPS, HEAD_NSA
    bf16 = jnp.bfloat16
    ncp = bias_cmp_t.shape[2]
    kc = _compress(k_cmp.reshape(B, S, G, Dh), pe_k, w1_k, w2_k)
    vc = _compress(v_cmp.reshape(B, S, G, Dh), pe_v, w1_v, w2_v)
    pad = ((0, 0), (0, 0), (0, ncp - n_cmp), (0, 0))
    kc_p = jnp.pad(kc, pad).astype(bf16)
    vc_t = jnp.pad(vc, pad).transpose(0, 1, 3, 2).astype(bf16)
    n_sel = min(N_SELECT, S // SEL_BLOCK)
    o_t = _nsa_fused(zt, gates_t, kc_p, vc_t, bias_cmp_t, sel_tiles, win_tiles, c2s_t, onehot_t,
                     n_sel, n_cmp)
    return o_t.reshape(B, NSA_Q_WIDTH, S)


def _split_in_proj(w_in):
    D = w_in.shape[0]
    scale = HEAD_NSA ** -0.5
    assert math.frexp(scale)[0] == 0.5
    o = np.cumsum((SHIFT_WIDTH,) + REST_SPLITS)
    col = lambda a, b: w_in[:, o[a]:o[b]]
    row_part = [w_in[:, :SHIFT_WIDTH], col(1, 3)]
    lead = SHIFT_WIDTH + int(o[3] - o[1])
    lead_pad = (-lead) % D_MODEL
    w_row = jnp.concatenate(row_part + [jnp.zeros((D, lead_pad), w_in.dtype), col(8, 10)], axis=1)
    gate_blk = (lead + lead_pad) // D_MODEL
    assert w_row.shape[1] % IN_PROJ_TN == 0
    G, n_g = N_KV_GROUPS, 3 * HEADS_PER_GROUP
    gates = col(7, 8).reshape(D, G, n_g)
    gates = jnp.pad(gates, ((0, 0), (0, 0), (0, NSA_GATE_ROWS - n_g))).reshape(D, G * NSA_GATE_ROWS)
    w_t = jnp.concatenate([col(0, 1) * scale, col(3, 7), gates], axis=1).T
    return w_row.astype(jnp.bfloat16), w_t.astype(jnp.bfloat16), gate_blk, int(o[7] - o[3]) + NSA_Q_WIDTH


def _token_mixer(x, w_in, shift_mu, rw_w0, rw_w2, rw_a0, rw_a2, rw_g2, rw_kk, rw_ka, rw_rk,
                 rw_ln_g, rw_ln_b, cmp_pe_k, cmp_w1_k, cmp_w2_k, cmp_pe_v, cmp_w1_v, cmp_w2_v,
                 w_up_rwkv, w_up_nsa, w_out, ln_g, ln_b, rel_bias, pos):
    B, S, D = x.shape
    xf = x.reshape(B * S, D)
    w_row, w_t, gate_blk, n_main = _split_in_proj(w_in)
    z = _matmul(xf, w_row, IN_PROJ_TM, IN_PROJ_TN)
    zt, gates_t = _matmul_t(xf, w_t, n_main, B, S)
    z3 = z.reshape(B, S, -1)
    k_cmp = z3[..., SHIFT_WIDTH:SHIFT_WIDTH + NSA_KV_WIDTH]
    v_cmp = z3[..., SHIFT_WIDTH + NSA_KV_WIDTH:SHIFT_WIDTH + 2 * NSA_KV_WIDTH]
    y_rw = _rwkv_fused(z, B, S, shift_mu, rw_w0, rw_w2, rw_a0, rw_a2, rw_g2,
                       rw_kk, rw_ka, rw_rk, rw_ln_g, rw_ln_b)
    y_nsa_t = _nsa_attention(zt, gates_t, k_cmp, v_cmp, cmp_pe_k, cmp_w1_k, cmp_w2_k,
                             cmp_pe_v, cmp_w1_v, cmp_w2_v, pos)
    bf16 = jnp.bfloat16
    return _merge_out(y_rw, y_nsa_t, z, gate_blk, xf,
                      w_up_rwkv.astype(bf16), w_up_nsa.astype(bf16), w_out.astype(bf16),
                      ln_g, ln_b).reshape(B, S, D)


def _hier_moe(x, wg, bg, we, be, w1, w3, w2, layer, ln_g, ln_b):
    B, S, D = x.shape
    N = B * S
    f32 = jnp.float32
    xf = x.reshape(N, D)
    g_prob = jax.nn.softmax((xf @ wg + bg).astype(f32), axis=-1)
    grp = jnp.argmax(g_prob, axis=-1)
    p_grp = jnp.take_along_axis(g_prob, grp[:, None], axis=1)[:, 0]
    e_logits = (xf @ we + be).astype(f32).reshape(N, N_GROUPS, EXPERTS_PER_GROUP)
    e_logits = jnp.take_along_axis(e_logits, grp[:, None, None], axis=1)[:, 0]
    top_v, top_i = lax.top_k(e_logits, TOP_K_INNER)
    top_w = jax.nn.softmax(top_v, axis=-1) * p_grp[:, None]
    eid = (grp[:, None] * EXPERTS_PER_GROUP + top_i).astype(jnp.int32)
    n_slots = N * TOP_K_INNER
    n_rows = n_slots + N_EXPERTS * MOE_BLOCK
    n_blk = n_rows // MOE_BLOCK
    dest, counts = _moe_dest(eid.T.reshape(1, n_slots), N_EXPERTS, MOE_BLOCK)
    dest2 = dest.reshape(TOP_K_INNER, N)
    pcounts = (counts + MOE_BLOCK - 1) // MOE_BLOCK * MOE_BLOCK
    pends = jnp.cumsum(pcounts)
    blk_exp = jnp.minimum(jnp.sum(jnp.arange(n_blk)[:, None] * MOE_BLOCK >= pends[None, :], axis=1),
                          N_EXPERTS - 1).astype(jnp.int32)
    xs = _moe_dispatch(xf, dest2, n_rows)
    ys = _moe_experts(xs, blk_exp, w1, w3, w2, layer, MOE_BLOCK)
    return _moe_combine(ys, dest2, top_w, xf, ln_g, ln_b).reshape(B, S, D)


def kernel(x, rel_bias, w_in, shift_mu, rw_w0, rw_w2, rw_a0, rw_a2, rw_g2, rw_kk, rw_ka, rw_rk,
           rw_ln_g, rw_ln_b, cmp_pe_k, cmp_w1_k, cmp_w2_k, cmp_pe_v, cmp_w1_v, cmp_w2_v,
           w_up_rwkv, w_up_nsa, w_out, ln1_g, ln1_b, router_group_w, router_group_b,
           router_expert_w, router_expert_b, exp_w1, exp_w3, exp_w2, ln2_g, ln2_b):
    pos = _nsa_tables(rel_bias, x.shape[1])
    for l in range(DEPTH):
        x = _token_mixer(x, w_in[l], shift_mu[l], rw_w0[l], rw_w2[l], rw_a0[l], rw_a2[l], rw_g2[l],
                         rw_kk[l], rw_ka[l], rw_rk[l], rw_ln_g[l], rw_ln_b[l],
                         cmp_pe_k[l], cmp_w1_k[l], cmp_w2_k[l], cmp_pe_v[l], cmp_w1_v[l], cmp_w2_v[l],
                         w_up_rwkv[l], w_up_nsa[l], w_out[l], ln1_g[l], ln1_b[l], rel_bias, pos)
        x = _hier_moe(x, router_group_w[l], router_group_b[l], router_expert_w[l], router_expert_b[l],
                      exp_w1, exp_w3, exp_w2, l, ln2_g[l], ln2_b[l])
    return x
```

```python
import functools
import math

import jax
import jax.numpy as jnp
import numpy as np
from jax import lax
from jax.experimental import pallas as pl
from jax.experimental.pallas import tpu as pltpu

D_MODEL = 1024
DEPTH = 4
N_HEADS_RWKV = 8
HEAD_RWKV = 64
RWKV_WIDTH = N_HEADS_RWKV * HEAD_RWKV
LORA_W = 64
LORA_A = 64
LORA_G = 128
RWKV_GN_EPS = 64e-5
N_HEADS_NSA = 8
N_KV_GROUPS = 2
HEADS_PER_GROUP = N_HEADS_NSA // N_KV_GROUPS
HEAD_NSA = 64
NSA_Q_WIDTH = N_HEADS_NSA * HEAD_NSA
NSA_KV_WIDTH = N_KV_GROUPS * HEAD_NSA
CMP_BLOCK = 32
CMP_STRIDE = 16
CMP_HIDDEN = 128
SEL_BLOCK = 64
N_SELECT = 16
WINDOW = 512
Q_BLOCK = 128
N_BAND = WINDOW // Q_BLOCK + 1
NEG_INF = -1e30
FORCED_SCORE = 1e4
NUM_BUCKETS = 32
MAX_DISTANCE = 1024
N_GROUPS = 4
EXPERTS_PER_GROUP = 8
N_EXPERTS = N_GROUPS * EXPERTS_PER_GROUP
TOP_K_INNER = 2
D_EXPERT = 512
MOE_BLOCK = 256
ALPHA = (2 * DEPTH) ** 0.25
LN_EPS = 1e-5
SHIFT_SPLITS = (RWKV_WIDTH, RWKV_WIDTH, RWKV_WIDTH, LORA_W, LORA_A, LORA_G)
SHIFT_WIDTH = 3 * RWKV_WIDTH + LORA_W + LORA_A + LORA_G
REST_SPLITS = (NSA_Q_WIDTH,) + (NSA_KV_WIDTH,) * 6 + (3 * N_HEADS_NSA, D_MODEL, D_MODEL)

VMEM_LIMIT = 48 * 1024 * 1024


IN_PROJ_TM = 1024
IN_PROJ_TN = 1024


def _mm_body(x_ref, w_ref, o_ref, xb_ref):
    @pl.when(pl.program_id(1) == 0)
    def _():
        xb_ref[...] = x_ref[...].astype(jnp.bfloat16)

    o_ref[...] = jnp.dot(xb_ref[...], w_ref[...], preferred_element_type=jnp.float32)


def _matmul(x, w, tm, tn):
    m, k = x.shape
    n = w.shape[1]
    assert m % tm == 0 and n % tn == 0
    return pl.pallas_call(
        _mm_body,
        grid=(m // tm, n // tn),
        in_specs=[pl.BlockSpec((tm, k), lambda i, j: (i, 0)),
                  pl.BlockSpec((k, tn), lambda i, j: (0, j))],
        out_specs=pl.BlockSpec((tm, tn), lambda i, j: (i, j)),
        out_shape=jax.ShapeDtypeStruct((m, n), jnp.float32),
        scratch_shapes=[pltpu.VMEM((tm, k), jnp.bfloat16)],
        compiler_params=pltpu.CompilerParams(dimension_semantics=("arbitrary", "arbitrary"),
                                             vmem_limit_bytes=VMEM_LIMIT),
        name="in_proj",
    )(x, w)


IN_PROJ_T_TM = 512
NSA_GATE_ROWS = 16


def _mm_t_body(n_main, x_ref, wt_ref, zt_ref, gt_ref):
    xb = x_ref[...].astype(jnp.bfloat16)
    out = lax.dot_general(wt_ref[...], xb, (((1,), (1,)), ((), ())), preferred_element_type=jnp.float32)
    zt_ref[0] = out[:n_main].astype(zt_ref.dtype)
    gt_ref[0] = out[n_main:]


def _matmul_t(x, wt, n_main, batch, seq):
    m, k = x.shape
    r = wt.shape[0]
    tm = min(IN_PROJ_T_TM, seq)
    assert seq % tm == 0 and m == batch * seq
    per_b = seq // tm
    return pl.pallas_call(
        functools.partial(_mm_t_body, n_main),
        grid=(m // tm,),
        in_specs=[pl.BlockSpec((tm, k), lambda i: (i, 0)),
                  pl.BlockSpec((r, k), lambda i: (0, 0))],
        out_specs=[pl.BlockSpec((1, n_main, tm), lambda i: (i // per_b, 0, i % per_b)),
                   pl.BlockSpec((1, r - n_main, tm), lambda i: (i // per_b, 0, i % per_b))],
        out_shape=[jax.ShapeDtypeStruct((batch, n_main, seq), jnp.bfloat16),
                   jax.ShapeDtypeStruct((batch, r - n_main, seq), jnp.float32)],
        compiler_params=pltpu.CompilerParams(dimension_semantics=("arbitrary",),
                                             vmem_limit_bytes=VMEM_LIMIT),
        name="in_proj_t",
    )(x, wt)


MERGE_TM = 256


def _merge_body(yr_ref, yn_ref, grw_ref, gns_ref, x_ref, wur_ref, wun_ref, wo_ref, g_ref, b_ref, o_ref):
    f32, bf16 = jnp.float32, jnp.bfloat16
    up_r = jnp.dot(yr_ref[...].astype(bf16), wur_ref[...], preferred_element_type=f32)
    up_n = lax.dot_general(yn_ref[0].astype(bf16), wun_ref[...], _TN, preferred_element_type=f32)
    merged = jax.nn.sigmoid(grw_ref[...]) * up_r + jax.nn.sigmoid(gns_ref[...]) * up_n
    y = ALPHA * x_ref[...] + jnp.dot(merged.astype(bf16), wo_ref[...], preferred_element_type=f32)
    mu = jnp.mean(y, axis=-1, keepdims=True)
    var = jnp.mean(jnp.square(y - mu), axis=-1, keepdims=True)
    o_ref[...] = (y - mu) * lax.rsqrt(var + LN_EPS) * g_ref[...] + b_ref[...]


def _merge_out(y_rw, y_nsa_t, z, gate_blk, x, w_up_rwkv, w_up_nsa, w_out, ln_g, ln_b):
    m, d = x.shape
    tm = MERGE_TM
    nb, wn, seq = y_nsa_t.shape
    assert m % tm == 0 and seq % tm == 0
    per_b = seq // tm
    row = lambda width: pl.BlockSpec((tm, width), lambda i: (i, 0))
    full = lambda a: pl.BlockSpec(a.shape, lambda i: (0,) * a.ndim)
    ln_g, ln_b = ln_g.reshape(1, d), ln_b.reshape(1, d)
    return pl.pallas_call(
        _merge_body,
        grid=(m // tm,),
        in_specs=[row(y_rw.shape[1]), pl.BlockSpec((1, wn, tm), lambda i: (i // per_b, 0, i % per_b)),
                  pl.BlockSpec((tm, d), lambda i: (i, gate_blk)),
                  pl.BlockSpec((tm, d), lambda i: (i, gate_blk + 1)),
                  row(d), full(w_up_rwkv), full(w_up_nsa), full(w_out), full(ln_g), full(ln_b)],
        out_specs=row(d),
        out_shape=jax.ShapeDtypeStruct((m, d), jnp.float32),
        compiler_params=pltpu.CompilerParams(dimension_semantics=("arbitrary",),
                                             vmem_limit_bytes=VMEM_LIMIT),
        name="merge_out",
    )(y_rw, y_nsa_t, z, z, x, w_up_rwkv, w_up_nsa, w_out, ln_g, ln_b)


def _topk_mask_body(n_sel, s_ref, o_ref):
    s = s_ref[0, 0]
    ns = s.shape[0]
    jidx = lax.broadcasted_iota(jnp.int32, s.shape, 0)
    cnt = jnp.zeros(s.shape, jnp.float32)
    for jp in range(ns):
        row = s[jp:jp + 1, :]
        tie = jnp.where(jidx > jp, 1.0, 0.0)
        cnt = cnt + jnp.where(row > s, 1.0, jnp.where(row == s, tie, 0.0))
    o_ref[0, 0] = jnp.where(cnt < n_sel, 1.0, 0.0)


def _topk_mask(scores_t, n_sel, tq=512):
    B, G, NS, S = scores_t.shape
    tq = min(tq, S)
    return pl.pallas_call(
        functools.partial(_topk_mask_body, float(n_sel)),
        grid=(B, G, S // tq),
        in_specs=[pl.BlockSpec((1, 1, NS, tq), lambda b, g, i: (b, g, 0, i))],
        out_specs=pl.BlockSpec((1, 1, NS, tq), lambda b, g, i: (b, g, 0, i)),
        out_shape=jax.ShapeDtypeStruct((B, G, NS, S), jnp.float32),
        compiler_params=pltpu.CompilerParams(dimension_semantics=("arbitrary",) * 3),
        name="topk_mask",
    )(scores_t)


def _sel_attn_body(n_delta, q_ref, k_ref, v_ref, m_ref, t_ref, o_ref, acc_ref):
    i = pl.program_id(2)
    hg, dh, qb = q_ref.shape[2], q_ref.shape[3], q_ref.shape[4]
    ns = m_ref.shape[2]
    maskf = m_ref[0, 0].astype(jnp.bfloat16)
    key = lax.broadcasted_iota(jnp.int32, (qb, ns), 0)
    blk = lax.broadcasted_iota(jnp.int32, (qb, ns), 1)
    per = qb // SEL_BLOCK
    acc_ref[...] = jnp.zeros_like(acc_ref)

    def body(j, carry):
        ms, ls = carry
        off = pl.multiple_of(j * qb, qb)
        k = k_ref[0, 0, pl.ds(off, qb), :]
        v = v_ref[0, 0, :, pl.ds(off, qb)]
        expand = jnp.where(blk == j * per + key // SEL_BLOCK, 1.0, 0.0).astype(jnp.bfloat16)
        selm = jnp.dot(expand, maskf, preferred_element_type=jnp.float32) > 0.5
        d = jnp.minimum(i - j, n_delta)
        new_ms, new_ls = [], []
        for h in range(hg):
            lg = jnp.dot(k, q_ref[0, 0, h], preferred_element_type=jnp.float32) + t_ref[0, h, d]
            lg = jnp.where(selm, lg, NEG_INF)
            m_new = jnp.maximum(ms[h], jnp.max(lg, axis=0, keepdims=True))
            p = jnp.exp(lg - m_new)
            alpha = jnp.exp(ms[h] - m_new)
            new_ls.append(alpha * ls[h] + jnp.sum(p, axis=0, keepdims=True))
            acc_ref[h] = alpha * acc_ref[h] + jnp.dot(v, p.astype(jnp.bfloat16),
                                                      preferred_element_type=jnp.float32)
            new_ms.append(m_new)
        return tuple(new_ms), tuple(new_ls)

    init = (tuple(jnp.full((1, qb), NEG_INF, jnp.float32) for _ in range(hg)),
            tuple(jnp.zeros((1, qb), jnp.float32) for _ in range(hg)))
    ms, ls = lax.fori_loop(0, i + 1, body, init)
    for h in range(hg):
        o_ref[0, 0, h] = acc_ref[h] / ls[h]


def _sel_attention(q_t, ks, vs_t, mask_t, bias_tiles):
    B, G, Hg, Dh, S = q_t.shape
    NS = mask_t.shape[2]
    ND = bias_tiles.shape[2] - 1
    QB = bias_tiles.shape[-1]
    return pl.pallas_call(
        functools.partial(_sel_attn_body, ND),
        grid=(B, G, S // QB),
        in_specs=[pl.BlockSpec((1, 1, Hg, Dh, QB), lambda b, g, i: (b, g, 0, 0, i)),
                  pl.BlockSpec((1, 1, S, Dh), lambda b, g, i: (b, g, 0, 0)),
                  pl.BlockSpec((1, 1, Dh, S), lambda b, g, i: (b, g, 0, 0)),
                  pl.BlockSpec((1, 1, NS, QB), lambda b, g, i: (b, g, 0, i)),
                  pl.BlockSpec((1, Hg, ND + 1, QB, QB), lambda b, g, i: (g, 0, 0, 0, 0))],
        out_specs=pl.BlockSpec((1, 1, Hg, Dh, QB), lambda b, g, i: (b, g, 0, 0, i)),
        out_shape=jax.ShapeDtypeStruct((B, G, Hg, Dh, S), jnp.float32),
        scratch_shapes=[pltpu.VMEM((Hg, Dh, QB), jnp.float32)],
        compiler_params=pltpu.CompilerParams(dimension_semantics=("arbitrary",) * 3,
                                             vmem_limit_bytes=VMEM_LIMIT),
        name="sel_attention",
    )(q_t, ks, vs_t, mask_t, bias_tiles)


def _sel_bias_tiles(rel_bias, seq):
    nd = -(-MAX_DISTANCE // Q_BLOCK)
    nd = min(nd, seq // Q_BLOCK)
    half = NUM_BUCKETS // 2
    min_far = nd * Q_BLOCK - (Q_BLOCK - 1)
    assert half + math.log(min_far / half) / math.log(MAX_DISTANCE / half) * half >= NUM_BUCKETS - 0.75
    kj = np.arange(Q_BLOCK)[:, None]
    qi = np.arange(Q_BLOCK)[None, :]
    dist = np.arange(nd + 1)[:, None, None] * Q_BLOCK + qi - kj
    dist[nd] = max(seq - 1, nd * Q_BLOCK)
    tiles = _bias_from_buckets(rel_bias, _t5_bucket(jnp.asarray(np.maximum(dist, 0))))
    causal = jnp.asarray(dist >= 0)
    return jnp.where(causal, tiles, NEG_INF)


SEL_KEY_TILE = 256


def _flash_update(lgs, v_ts, ms, ls, acc_ref):
    H = range(len(lgs))
    m_new = []
    for h in H:
        m = ms[h]
        for lg in lgs[h]:
            m = jnp.maximum(m, jnp.max(lg, axis=0, keepdims=True))
        m_new.append(m)
    ps = [[jnp.exp(lg - m_new[h]) for lg in lgs[h]] for h in H]
    alpha = [jnp.exp(ms[h] - m_new[h]) for h in H]
    l_new = []
    for h in H:
        l = alpha[h] * ls[h]
        for p in ps[h]:
            l = l + jnp.sum(p, axis=0, keepdims=True)
        l_new.append(l)
    pv = [[jnp.dot(v_t, p.astype(jnp.bfloat16), preferred_element_type=jnp.float32)
           for p, v_t in zip(ps[h], v_ts)] for h in H]
    for h in H:
        acc = alpha[h] * acc_ref[h]
        for o in pv[h]:
            acc = acc + o
        acc_ref[h] = acc
    return tuple(m_new), tuple(l_new)


def _nsa_body(n_sel, n_cmp, n_delta, q_ref, kc_ref, vc_ref, ks_ref, vs_ref, kw_ref, vw_ref, gl_ref,
              bc_ref, ts_ref, tw_ref, c2s_ref, oh_ref, o_ref, acc_s, acc_w, ocmp_ref):
    i = pl.program_id(2)
    hg, dh, qb = q_ref.shape[1], q_ref.shape[2], q_ref.shape[3]
    ncp = kc_ref.shape[2]
    ns = c2s_ref.shape[0]
    f32, bf16 = jnp.float32, jnp.bfloat16
    H = range(hg)

    cidx = lax.broadcasted_iota(jnp.int32, (ncp, qb), 0)
    tpos = lax.broadcasted_iota(jnp.int32, (ncp, qb), 1) + i * qb
    valid = (tpos - (cidx * CMP_STRIDE + CMP_BLOCK - 1)) >= 0
    fill = jnp.where(cidx < n_cmp, NEG_INF, 2.0 * NEG_INF)
    kc = kc_ref[0, 0]
    vc_t = vc_ref[0, 0]
    qkc = [jnp.dot(kc, q_ref[0, h], preferred_element_type=f32) for h in H]
    lgc = [jnp.where(valid, qkc[h] + bc_ref[0, h], fill) for h in H]
    mc = [jnp.max(lgc[h], axis=0, keepdims=True) for h in H]
    pc = [jnp.exp(lgc[h] - mc[h]) for h in H]
    inv = [1.0 / jnp.sum(pc[h], axis=0, keepdims=True) for h in H]
    pc = [jnp.where(valid, pc[h] * inv[h], 0.0) for h in H]
    psum = pc[0]
    for h in H[1:]:
        psum = psum + pc[h]
    for h in H:
        ocmp_ref[h] = jnp.dot(vc_t, pc[h].astype(bf16), preferred_element_type=f32)

    score = jnp.dot(c2s_ref[...], psum.astype(bf16), preferred_element_type=f32)
    jblk = lax.broadcasted_iota(jnp.int32, (ns, qb), 0)
    cur = (lax.broadcasted_iota(jnp.int32, (ns, qb), 1) + i * qb) // SEL_BLOCK
    forced = (jblk == 0) | (jblk == cur) | (jblk == cur - 1)
    score = jnp.where(forced, FORCED_SCORE, jnp.where(jblk <= cur, score, -1.0))
    cnt = jnp.zeros((ns, qb), f32)
    for jp in range(ns):
        row = score[jp:jp + 1, :]
        tie = jnp.where(jblk > jp, 1.0, 0.0)
        cnt = cnt + jnp.where(row > score, 1.0, jnp.where(row == score, tie, 0.0))
    maskneg = jnp.where(cnt < n_sel, 0.0, NEG_INF).astype(bf16)
    q_aug = [jnp.concatenate([q_ref[0, h], maskneg], axis=0) for h in H]

    kt = SEL_KEY_TILE
    per = kt // qb
    last_tile = i // per
    acc_s[...] = jnp.zeros_like(acc_s)

    def sel_step(jj, carry):
        ms, ls = carry
        ks_, vs_, tiles = [], [], []
        for u in range(2):
            j = 2 * jj + u
            off = pl.multiple_of(jnp.minimum(j, last_tile) * kt, kt)
            ks_.append(jnp.concatenate([ks_ref[0, 0, :, pl.ds(off, kt)], oh_ref[:, pl.ds(off, kt)]], axis=0))
            vs_.append(vs_ref[0, 0, :, pl.ds(off, kt)])
            tiles.append([jnp.clip(i - (j * per + w), -1, n_delta) + 1 for w in range(per)])
        qk = [[lax.dot_general(ks_[u], q_aug[h], _TN, preferred_element_type=f32) for u in range(2)] for h in H]
        lgs = [[qk[h][u] + jnp.concatenate([ts_ref[0, h, tiles[u][w]] for w in range(per)], axis=0)
                for u in range(2)] for h in H]
        return _flash_update(lgs, vs_, ms, ls, acc_s)

    init = (tuple(jnp.full((1, qb), NEG_INF, f32) for _ in H), tuple(jnp.zeros((1, qb), f32) for _ in H))
    _, ls_s = lax.fori_loop(0, last_tile // 2 + 1, sel_step, init)

    acc_w[...] = jnp.zeros_like(acc_w)
    kws, vws, widx = [], [], []
    for u in range(N_BAND):
        j = i - (N_BAND - 1) + u
        off = pl.multiple_of(jnp.maximum(j, 0) * qb, qb)
        kws.append(kw_ref[0, 0, :, pl.ds(off, qb)])
        vws.append(vw_ref[0, 0, :, pl.ds(off, qb)])
        widx.append(jnp.where(j >= 0, N_BAND - 1 - u, N_BAND))
    qk = [[lax.dot_general(kws[u], q_ref[0, h], _TN, preferred_element_type=f32) for u in range(N_BAND)] for h in H]
    lgs = [[qk[h][u] + tw_ref[0, h, widx[u]] for u in range(N_BAND)] for h in H]
    _, ls_w = _flash_update(lgs, vws, init[0], init[1], acc_w)

    gates = jax.nn.sigmoid(gl_ref[0])
    for h in H:
        o_ref[0, 0, h] = (gates[3 * h:3 * h + 1] * ocmp_ref[h]
                          + gates[3 * h + 1:3 * h + 2] * (acc_s[h] / ls_s[h])
                          + gates[3 * h + 2:3 * h + 3] * (acc_w[h] / ls_w[h]))


def _nsa_fused(zt, gates_t, kc, vc_t, bias_cmp_t, sel_tiles, win_tiles, c2s_t, onehot_t, n_sel, n_cmp):
    B, _, S = zt.shape
    G, Hg, Dh = N_KV_GROUPS, HEADS_PER_GROUP, HEAD_NSA
    NCP = kc.shape[2]
    NS = c2s_t.shape[0]
    QB = Q_BLOCK
    ND = sel_tiles.shape[2] - 2
    assert S % SEL_KEY_TILE == 0 and SEL_KEY_TILE % QB == 0
    z4 = zt.reshape(B, -1, Dh, S)
    stream = lambda n: pl.BlockSpec((1, 1, Dh, S), lambda b, g, i: (b, G * Hg + n * G + g, 0, 0))
    bg = lambda *blk: pl.BlockSpec((1, 1) + blk, lambda b, g, i: (b, g) + (0,) * len(blk))
    gt = lambda arr: pl.BlockSpec((1,) + arr.shape[1:], lambda b, g, i: (g,) + (0,) * (arr.ndim - 1))
    return pl.pallas_call(
        functools.partial(_nsa_body, float(n_sel), n_cmp, ND),
        grid=(B, G, S // QB),
        in_specs=[pl.BlockSpec((1, Hg, Dh, QB), lambda b, g, i: (b, g, 0, i)),
                  bg(NCP, Dh), bg(Dh, NCP), stream(0), stream(1), stream(2), stream(3),
                  pl.BlockSpec((1, NSA_GATE_ROWS, QB), lambda b, g, i: (b, g, i)),
                  pl.BlockSpec((1, Hg, NCP, QB), lambda b, g, i: (g, 0, 0, i)),
                  gt(sel_tiles), gt(win_tiles),
                  pl.BlockSpec(c2s_t.shape, lambda b, g, i: (0, 0)),
                  pl.BlockSpec(onehot_t.shape, lambda b, g, i: (0, 0))],
        out_specs=pl.BlockSpec((1, 1, Hg, Dh, QB), lambda b, g, i: (b, g, 0, 0, i)),
        out_shape=jax.ShapeDtypeStruct((B, G, Hg, Dh, S), jnp.float32),
        scratch_shapes=[pltpu.VMEM((Hg, Dh, QB), jnp.float32)] * 3,
        compiler_params=pltpu.CompilerParams(dimension_semantics=("arbitrary",) * 3,
                                             vmem_limit_bytes=VMEM_LIMIT),
        name="nsa_fused",
    )(z4, kc, vc_t, z4, z4, z4, z4, gates_t, bias_cmp_t, sel_tiles, win_tiles, c2s_t, onehot_t)


def _nsa_tables(rel_bias, seq):
    n_cmp = seq // CMP_STRIDE - CMP_BLOCK // CMP_STRIDE + 1
    ncp = -(-n_cmp // 128) * 128
    c = jnp.arange(ncp)[:, None]
    t = jnp.arange(seq)[None, :]
    bias_cmp_t = _bias_from_buckets(rel_bias, _t5_bucket(t - (c * CMP_STRIDE + CMP_BLOCK - 1)))
    base = _sel_bias_tiles(rel_bias, seq)
    sel_tiles = jnp.concatenate([jnp.full_like(base[:, :, :1], NEG_INF), base], axis=2)
    kj = np.arange(Q_BLOCK)[:, None]
    qi = np.arange(Q_BLOCK)[None, :]
    dwin = np.arange(N_BAND)[:, None, None] * Q_BLOCK + qi - kj
    nwin = min(N_BAND, base.shape[2])
    win_tiles = jnp.where(jnp.asarray(dwin[:nwin] < WINDOW), base[:, :, :nwin], NEG_INF)
    win_tiles = jnp.concatenate(
        [win_tiles, jnp.full(win_tiles.shape[:2] + (N_BAND + 1 - nwin,) + win_tiles.shape[3:], NEG_INF)], axis=2)
    n_sel_blocks = seq // SEL_BLOCK
    cs = np.arange(ncp)[None, :] * CMP_STRIDE
    ss = np.arange(n_sel_blocks)[:, None] * SEL_BLOCK
    overlap = np.clip(np.minimum(cs + CMP_BLOCK, ss + SEL_BLOCK) - np.maximum(cs, ss), 0, None)
    overlap = np.where(np.arange(ncp)[None, :] < n_cmp, overlap, 0)
    c2s_t = jnp.asarray(overlap.astype(np.float32) / CMP_BLOCK, jnp.bfloat16)
    onehot_t = jnp.asarray(np.arange(n_sel_blocks)[:, None] == np.arange(seq)[None, :] // SEL_BLOCK, jnp.bfloat16)
    return bias_cmp_t, sel_tiles, win_tiles, c2s_t, onehot_t, n_cmp


RWKV_CHUNK = 64
RWKV_INV_BASE = 8


def _split_bf16(x):
    hi = x.astype(jnp.bfloat16)
    lo = (x - hi.astype(jnp.float32)).astype(jnp.bfloat16)
    return hi, lo


def _dot3(a, b, dims=(((1,), (0,)), ((), ()))):
    a_hi, a_lo = _split_bf16(a)
    b_hi, b_lo = _split_bf16(b)
    d = functools.partial(lax.dot_general, dimension_numbers=dims, preferred_element_type=jnp.float32)
    free_axis = 1 - dims[0][0][0]
    m = a.shape[free_axis]
    both = d(jnp.concatenate([a_hi, a_lo], axis=free_axis), b_hi)
    return both[:m] + (both[m:] + d(a_hi, b_lo))


_NT = (((1,), (1,)), ((), ()))
_TN = (((0,), (0,)), ((), ()))


def _rwkv_chunk(r_w, lw, k_w, v_w, kk, lr, st_ref, n_heads):
    C = r_w.shape[0]
    N = r_w.shape[1] // n_heads
    H = range(n_heads)
    f32 = jnp.float32
    row = lax.broadcasted_iota(jnp.int32, (C, C), 0)
    col = lax.broadcasted_iota(jnp.int32, (C, C), 1)
    strict = col < row
    incl = col <= row
    eye = jnp.where(row == col, 1.0, 0.0).astype(f32)
    tri = jnp.where(incl, 1.0, 0.0).astype(jnp.bfloat16)

    cl = _dot_exact_lhs(tri, lw)

    base = RWKV_INV_BASE
    diag_blk = strict & ((row // base) == (col // base))
    level_masks = []
    s = base
    while s < C:
        level_masks.append(strict & ((row // (2 * s)) == (col // (2 * s))) & ((row // s) != (col // s)))
        s *= 2

    cl_end = cl[C - 1:C, :]
    bb = kk * lr
    g_inv = jnp.exp(-cl)
    g_rem = jnp.exp(cl_end - cl)
    g_end = jnp.exp(cl_end)
    abar_w = -kk * jnp.exp(cl - lw)
    rbar_w = r_w * jnp.exp(cl)
    bbar_w = bb * g_inv
    kbar_w = k_w * g_inv
    bhat_w = bb * g_rem
    khat_w = k_w * g_rem
    hs = lambda x, h: x[:, h * N:(h + 1) * N]

    abar = [hs(abar_w, h) for h in H]
    rbar = [hs(rbar_w, h) for h in H]
    v = [hs(v_w, h) for h in H]
    gmat = [_dot3(jnp.concatenate([abar[h], rbar[h]], axis=0),
                  jnp.concatenate([hs(bbar_w, h), hs(kbar_w, h)], axis=0), _NT) for h in H]
    a_ab = [jnp.where(strict, gmat[h][:C, :C], 0.0) for h in H]
    a_ak = [jnp.where(strict, gmat[h][:C, C:], 0.0) for h in H]
    m_rb = [jnp.where(incl, gmat[h][C:, :C], 0.0) for h in H]
    m_rk = [jnp.where(incl, gmat[h][C:, C:], 0.0) for h in H]
    akv = [_dot3(a_ak[h], v[h]) for h in H]
    dp = [jnp.where(diag_blk, a_ab[h], 0.0) for h in H]
    x = [eye + dp[h] for h in H]
    s = 2
    while s < base:
        dp = [_dot3(dp[h], dp[h]) for h in H]
        x = [x[h] + _dot3(x[h], dp[h]) for h in H]
        s *= 2
    for lm in level_masks:
        t = [_dot3(jnp.where(lm, a_ab[h], 0.0), x[h]) for h in H]
        x = [x[h] + _dot3(x[h], t[h]) for h in H]
    xw = [_dot3(x[h], jnp.concatenate([akv[h], abar[h]], axis=1)) for h in H]
    uv = [jnp.concatenate([xw[h][:, :N], v[h]], axis=0) for h in H]
    atil = [xw[h][:, N:] for h in H]
    y_loc = [_dot3(jnp.concatenate([m_rb[h], m_rk[h]], axis=1), uv[h]) for h in H]
    qm = [rbar[h] + _dot3(m_rb[h], atil[h]) for h in H]
    s_loc = [_dot3(jnp.concatenate([hs(bhat_w, h), hs(khat_w, h)], axis=0), uv[h], _TN) for h in H]
    pm = [eye[:N, :N] * hs(g_end, h) + _dot3(hs(bhat_w, h), atil[h], _TN) for h in H]
    s0 = [st_ref[h] for h in H]
    y = [y_loc[h] + _dot3(qm[h], s0[h]) for h in H]
    for h in H:
        st_ref[h] = s_loc[h] + _dot3(pm[h], s0[h])
    return jnp.concatenate(y, axis=1)


def _bf16_pieces(x):
    f32 = jnp.float32
    p1 = x.astype(jnp.bfloat16)
    r1 = x - p1.astype(f32)
    p2 = r1.astype(jnp.bfloat16)
    p3 = (r1 - p2.astype(f32)).astype(jnp.bfloat16)
    return p1, p2, p3


def _dot_exact_lhs(a_bf16, x):
    dd = functools.partial(jnp.dot, preferred_element_type=jnp.float32)
    p1, p2, p3 = _bf16_pieces(x)
    return dd(a_bf16, p1) + (dd(a_bf16, p2) + dd(a_bf16, p3))


def _dot_exact_rhs(x, b_bf16):
    dd = functools.partial(jnp.dot, preferred_element_type=jnp.float32)
    p1, p2, p3 = _bf16_pieces(x)
    return dd(p1, b_bf16) + (dd(p2, b_bf16) + dd(p3, b_bf16))


def _rwkv_body(n_heads, z_ref, mu_ref, w0_ref, w2_ref, a0_ref, a2_ref, g2_ref, kk_ref, ka_ref, rk_ref,
               lng_ref, lnb_ref, y_ref, st_ref, prev_ref):
    c = pl.program_id(1)
    C = z_ref.shape[0]
    W = y_ref.shape[1]
    N = W // n_heads
    f32, bf16 = jnp.float32, jnp.bfloat16

    @pl.when(c == 0)
    def _():
        st_ref[...] = jnp.zeros_like(st_ref)
        prev_ref[...] = jnp.zeros_like(prev_ref)

    z = z_ref[...]
    z_prev = jnp.concatenate([prev_ref[...], z[:C - 1]], axis=0)
    prev_ref[...] = z[C - 1:C]
    zs = z + (z_prev - z) * mu_ref[...]
    o = np.cumsum((0,) + SHIFT_SPLITS)
    r, k, v, wl, al, gl = (zs[:, o[i]:o[i + 1]] for i in range(6))
    dd = functools.partial(jnp.dot, preferred_element_type=f32)
    yw = w0_ref[...] + dd(jnp.tanh(wl).astype(bf16), w2_ref[...])
    logw = -(jnp.maximum(-yw, 0.0) + jnp.log1p(jnp.exp(-jnp.abs(yw)))) - 0.5
    lw = -jnp.exp(logw)
    lr = jax.nn.sigmoid(a0_ref[...] + dd(al.astype(bf16), a2_ref[...]))
    g = dd(jax.nn.sigmoid(gl).astype(bf16), g2_ref[...])
    hrow = lax.broadcasted_iota(jnp.int32, (W, W), 0) // N
    hcol = lax.broadcasted_iota(jnp.int32, (W, W), 1) // N
    seg = jnp.where(hrow == hcol, 1.0, 0.0).astype(bf16)
    kk = k * kk_ref[...]
    kk = kk / jnp.maximum(jnp.sqrt(_dot_exact_rhs(kk * kk, seg)), 1e-12)
    k = k * (1.0 + (lr - 1.0) * ka_ref[...])
    y = _rwkv_chunk(r, lw, k, v, kk, lr, st_ref, n_heads)
    mu = _dot_exact_rhs(y, seg) * (1.0 / N)
    yc = y - mu
    var = _dot_exact_rhs(yc * yc, seg) * (1.0 / N)
    yn = yc * lax.rsqrt(var + RWKV_GN_EPS) * lng_ref[...] + lnb_ref[...]
    bonus = _dot_exact_rhs(r * k * rk_ref[...], seg) * v
    y_ref[...] = (yn + bonus) * g


def _rwkv_fused(z, batch, seq, shift_mu, w0, w2, a0, a2, g2, k_k, k_a, r_k, ln_g, ln_b):
    H, N, W = N_HEADS_RWKV, HEAD_RWKV, RWKV_WIDTH
    C = min(RWKV_CHUNK, seq)
    assert N <= C and seq % C == 0 and SHIFT_WIDTH % 128 == 0
    nc = seq // C
    bf16 = jnp.bfloat16
    row = lambda a: a.reshape(1, -1).astype(jnp.float32)
    params = [row(shift_mu), row(w0), w2.astype(bf16), row(a0), a2.astype(bf16), g2.astype(bf16),
              row(k_k), row(k_a), row(r_k), row(ln_g), row(ln_b)]
    full = lambda a: pl.BlockSpec(a.shape, lambda b, c: (0, 0))
    return pl.pallas_call(
        functools.partial(_rwkv_body, H),
        grid=(batch, nc),
        in_specs=[pl.BlockSpec((C, SHIFT_WIDTH), lambda b, c: (b * nc + c, 0))] + [full(p) for p in params],
        out_specs=pl.BlockSpec((C, W), lambda b, c: (b * nc + c, 0)),
        out_shape=jax.ShapeDtypeStruct((batch * seq, W), jnp.float32),
        scratch_shapes=[pltpu.VMEM((H, N, N), jnp.float32), pltpu.VMEM((1, SHIFT_WIDTH), jnp.float32)],
        compiler_params=pltpu.CompilerParams(dimension_semantics=("arbitrary", "arbitrary")),
        name="rwkv_fused",
    )(z, *params)


def _moe_body(be_ref, nu_ref, x_ref, w1_ref, w3_ref, w2_ref, o_ref):
    bf16 = jnp.bfloat16
    used = pl.program_id(0) < nu_ref[0]

    @pl.when(used)
    def _():
        x = x_ref[...].astype(bf16)
        h1 = jnp.dot(x, w1_ref[0, 0].astype(bf16), preferred_element_type=jnp.float32)
        h3 = jnp.dot(x, w3_ref[0, 0].astype(bf16), preferred_element_type=jnp.float32)
        h = (h1 * jax.nn.sigmoid(h1)) * h3
        o_ref[...] = jnp.dot(h.astype(bf16), w2_ref[0, 0].astype(bf16), preferred_element_type=jnp.float32)

    @pl.when(jnp.logical_not(used))
    def _():
        o_ref[...] = jnp.zeros_like(o_ref)


def _moe_experts(xs, blk_exp, n_used, w1, w3, w2, layer, blk):
    n_rows, D = xs.shape
    De = w1.shape[3]
    grid_spec = pltpu.PrefetchScalarGridSpec(
        num_scalar_prefetch=2,
        grid=(n_rows // blk,),
        in_specs=[pl.BlockSpec((blk, D), lambda i, be, nu: (i, 0)),
                  pl.BlockSpec((1, 1, D, De), lambda i, be, nu: (layer, be[i], 0, 0)),
                  pl.BlockSpec((1, 1, D, De), lambda i, be, nu: (layer, be[i], 0, 0)),
                  pl.BlockSpec((1, 1, De, D), lambda i, be, nu: (layer, be[i], 0, 0))],
        out_specs=pl.BlockSpec((blk, D), lambda i, be, nu: (i, 0)))
    return pl.pallas_call(
        _moe_body,
        grid_spec=grid_spec,
        out_shape=jax.ShapeDtypeStruct((n_rows, D), jnp.float32),
        compiler_params=pltpu.CompilerParams(dimension_semantics=("arbitrary",),
                                             vmem_limit_bytes=VMEM_LIMIT),
        name="moe_experts",
    )(blk_exp, n_used, xs, w1, w3, w2)


MOE_DEST_CHUNK = 512


def _moe_dest_body(n_exp, blk, eid_ref, dest_ref, cnt_ref, run_ref, pst_ref):
    ph, c = pl.program_id(0), pl.program_id(1)
    T = eid_ref.shape[1]
    f32, bf16 = jnp.float32, jnp.bfloat16
    onehot = jnp.where(lax.broadcasted_iota(jnp.int32, (n_exp, T), 0) == eid_ref[...], 1.0, 0.0)
    here = jnp.sum(onehot, axis=1, keepdims=True)

    @pl.when((ph == 0) & (c == 0))
    def _():
        run_ref[...] = jnp.zeros_like(run_ref)

    @pl.when(ph == 0)
    def _():
        run_ref[...] = run_ref[...] + here
        dest_ref[...] = jnp.zeros_like(dest_ref)

    @pl.when((ph == 1) & (c == 0))
    def _():
        counts = run_ref[...]
        cnt_ref[...] = counts
        padded = jnp.floor((counts + (blk - 1)) * (1.0 / blk)) * blk
        er = lax.broadcasted_iota(jnp.int32, (n_exp, n_exp), 0)
        ec = lax.broadcasted_iota(jnp.int32, (n_exp, n_exp), 1)
        pst_ref[...] = _dot_exact_lhs(jnp.where(ec < er, 1.0, 0.0).astype(bf16), padded)
        run_ref[...] = jnp.zeros_like(run_ref)

    @pl.when(ph == 1)
    def _():
        sr = lax.broadcasted_iota(jnp.int32, (T, T), 0)
        sc = lax.broadcasted_iota(jnp.int32, (T, T), 1)
        earlier = jnp.dot(onehot.astype(bf16), jnp.where(sr < sc, 1.0, 0.0).astype(bf16),
                          preferred_element_type=f32)
        base = pst_ref[:, 0:1] + run_ref[:, 0:1]
        dest = jnp.sum(onehot * (earlier + base), axis=0, keepdims=True)
        dest_ref[...] = dest.astype(jnp.int32)
        run_ref[...] = run_ref[...] + here


def _moe_dest(eid_row, n_exp, blk):
    n_slots = eid_row.shape[1]
    T = min(MOE_DEST_CHUNK, n_slots)
    assert n_slots % T == 0 and n_slots + n_exp * blk < 2 ** 24
    dest, cnt = pl.pallas_call(
        functools.partial(_moe_dest_body, n_exp, blk),
        grid=(2, n_slots // T),
        in_specs=[pl.BlockSpec((1, T), lambda p, c: (0, c))],
        out_specs=[pl.BlockSpec((1, T), lambda p, c: (0, c * p)),
                   pl.BlockSpec((n_exp, 128), lambda p, c: (0, 0))],
        out_shape=[jax.ShapeDtypeStruct((1, n_slots), jnp.int32),
                   jax.ShapeDtypeStruct((n_exp, 128), jnp.float32)],
        scratch_shapes=[pltpu.VMEM((n_exp, 128), jnp.float32), pltpu.VMEM((n_exp, 128), jnp.float32)],
        compiler_params=pltpu.CompilerParams(dimension_semantics=("arbitrary", "arbitrary")),
        name="moe_dest",
    )(eid_row)
    return dest, cnt[:, 0].astype(jnp.int32)


MOE_TOKEN_TILE = 256


def _row_copy(src_ref, src_row, dst_ref, dst_row, sem):
    return pltpu.make_async_copy(src_ref.at[pl.ds(src_row, 1)], dst_ref.at[pl.ds(dst_row, 1)], sem)


def _moe_dispatch_body(dest_ref, x_ref, init_ref, xs_ref, sem):
    del init_ref
    n_choice, T = dest_ref.shape

    def start(r, carry):
        for k in range(n_choice):
            _row_copy(x_ref, r, xs_ref, dest_ref[k, r], sem).start(priority=k % 2)
        return carry

    def wait(r, carry):
        for k in range(n_choice):
            _row_copy(x_ref, 0, xs_ref, 0, sem).wait()
        return carry

    lax.fori_loop(0, T, start, 0, unroll=8)
    lax.fori_loop(0, T, wait, 0)


def _moe_dispatch(x, dest2, n_rows):
    N, D = x.shape
    T = min(MOE_TOKEN_TILE, N)
    assert N % T == 0
    return pl.pallas_call(
        _moe_dispatch_body,
        grid=(N // T,),
        in_specs=[pl.BlockSpec((dest2.shape[0], T), lambda i: (0, i), memory_space=pltpu.SMEM),
                  pl.BlockSpec((T, D), lambda i: (i, 0)),
                  pl.BlockSpec(memory_space=pl.ANY)],
        out_specs=pl.BlockSpec(memory_space=pl.ANY),
        out_shape=jax.ShapeDtypeStruct((n_rows, D), x.dtype),
        scratch_shapes=[pltpu.SemaphoreType.DMA(())],
        input_output_aliases={2: 0},
        compiler_params=pltpu.CompilerParams(dimension_semantics=("arbitrary",)),
        name="moe_dispatch",
    )(dest2, x, jnp.zeros((n_rows, D), x.dtype))


def _moe_combine_body(dest_ref, ys_ref, w_ref, x_ref, g_ref, b_ref, o_ref, buf_ref, sem):
    n_choice, T = dest_ref.shape

    def start(r, carry):
        for k in range(n_choice):
            _row_copy(ys_ref, dest_ref[k, r], buf_ref.at[k], r, sem).start(priority=k % 2)
        return carry

    def wait(r, carry):
        for k in range(n_choice):
            _row_copy(ys_ref, 0, buf_ref.at[k], 0, sem).wait()
        return carry

    lax.fori_loop(0, T, start, 0, unroll=8)
    lax.fori_loop(0, T, wait, 0)
    w = w_ref[...]
    h = w[:, 0:1] * buf_ref[0]
    for k in range(1, n_choice):
        h = h + w[:, k:k + 1] * buf_ref[k]
    y = ALPHA * x_ref[...] + h
    mu = jnp.mean(y, axis=-1, keepdims=True)
    var = jnp.mean(jnp.square(y - mu), axis=-1, keepdims=True)
    o_ref[...] = (y - mu) * lax.rsqrt(var + LN_EPS) * g_ref[...] + b_ref[...]


def _moe_combine(ys, dest2, w, x, ln_g, ln_b):
    N, D = x.shape
    n_choice = dest2.shape[0]
    T = min(MOE_TOKEN_TILE, N)
    assert N % T == 0
    ln_g, ln_b = ln_g.reshape(1, D), ln_b.reshape(1, D)
    return pl.pallas_call(
        _moe_combine_body,
        grid=(N // T,),
        in_specs=[pl.BlockSpec((n_choice, T), lambda i: (0, i), memory_space=pltpu.SMEM),
                  pl.BlockSpec(memory_space=pl.ANY),
                  pl.BlockSpec((T, n_choice), lambda i: (i, 0)),
                  pl.BlockSpec((T, D), lambda i: (i, 0)),
                  pl.BlockSpec((1, D), lambda i: (0, 0)),
                  pl.BlockSpec((1, D), lambda i: (0, 0))],
        out_specs=pl.BlockSpec((T, D), lambda i: (i, 0)),
        out_shape=jax.ShapeDtypeStruct((N, D), jnp.float32),
        scratch_shapes=[pltpu.VMEM((n_choice, T, D), jnp.float32), pltpu.SemaphoreType.DMA(())],
        compiler_params=pltpu.CompilerParams(dimension_semantics=("arbitrary",)),
        name="moe_combine",
    )(dest2, ys, w, x, ln_g, ln_b)


def _split_cols(z, sizes):
    return jnp.split(z, np.cumsum(sizes)[:-1].tolist(), axis=-1)


def _layer_norm(x, g, b, eps=LN_EPS):
    mu = jnp.mean(x, axis=-1, keepdims=True)
    var = jnp.mean(jnp.square(x - mu), axis=-1, keepdims=True)
    return (x - mu) * lax.rsqrt(var + eps) * g + b


def _t5_bucket(dist):
    n = jnp.maximum(dist, 0)
    max_exact = NUM_BUCKETS // 2
    nf = jnp.maximum(n, 1).astype(jnp.float32)
    large = max_exact + (jnp.log(nf / max_exact) / math.log(MAX_DISTANCE / max_exact)
                         * (NUM_BUCKETS - max_exact)).astype(jnp.int32)
    large = jnp.minimum(large, NUM_BUCKETS - 1)
    return jnp.where(n < max_exact, n, large)


def _bias_from_buckets(rel_bias, bucket):
    rb = rel_bias.astype(jnp.float32)
    shape = (N_KV_GROUPS, HEADS_PER_GROUP) + (1,) * bucket.ndim
    out = jnp.zeros((N_KV_GROUPS, HEADS_PER_GROUP) + bucket.shape, jnp.float32)
    for b in range(NUM_BUCKETS):
        out = jnp.where(bucket == b, rb[b].reshape(shape), out)
    return out


def _nsa_positional(rel_bias, seq):
    n_cmp = seq // CMP_STRIDE - CMP_BLOCK // CMP_STRIDE + 1
    n_sel_blocks = seq // SEL_BLOCK
    t = jnp.arange(seq)[:, None]
    c = jnp.arange(n_cmp)[None, :]
    d_cmp = t - (c * CMP_STRIDE + CMP_BLOCK - 1)
    mask_cmp = d_cmp >= 0
    bias_cmp = _bias_from_buckets(rel_bias, _t5_bucket(d_cmp))
    qo = jnp.arange(Q_BLOCK)[:, None]
    m = jnp.arange(WINDOW + Q_BLOCK)[None, :]
    d_win = qo + WINDOW - m
    bias_win = _bias_from_buckets(rel_bias, _t5_bucket(d_win))
    blk = jnp.arange(seq // Q_BLOCK)[:, None, None]
    mask_win = (d_win >= 0) & (d_win < WINDOW) & (blk * Q_BLOCK - WINDOW + m >= 0)
    cs = jnp.arange(n_cmp)[:, None] * CMP_STRIDE
    ss = jnp.arange(n_sel_blocks)[None, :] * SEL_BLOCK
    overlap = jnp.clip(jnp.minimum(cs + CMP_BLOCK, ss + SEL_BLOCK) - jnp.maximum(cs, ss), 0, None)
    cmp_to_sel = overlap.astype(jnp.float32) / CMP_BLOCK
    return bias_cmp, mask_cmp, bias_win, mask_win, cmp_to_sel, _sel_bias_tiles(rel_bias, seq)


def _rwkv7_time_mix(r, k, v, wl, al, gl, w0, w2, a0, a2, g2, k_k, k_a, r_k, ln_g, ln_b):
    B, S, C = r.shape
    H, N = N_HEADS_RWKV, HEAD_RWKV
    f32 = jnp.float32
    logw = -jax.nn.softplus(-(w0 + jnp.tanh(wl) @ w2).astype(f32)) - 0.5
    log_decay = -jnp.exp(logw)
    a = jax.nn.sigmoid((a0 + al @ a2).astype(f32))
    g = jax.nn.sigmoid(gl) @ g2
    heads = lambda t: t.reshape(B, S, H, N)
    kk = heads(k * k_k)
    kk = kk / jnp.maximum(jnp.linalg.norm(kk, axis=-1, keepdims=True), 1e-12)
    k = k * (1.0 + (a - 1.0) * k_a)
    rh, kh, vh = heads(r), heads(k), heads(v)
    y = heads(_rwkv_scan(r, log_decay, k, v, kk.reshape(B, S, C), a, H))
    mu = jnp.mean(y, axis=-1, keepdims=True)
    var = jnp.mean(jnp.square(y - mu), axis=-1, keepdims=True)
    y = ((y - mu) * lax.rsqrt(var + RWKV_GN_EPS)).reshape(B, S, C) * ln_g + ln_b
    bonus = jnp.sum(rh * kh * r_k, axis=-1, keepdims=True) * vh
    return (y + bonus.reshape(B, S, C)) * g


def _compress(t, pe, w1, w2):
    B, S, G, Dh = t.shape
    rep = CMP_BLOCK // CMP_STRIDE
    nc = S // CMP_STRIDE - rep + 1
    sub = t.reshape(B, S // CMP_STRIDE, CMP_STRIDE, G, Dh)
    blk = jnp.concatenate([sub[:, j:j + nc] for j in range(rep)], axis=2)
    blk = blk + pe[:, None, :]
    blk = blk.transpose(0, 1, 3, 2, 4).reshape(B, nc, G, CMP_BLOCK * Dh)
    out = jax.nn.gelu(blk @ w1) @ w2
    return out.transpose(0, 2, 1, 3)


def _nsa_attention(zt, gates_t, k_cmp, v_cmp, pe_k, w1_k, w2_k, pe_v, w1_v, w2_v, tables):
    bias_cmp_t, sel_tiles, win_tiles, c2s_t, onehot_t, n_cmp = tables
    B, S, _ = k_cmp.shape
    G, Dh = N_KV_GROUPS, HEAD_NSA
    bf16 = jnp.bfloat16
    ncp = bias_cmp_t.shape[2]
    kc = _compress(k_cmp.reshape(B, S, G, Dh), pe_k, w1_k, w2_k)
    vc = _compress(v_cmp.reshape(B, S, G, Dh), pe_v, w1_v, w2_v)
    pad = ((0, 0), (0, 0), (0, ncp - n_cmp), (0, 0))
    kc_p = jnp.pad(kc, pad).astype(bf16)
    vc_t = jnp.pad(vc, pad).transpose(0, 1, 3, 2).astype(bf16)
    n_sel = min(N_SELECT, S // SEL_BLOCK)
    o_t = _nsa_fused(zt, gates_t, kc_p, vc_t, bias_cmp_t, sel_tiles, win_tiles, c2s_t, onehot_t,
                     n_sel, n_cmp)
    return o_t.reshape(B, NSA_Q_WIDTH, S)


def _split_in_proj(w_in):
    D = w_in.shape[0]
    scale = HEAD_NSA ** -0.5
    assert math.frexp(scale)[0] == 0.5
    o = np.cumsum((SHIFT_WIDTH,) + REST_SPLITS)
    col = lambda a, b: w_in[:, o[a]:o[b]]
    row_part = [w_in[:, :SHIFT_WIDTH], col(1, 3)]
    lead = SHIFT_WIDTH + int(o[3] - o[1])
    lead_pad = (-lead) % D_MODEL
    w_row = jnp.concatenate(row_part + [jnp.zeros((D, lead_pad), w_in.dtype), col(8, 10)], axis=1)
    gate_blk = (lead + lead_pad) // D_MODEL
    assert w_row.shape[1] % IN_PROJ_TN == 0
    G, n_g = N_KV_GROUPS, 3 * HEADS_PER_GROUP
    gates = col(7, 8).reshape(D, G, n_g)
    gates = jnp.pad(gates, ((0, 0), (0, 0), (0, NSA_GATE_ROWS - n_g))).reshape(D, G * NSA_GATE_ROWS)
    w_t = jnp.concatenate([col(0, 1) * scale, col(3, 7), gates], axis=1).T
    return w_row.astype(jnp.bfloat16), w_t.astype(jnp.bfloat16), gate_blk, int(o[7] - o[3]) + NSA_Q_WIDTH


def _token_mixer(x, w_in, shift_mu, rw_w0, rw_w2, rw_a0, rw_a2, rw_g2, rw_kk, rw_ka, rw_rk,
                 rw_ln_g, rw_ln_b, cmp_pe_k, cmp_w1_k, cmp_w2_k, cmp_pe_v, cmp_w1_v, cmp_w2_v,
                 w_up_rwkv, w_up_nsa, w_out, ln_g, ln_b, rel_bias, pos):
    B, S, D = x.shape
    xf = x.reshape(B * S, D)
    w_row, w_t, gate_blk, n_main = _split_in_proj(w_in)
    z = _matmul(xf, w_row, IN_PROJ_TM, IN_PROJ_TN)
    zt, gates_t = _matmul_t(xf, w_t, n_main, B, S)
    z3 = z.reshape(B, S, -1)
    k_cmp = z3[..., SHIFT_WIDTH:SHIFT_WIDTH + NSA_KV_WIDTH]
    v_cmp = z3[..., SHIFT_WIDTH + NSA_KV_WIDTH:SHIFT_WIDTH + 2 * NSA_KV_WIDTH]
    y_rw = _rwkv_fused(z, B, S, shift_mu, rw_w0, rw_w2, rw_a0, rw_a2, rw_g2,
                       rw_kk, rw_ka, rw_rk, rw_ln_g, rw_ln_b)
    y_nsa_t = _nsa_attention(zt, gates_t, k_cmp, v_cmp, cmp_pe_k, cmp_w1_k, cmp_w2_k,
                             cmp_pe_v, cmp_w1_v, cmp_w2_v, pos)
    bf16 = jnp.bfloat16
    return _merge_out(y_rw, y_nsa_t, z, gate_blk, xf,
                      w_up_rwkv.astype(bf16), w_up_nsa.astype(bf16), w_out.astype(bf16),
                      ln_g, ln_b).reshape(B, S, D)


def _hier_moe(x, wg, bg, we, be, w1, w3, w2, layer, ln_g, ln_b):
    B, S, D = x.shape
    N = B * S
    f32 = jnp.float32
    xf = x.reshape(N, D)
    g_prob = jax.nn.softmax((xf @ wg + bg).astype(f32), axis=-1)
    grp = jnp.argmax(g_prob, axis=-1)
    p_grp = jnp.take_along_axis(g_prob, grp[:, None], axis=1)[:, 0]
    e_logits = (xf @ we + be).astype(f32).reshape(N, N_GROUPS, EXPERTS_PER_GROUP)
    e_logits = jnp.take_along_axis(e_logits, grp[:, None, None], axis=1)[:, 0]
    top_v, top_i = lax.top_k(e_logits, TOP_K_INNER)
    top_w = jax.nn.softmax(top_v, axis=-1) * p_grp[:, None]
    eid = (grp[:, None] * EXPERTS_PER_GROUP + top_i).astype(jnp.int32)
    n_slots = N * TOP_K_INNER
    n_rows = n_slots + N_EXPERTS * MOE_BLOCK
    n_blk = n_rows // MOE_BLOCK
    dest, counts = _moe_dest(eid.T.reshape(1, n_slots), N_EXPERTS, MOE_BLOCK)
    dest2 = dest.reshape(TOP_K_INNER, N)
    pcounts = (counts + MOE_BLOCK - 1) // MOE_BLOCK * MOE_BLOCK
    pends = jnp.cumsum(pcounts)
    blk_exp = jnp.minimum(jnp.sum(jnp.arange(n_blk)[:, None] * MOE_BLOCK >= pends[None, :], axis=1),
                          N_EXPERTS - 1).astype(jnp.int32)
    xs = _moe_dispatch(xf, dest2, n_rows)
    n_used = (pends[-1:] // MOE_BLOCK).astype(jnp.int32)
    ys = _moe_experts(xs, blk_exp, n_used, w1, w3, w2, layer, MOE_BLOCK)
    return _moe_combine(ys, dest2, top_w, xf, ln_g, ln_b).reshape(B, S, D)


def kernel(x, rel_bias, w_in, shift_mu, rw_w0, rw_w2, rw_a0, rw_a2, rw_g2, rw_kk, rw_ka, rw_rk,
           rw_ln_g, rw_ln_b, cmp_pe_k, cmp_w1_k, cmp_w2_k, cmp_pe_v, cmp_w1_v, cmp_w2_v,
           w_up_rwkv, w_up_nsa, w_out, ln1_g, ln1_b, router_group_w, router_group_b,
           router_expert_w, router_expert_b, exp_w1, exp_w3, exp_w2, ln2_g, ln2_b):
    pos = _nsa_tables(rel_bias, x.shape[1])
    for l in range(DEPTH):
        x = _token_mixer(x, w_in[l], shift_mu[l], rw_w0[l], rw_w2[l], rw_a0[l], rw_a2[l], rw_g2[l],
                         rw_kk[l], rw_ka[l], rw_rk[l], rw_ln_g[l], rw_ln_b[l],
                         cmp_pe_k[l], cmp_w1_k[l], cmp_w2_k[l], cmp_pe_v[l], cmp_w1_v[l], cmp_w2_v[l],
                         w_up_rwkv[l], w_up_nsa[l], w_out[l], ln1_g[l], ln1_b[l], rel_bias, pos)
        x = _hier_moe(x, router_group_w[l], router_group_b[l], router_expert_w[l], router_expert_b[l],
                      exp_w1, exp_w3, exp_w2, l, ln2_g[l], ln2_b[l])
    return x
```

```python
import functools
import math

import jax
import jax.numpy as jnp
import numpy as np
from jax import lax
from jax.experimental import pallas as pl
from jax.experimental.pallas import tpu as pltpu

D_MODEL = 1024
DEPTH = 4
N_HEADS_RWKV = 8
HEAD_RWKV = 64
RWKV_WIDTH = N_HEADS_RWKV * HEAD_RWKV
LORA_W = 64
LORA_A = 64
LORA_G = 128
RWKV_GN_EPS = 64e-5
N_HEADS_NSA = 8
N_KV_GROUPS = 2
HEADS_PER_GROUP = N_HEADS_NSA // N_KV_GROUPS
HEAD_NSA = 64
NSA_Q_WIDTH = N_HEADS_NSA * HEAD_NSA
NSA_KV_WIDTH = N_KV_GROUPS * HEAD_NSA
CMP_BLOCK = 32
CMP_STRIDE = 16
CMP_HIDDEN = 128
SEL_BLOCK = 64
N_SELECT = 16
WINDOW = 512
Q_BLOCK = 128
N_BAND = WINDOW // Q_BLOCK + 1
NEG_INF = -1e30
FORCED_SCORE = 1e4
NUM_BUCKETS = 32
MAX_DISTANCE = 1024
N_GROUPS = 4
EXPERTS_PER_GROUP = 8
N_EXPERTS = N_GROUPS * EXPERTS_PER_GROUP
TOP_K_INNER = 2
D_EXPERT = 512
MOE_BLOCK = 256
ALPHA = (2 * DEPTH) ** 0.25
LN_EPS = 1e-5
SHIFT_SPLITS = (RWKV_WIDTH, RWKV_WIDTH, RWKV_WIDTH, LORA_W, LORA_A, LORA_G)
SHIFT_WIDTH = 3 * RWKV_WIDTH + LORA_W + LORA_A + LORA_G
REST_SPLITS = (NSA_Q_WIDTH,) + (NSA_KV_WIDTH,) * 6 + (3 * N_HEADS_NSA, D_MODEL, D_MODEL)

VMEM_LIMIT = 48 * 1024 * 1024


IN_PROJ_TM = 1024
IN_PROJ_TN = 1024


def _mm_body(x_ref, w_ref, o_ref, xb_ref):
    @pl.when(pl.program_id(1) == 0)
    def _():
        xb_ref[...] = x_ref[...].astype(jnp.bfloat16)

    o_ref[...] = jnp.dot(xb_ref[...], w_ref[...], preferred_element_type=jnp.float32)


def _matmul(x, w, tm, tn):
    m, k = x.shape
    n = w.shape[1]
    assert m % tm == 0 and n % tn == 0
    return pl.pallas_call(
        _mm_body,
        grid=(m // tm, n // tn),
        in_specs=[pl.BlockSpec((tm, k), lambda i, j: (i, 0)),
                  pl.BlockSpec((k, tn), lambda i, j: (0, j))],
        out_specs=pl.BlockSpec((tm, tn), lambda i, j: (i, j)),
        out_shape=jax.ShapeDtypeStruct((m, n), jnp.float32),
        scratch_shapes=[pltpu.VMEM((tm, k), jnp.bfloat16)],
        compiler_params=pltpu.CompilerParams(dimension_semantics=("arbitrary", "arbitrary"),
                                             vmem_limit_bytes=VMEM_LIMIT),
        name="in_proj",
    )(x, w)


IN_PROJ_T_TM = 512
NSA_GATE_ROWS = 16


def _mm_t_body(n_main, x_ref, wt_ref, zt_ref, gt_ref):
    xb = x_ref[...].astype(jnp.bfloat16)
    out = lax.dot_general(wt_ref[...], xb, (((1,), (1,)), ((), ())), preferred_element_type=jnp.float32)
    zt_ref[0] = out[:n_main].astype(zt_ref.dtype)
    gt_ref[0] = out[n_main:]


def _matmul_t(x, wt, n_main, batch, seq):
    m, k = x.shape
    r = wt.shape[0]
    tm = min(IN_PROJ_T_TM, seq)
    assert seq % tm == 0 and m == batch * seq
    per_b = seq // tm
    return pl.pallas_call(
        functools.partial(_mm_t_body, n_main),
        grid=(m // tm,),
        in_specs=[pl.BlockSpec((tm, k), lambda i: (i, 0)),
                  pl.BlockSpec((r, k), lambda i: (0, 0))],
        out_specs=[pl.BlockSpec((1, n_main, tm), lambda i: (i // per_b, 0, i % per_b)),
                   pl.BlockSpec((1, r - n_main, tm), lambda i: (i // per_b, 0, i % per_b))],
        out_shape=[jax.ShapeDtypeStruct((batch, n_main, seq), jnp.bfloat16),
                   jax.ShapeDtypeStruct((batch, r - n_main, seq), jnp.float32)],
        compiler_params=pltpu.CompilerParams(dimension_semantics=("arbitrary",),
                                             vmem_limit_bytes=VMEM_LIMIT),
        name="in_proj_t",
    )(x, wt)


MERGE_TM = 256


def _merge_body(yr_ref, yn_ref, grw_ref, gns_ref, x_ref, wur_ref, wun_ref, wo_ref, g_ref, b_ref, o_ref):
    f32, bf16 = jnp.float32, jnp.bfloat16
    up_r = jnp.dot(yr_ref[...].astype(bf16), wur_ref[...], preferred_element_type=f32)
    up_n = lax.dot_general(yn_ref[0].astype(bf16), wun_ref[...], _TN, preferred_element_type=f32)
    merged = jax.nn.sigmoid(grw_ref[...]) * up_r + jax.nn.sigmoid(gns_ref[...]) * up_n
    y = ALPHA * x_ref[...] + jnp.dot(merged.astype(bf16), wo_ref[...], preferred_element_type=f32)
    mu = jnp.mean(y, axis=-1, keepdims=True)
    var = jnp.mean(jnp.square(y - mu), axis=-1, keepdims=True)
    o_ref[...] = (y - mu) * lax.rsqrt(var + LN_EPS) * g_ref[...] + b_ref[...]


def _merge_out(y_rw, y_nsa_t, z, gate_blk, x, w_up_rwkv, w_up_nsa, w_out, ln_g, ln_b):
    m, d = x.shape
    tm = MERGE_TM
    nb, wn, seq = y_nsa_t.shape
    assert m % tm == 0 and seq % tm == 0
    per_b = seq // tm
    row = lambda width: pl.BlockSpec((tm, width), lambda i: (i, 0))
    full = lambda a: pl.BlockSpec(a.shape, lambda i: (0,) * a.ndim)
    ln_g, ln_b = ln_g.reshape(1, d), ln_b.reshape(1, d)
    return pl.pallas_call(
        _merge_body,
        grid=(m // tm,),
        in_specs=[row(y_rw.shape[1]), pl.BlockSpec((1, wn, tm), lambda i: (i // per_b, 0, i % per_b)),
                  pl.BlockSpec((tm, d), lambda i: (i, gate_blk)),
                  pl.BlockSpec((tm, d), lambda i: (i, gate_blk + 1)),
                  row(d), full(w_up_rwkv), full(w_up_nsa), full(w_out), full(ln_g), full(ln_b)],
        out_specs=row(d),
        out_shape=jax.ShapeDtypeStruct((m, d), jnp.float32),
        compiler_params=pltpu.CompilerParams(dimension_semantics=("arbitrary",),
                                             vmem_limit_bytes=VMEM_LIMIT),
        name="merge_out",
    )(y_rw, y_nsa_t, z, z, x, w_up_rwkv, w_up_nsa, w_out, ln_g, ln_b)


def _topk_mask_body(n_sel, s_ref, o_ref):
    s = s_ref[0, 0]
    ns = s.shape[0]
    jidx = lax.broadcasted_iota(jnp.int32, s.shape, 0)
    cnt = jnp.zeros(s.shape, jnp.float32)
    for jp in range(ns):
        row = s[jp:jp + 1, :]
        tie = jnp.where(jidx > jp, 1.0, 0.0)
        cnt = cnt + jnp.where(row > s, 1.0, jnp.where(row == s, tie, 0.0))
    o_ref[0, 0] = jnp.where(cnt < n_sel, 1.0, 0.0)


def _topk_mask(scores_t, n_sel, tq=512):
    B, G, NS, S = scores_t.shape
    tq = min(tq, S)
    return pl.pallas_call(
        functools.partial(_topk_mask_body, float(n_sel)),
        grid=(B, G, S // tq),
        in_specs=[pl.BlockSpec((1, 1, NS, tq), lambda b, g, i: (b, g, 0, i))],
        out_specs=pl.BlockSpec((1, 1, NS, tq), lambda b, g, i: (b, g, 0, i)),
        out_shape=jax.ShapeDtypeStruct((B, G, NS, S), jnp.float32),
        compiler_params=pltpu.CompilerParams(dimension_semantics=("arbitrary",) * 3),
        name="topk_mask",
    )(scores_t)


def _sel_attn_body(n_delta, q_ref, k_ref, v_ref, m_ref, t_ref, o_ref, acc_ref):
    i = pl.program_id(2)
    hg, dh, qb = q_ref.shape[2], q_ref.shape[3], q_ref.shape[4]
    ns = m_ref.shape[2]
    maskf = m_ref[0, 0].astype(jnp.bfloat16)
    key = lax.broadcasted_iota(jnp.int32, (qb, ns), 0)
    blk = lax.broadcasted_iota(jnp.int32, (qb, ns), 1)
    per = qb // SEL_BLOCK
    acc_ref[...] = jnp.zeros_like(acc_ref)

    def body(j, carry):
        ms, ls = carry
        off = pl.multiple_of(j * qb, qb)
        k = k_ref[0, 0, pl.ds(off, qb), :]
        v = v_ref[0, 0, :, pl.ds(off, qb)]
        expand = jnp.where(blk == j * per + key // SEL_BLOCK, 1.0, 0.0).astype(jnp.bfloat16)
        selm = jnp.dot(expand, maskf, preferred_element_type=jnp.float32) > 0.5
        d = jnp.minimum(i - j, n_delta)
        new_ms, new_ls = [], []
        for h in range(hg):
            lg = jnp.dot(k, q_ref[0, 0, h], preferred_element_type=jnp.float32) + t_ref[0, h, d]
            lg = jnp.where(selm, lg, NEG_INF)
            m_new = jnp.maximum(ms[h], jnp.max(lg, axis=0, keepdims=True))
            p = jnp.exp(lg - m_new)
            alpha = jnp.exp(ms[h] - m_new)
            new_ls.append(alpha * ls[h] + jnp.sum(p, axis=0, keepdims=True))
            acc_ref[h] = alpha * acc_ref[h] + jnp.dot(v, p.astype(jnp.bfloat16),
                                                      preferred_element_type=jnp.float32)
            new_ms.append(m_new)
        return tuple(new_ms), tuple(new_ls)

    init = (tuple(jnp.full((1, qb), NEG_INF, jnp.float32) for _ in range(hg)),
            tuple(jnp.zeros((1, qb), jnp.float32) for _ in range(hg)))
    ms, ls = lax.fori_loop(0, i + 1, body, init)
    for h in range(hg):
        o_ref[0, 0, h] = acc_ref[h] / ls[h]


def _sel_attention(q_t, ks, vs_t, mask_t, bias_tiles):
    B, G, Hg, Dh, S = q_t.shape
    NS = mask_t.shape[2]
    ND = bias_tiles.shape[2] - 1
    QB = bias_tiles.shape[-1]
    return pl.pallas_call(
        functools.partial(_sel_attn_body, ND),
        grid=(B, G, S // QB),
        in_specs=[pl.BlockSpec((1, 1, Hg, Dh, QB), lambda b, g, i: (b, g, 0, 0, i)),
                  pl.BlockSpec((1, 1, S, Dh), lambda b, g, i: (b, g, 0, 0)),
                  pl.BlockSpec((1, 1, Dh, S), lambda b, g, i: (b, g, 0, 0)),
                  pl.BlockSpec((1, 1, NS, QB), lambda b, g, i: (b, g, 0, i)),
                  pl.BlockSpec((1, Hg, ND + 1, QB, QB), lambda b, g, i: (g, 0, 0, 0, 0))],
        out_specs=pl.BlockSpec((1, 1, Hg, Dh, QB), lambda b, g, i: (b, g, 0, 0, i)),
        out_shape=jax.ShapeDtypeStruct((B, G, Hg, Dh, S), jnp.float32),
        scratch_shapes=[pltpu.VMEM((Hg, Dh, QB), jnp.float32)],
        compiler_params=pltpu.CompilerParams(dimension_semantics=("arbitrary",) * 3,
                                             vmem_limit_bytes=VMEM_LIMIT),
        name="sel_attention",
    )(q_t, ks, vs_t, mask_t, bias_tiles)


def _sel_bias_tiles(rel_bias, seq):
    nd = -(-MAX_DISTANCE // Q_BLOCK)
    nd = min(nd, seq // Q_BLOCK)
    half = NUM_BUCKETS // 2
    min_far = nd * Q_BLOCK - (Q_BLOCK - 1)
    assert half + math.log(min_far / half) / math.log(MAX_DISTANCE / half) * half >= NUM_BUCKETS - 0.75
    kj = np.arange(Q_BLOCK)[:, None]
    qi = np.arange(Q_BLOCK)[None, :]
    dist = np.arange(nd + 1)[:, None, None] * Q_BLOCK + qi - kj
    dist[nd] = max(seq - 1, nd * Q_BLOCK)
    tiles = _bias_from_buckets(rel_bias, _t5_bucket(jnp.asarray(np.maximum(dist, 0))))
    causal = jnp.asarray(dist >= 0)
    return jnp.where(causal, tiles, NEG_INF)


SEL_KEY_TILE = 256


def _flash_update(lgs, v_ts, ms, ls, acc_ref):
    H = range(len(lgs))
    m_new = []
    for h in H:
        m = ms[h]
        for lg in lgs[h]:
            m = jnp.maximum(m, jnp.max(lg, axis=0, keepdims=True))
        m_new.append(m)
    ps = [[jnp.exp(lg - m_new[h]) for lg in lgs[h]] for h in H]
    alpha = [jnp.exp(ms[h] - m_new[h]) for h in H]
    l_new = []
    for h in H:
        l = alpha[h] * ls[h]
        for p in ps[h]:
            l = l + jnp.sum(p, axis=0, keepdims=True)
        l_new.append(l)
    pv = [[jnp.dot(v_t, p.astype(jnp.bfloat16), preferred_element_type=jnp.float32)
           for p, v_t in zip(ps[h], v_ts[h])] for h in H]
    for h in H:
        acc = alpha[h] * acc_ref[h]
        for o in pv[h]:
            acc = acc + o
        acc_ref[h] = acc
    return tuple(m_new), tuple(l_new)


def _nsa_body(n_sel, n_cmp, n_delta, q_ref, kc_ref, vc_ref, ks_ref, vs_ref, kw_ref, vw_ref, gl_ref,
              bc_ref, ts_ref, tw_ref, c2s_ref, oh_ref, o_ref, acc_s, acc_w, ocmp_ref):
    i = pl.program_id(1)
    hg = HEADS_PER_GROUP
    ng = q_ref.shape[1] // hg
    dh, qb = q_ref.shape[2], q_ref.shape[3]
    ncp = kc_ref.shape[2]
    ns = c2s_ref.shape[0]
    f32, bf16 = jnp.float32, jnp.bfloat16
    H = range(ng * hg)
    G = range(ng)

    cidx = lax.broadcasted_iota(jnp.int32, (ncp, qb), 0)
    tpos = lax.broadcasted_iota(jnp.int32, (ncp, qb), 1) + i * qb
    valid = (tpos - (cidx * CMP_STRIDE + CMP_BLOCK - 1)) >= 0
    fill = jnp.where(cidx < n_cmp, NEG_INF, 2.0 * NEG_INF)
    qkc = [jnp.dot(kc_ref[0, h // hg], q_ref[0, h], preferred_element_type=f32) for h in H]
    lgc = [jnp.where(valid, qkc[h] + bc_ref[h // hg, h % hg], fill) for h in H]
    mc = [jnp.max(lgc[h], axis=0, keepdims=True) for h in H]
    pc = [jnp.exp(lgc[h] - mc[h]) for h in H]
    inv = [1.0 / jnp.sum(pc[h], axis=0, keepdims=True) for h in H]
    pc = [jnp.where(valid, pc[h] * inv[h], 0.0) for h in H]
    psum = [pc[g * hg] for g in G]
    for h in H:
        if h % hg:
            psum[h // hg] = psum[h // hg] + pc[h]
    for h in H:
        ocmp_ref[h] = jnp.dot(vc_ref[0, h // hg], pc[h].astype(bf16), preferred_element_type=f32)

    jblk = lax.broadcasted_iota(jnp.int32, (ns, qb), 0)
    cur = (lax.broadcasted_iota(jnp.int32, (ns, qb), 1) + i * qb) // SEL_BLOCK
    forced = (jblk == 0) | (jblk == cur) | (jblk == cur - 1)
    score = [jnp.dot(c2s_ref[...], psum[g].astype(bf16), preferred_element_type=f32) for g in G]
    score = [jnp.where(forced, FORCED_SCORE, jnp.where(jblk <= cur, score[g], -1.0)) for g in G]
    cnt = [jnp.zeros((ns, qb), f32) for g in G]
    for jp in range(ns):
        tie = jnp.where(jblk > jp, 1.0, 0.0)
        for g in G:
            row = score[g][jp:jp + 1, :]
            cnt[g] = cnt[g] + jnp.where(row > score[g], 1.0, jnp.where(row == score[g], tie, 0.0))
    maskneg = [jnp.where(cnt[g] < n_sel, 0.0, NEG_INF).astype(bf16) for g in G]
    q_aug = [jnp.concatenate([q_ref[0, h], maskneg[h // hg]], axis=0) for h in H]

    kt = SEL_KEY_TILE
    per = kt // qb
    last_tile = i // per
    acc_s[...] = jnp.zeros_like(acc_s)

    def sel_step(jj, carry):
        ms, ls = carry
        ks_, vs_, tiles = [], [], []
        for u in range(2):
            j = 2 * jj + u
            off = pl.multiple_of(jnp.minimum(j, last_tile) * kt, kt)
            oh = oh_ref[:, pl.ds(off, kt)]
            ks_.append([jnp.concatenate([ks_ref[0, g, :, pl.ds(off, kt)], oh], axis=0) for g in G])
            vs_.append([vs_ref[0, g, :, pl.ds(off, kt)] for g in G])
            tiles.append([jnp.clip(i - (j * per + w), -1, n_delta) + 1 for w in range(per)])
        qk = [[lax.dot_general(ks_[u][h // hg], q_aug[h], _TN, preferred_element_type=f32) for u in range(2)]
              for h in H]
        lgs = [[qk[h][u] + jnp.concatenate([ts_ref[h // hg, h % hg, tiles[u][w]] for w in range(per)], axis=0)
                for u in range(2)] for h in H]
        return _flash_update(lgs, [[vs_[u][h // hg] for u in range(2)] for h in H], ms, ls, acc_s)

    init = (tuple(jnp.full((1, qb), NEG_INF, f32) for _ in H), tuple(jnp.zeros((1, qb), f32) for _ in H))
    _, ls_s = lax.fori_loop(0, last_tile // 2 + 1, sel_step, init)

    acc_w[...] = jnp.zeros_like(acc_w)
    kws, vws, widx = [], [], []
    for u in range(N_BAND):
        j = i - (N_BAND - 1) + u
        off = pl.multiple_of(jnp.maximum(j, 0) * qb, qb)
        kws.append([kw_ref[0, g, :, pl.ds(off, qb)] for g in G])
        vws.append([vw_ref[0, g, :, pl.ds(off, qb)] for g in G])
        widx.append(jnp.where(j >= 0, N_BAND - 1 - u, N_BAND))
    qk = [[lax.dot_general(kws[u][h // hg], q_ref[0, h], _TN, preferred_element_type=f32) for u in range(N_BAND)]
          for h in H]
    lgs = [[qk[h][u] + tw_ref[h // hg, h % hg, widx[u]] for u in range(N_BAND)] for h in H]
    _, ls_w = _flash_update(lgs, [[vws[u][h // hg] for u in range(N_BAND)] for h in H], init[0], init[1], acc_w)

    gates = jax.nn.sigmoid(gl_ref[0])
    for h in H:
        r0 = (h // hg) * NSA_GATE_ROWS + 3 * (h % hg)
        o_ref[0, h] = (gates[r0:r0 + 1] * ocmp_ref[h]
                       + gates[r0 + 1:r0 + 2] * (acc_s[h] / ls_s[h])
                       + gates[r0 + 2:r0 + 3] * (acc_w[h] / ls_w[h]))


def _nsa_fused(zt, gates_t, kc, vc_t, bias_cmp_t, sel_tiles, win_tiles, c2s_t, onehot_t, n_sel, n_cmp):
    B, _, S = zt.shape
    G, Hg, Dh = N_KV_GROUPS, HEADS_PER_GROUP, HEAD_NSA
    NCP = kc.shape[2]
    NS = c2s_t.shape[0]
    QB = Q_BLOCK
    ND = sel_tiles.shape[2] - 2
    assert S % SEL_KEY_TILE == 0 and SEL_KEY_TILE % QB == 0
    z4 = zt.reshape(B, -1, Dh, S)
    stream = lambda n: pl.BlockSpec((1, G, Dh, S), lambda b, i: (b, Hg + n, 0, 0))
    bg = lambda *blk: pl.BlockSpec((1, G) + blk, lambda b, i: (b, 0) + (0,) * len(blk))
    gt = lambda arr: pl.BlockSpec(arr.shape, lambda b, i: (0,) * arr.ndim)
    return pl.pallas_call(
        functools.partial(_nsa_body, float(n_sel), n_cmp, ND),
        grid=(B, S // QB),
        in_specs=[pl.BlockSpec((1, G * Hg, Dh, QB), lambda b, i: (b, 0, 0, i)),
                  bg(NCP, Dh), bg(Dh, NCP), stream(0), stream(1), stream(2), stream(3),
                  pl.BlockSpec((1, G * NSA_GATE_ROWS, QB), lambda b, i: (b, 0, i)),
                  pl.BlockSpec((G, Hg, NCP, QB), lambda b, i: (0, 0, 0, i)),
                  gt(sel_tiles), gt(win_tiles),
                  pl.BlockSpec(c2s_t.shape, lambda b, i: (0, 0)),
                  pl.BlockSpec(onehot_t.shape, lambda b, i: (0, 0))],
        out_specs=pl.BlockSpec((1, G * Hg, Dh, QB), lambda b, i: (b, 0, 0, i)),
        out_shape=jax.ShapeDtypeStruct((B, G * Hg, Dh, S), jnp.float32),
        scratch_shapes=[pltpu.VMEM((G * Hg, Dh, QB), jnp.float32)] * 3,
        compiler_params=pltpu.CompilerParams(dimension_semantics=("arbitrary",) * 2,
                                             vmem_limit_bytes=VMEM_LIMIT),
        name="nsa_fused",
    )(z4, kc, vc_t, z4, z4, z4, z4, gates_t, bias_cmp_t, sel_tiles, win_tiles, c2s_t, onehot_t)


def _nsa_tables(rel_bias, seq):
    n_cmp = seq // CMP_STRIDE - CMP_BLOCK // CMP_STRIDE + 1
    ncp = -(-n_cmp // 128) * 128
    c = jnp.arange(ncp)[:, None]
    t = jnp.arange(seq)[None, :]
    bias_cmp_t = _bias_from_buckets(rel_bias, _t5_bucket(t - (c * CMP_STRIDE + CMP_BLOCK - 1)))
    base = _sel_bias_tiles(rel_bias, seq)
    sel_tiles = jnp.concatenate([jnp.full_like(base[:, :, :1], NEG_INF), base], axis=2)
    kj = np.arange(Q_BLOCK)[:, None]
    qi = np.arange(Q_BLOCK)[None, :]
    dwin = np.arange(N_BAND)[:, None, None] * Q_BLOCK + qi - kj
    nwin = min(N_BAND, base.shape[2])
    win_tiles = jnp.where(jnp.asarray(dwin[:nwin] < WINDOW), base[:, :, :nwin], NEG_INF)
    win_tiles = jnp.concatenate(
        [win_tiles, jnp.full(win_tiles.shape[:2] + (N_BAND + 1 - nwin,) + win_tiles.shape[3:], NEG_INF)], axis=2)
    n_sel_blocks = seq // SEL_BLOCK
    cs = np.arange(ncp)[None, :] * CMP_STRIDE
    ss = np.arange(n_sel_blocks)[:, None] * SEL_BLOCK
    overlap = np.clip(np.minimum(cs + CMP_BLOCK, ss + SEL_BLOCK) - np.maximum(cs, ss), 0, None)
    overlap = np.where(np.arange(ncp)[None, :] < n_cmp, overlap, 0)
    c2s_t = jnp.asarray(overlap.astype(np.float32) / CMP_BLOCK, jnp.bfloat16)
    onehot_t = jnp.asarray(np.arange(n_sel_blocks)[:, None] == np.arange(seq)[None, :] // SEL_BLOCK, jnp.bfloat16)
    return bias_cmp_t, sel_tiles, win_tiles, c2s_t, onehot_t, n_cmp


RWKV_CHUNK = 64
RWKV_INV_BASE = 8


def _split_bf16(x):
    hi = x.astype(jnp.bfloat16)
    lo = (x - hi.astype(jnp.float32)).astype(jnp.bfloat16)
    return hi, lo


def _dot3(a, b, dims=(((1,), (0,)), ((), ()))):
    a_hi, a_lo = _split_bf16(a)
    b_hi, b_lo = _split_bf16(b)
    d = functools.partial(lax.dot_general, dimension_numbers=dims, preferred_element_type=jnp.float32)
    free_axis = 1 - dims[0][0][0]
    m = a.shape[free_axis]
    both = d(jnp.concatenate([a_hi, a_lo], axis=free_axis), b_hi)
    return both[:m] + (both[m:] + d(a_hi, b_lo))


_NT = (((1,), (1,)), ((), ()))
_TN = (((0,), (0,)), ((), ()))


def _rwkv_chunk(r_w, lw, k_w, v_w, kk, lr, st_ref, n_heads):
    C = r_w.shape[0]
    N = r_w.shape[1] // n_heads
    H = range(n_heads)
    f32 = jnp.float32
    row = lax.broadcasted_iota(jnp.int32, (C, C), 0)
    col = lax.broadcasted_iota(jnp.int32, (C, C), 1)
    strict = col < row
    incl = col <= row
    eye = jnp.where(row == col, 1.0, 0.0).astype(f32)
    tri = jnp.where(incl, 1.0, 0.0).astype(jnp.bfloat16)

    cl = _dot_exact_lhs(tri, lw)

    base = RWKV_INV_BASE
    diag_blk = strict & ((row // base) == (col // base))
    level_masks = []
    s = base
    while s < C:
        level_masks.append(strict & ((row // (2 * s)) == (col // (2 * s))) & ((row // s) != (col // s)))
        s *= 2

    cl_end = cl[C - 1:C, :]
    bb = kk * lr
    g_inv = jnp.exp(-cl)
    g_rem = jnp.exp(cl_end - cl)
    g_end = jnp.exp(cl_end)
    abar_w = -kk * jnp.exp(cl - lw)
    rbar_w = r_w * jnp.exp(cl)
    bbar_w = bb * g_inv
    kbar_w = k_w * g_inv
    bhat_w = bb * g_rem
    khat_w = k_w * g_rem
    hs = lambda x, h: x[:, h * N:(h + 1) * N]

    abar = [hs(abar_w, h) for h in H]
    rbar = [hs(rbar_w, h) for h in H]
    v = [hs(v_w, h) for h in H]
    gmat = [_dot3(jnp.concatenate([abar[h], rbar[h]], axis=0),
                  jnp.concatenate([hs(bbar_w, h), hs(kbar_w, h)], axis=0), _NT) for h in H]
    a_ab = [jnp.where(strict, gmat[h][:C, :C], 0.0) for h in H]
    a_ak = [jnp.where(strict, gmat[h][:C, C:], 0.0) for h in H]
    m_rb = [jnp.where(incl, gmat[h][C:, :C], 0.0) for h in H]
    m_rk = [jnp.where(incl, gmat[h][C:, C:], 0.0) for h in H]
    akv = [_dot3(a_ak[h], v[h]) for h in H]
    dp = [jnp.where(diag_blk, a_ab[h], 0.0) for h in H]
    x = [eye + dp[h] for h in H]
    s = 2
    while s < base:
        dp = [_dot3(dp[h], dp[h]) for h in H]
        x = [x[h] + _dot3(x[h], dp[h]) for h in H]
        s *= 2
    for lm in level_masks:
        t = [_dot3(jnp.where(lm, a_ab[h], 0.0), x[h]) for h in H]
        x = [x[h] + _dot3(x[h], t[h]) for h in H]
    xw = [_dot3(x[h], jnp.concatenate([akv[h], abar[h]], axis=1)) for h in H]
    uv = [jnp.concatenate([xw[h][:, :N], v[h]], axis=0) for h in H]
    atil = [xw[h][:, N:] for h in H]
    y_loc = [_dot3(jnp.concatenate([m_rb[h], m_rk[h]], axis=1), uv[h]) for h in H]
    qm = [rbar[h] + _dot3(m_rb[h], atil[h]) for h in H]
    s_loc = [_dot3(jnp.concatenate([hs(bhat_w, h), hs(khat_w, h)], axis=0), uv[h], _TN) for h in H]
    pm = [eye[:N, :N] * hs(g_end, h) + _dot3(hs(bhat_w, h), atil[h], _TN) for h in H]
    s0 = [st_ref[h] for h in H]
    y = [y_loc[h] + _dot3(qm[h], s0[h]) for h in H]
    for h in H:
        st_ref[h] = s_loc[h] + _dot3(pm[h], s0[h])
    return jnp.concatenate(y, axis=1)


def _bf16_pieces(x):
    f32 = jnp.float32
    p1 = x.astype(jnp.bfloat16)
    r1 = x - p1.astype(f32)
    p2 = r1.astype(jnp.bfloat16)
    p3 = (r1 - p2.astype(f32)).astype(jnp.bfloat16)
    return p1, p2, p3


def _dot_exact_lhs(a_bf16, x):
    dd = functools.partial(jnp.dot, preferred_element_type=jnp.float32)
    p1, p2, p3 = _bf16_pieces(x)
    return dd(a_bf16, p1) + (dd(a_bf16, p2) + dd(a_bf16, p3))


def _dot_exact_rhs(x, b_bf16):
    dd = functools.partial(jnp.dot, preferred_element_type=jnp.float32)
    p1, p2, p3 = _bf16_pieces(x)
    return dd(p1, b_bf16) + (dd(p2, b_bf16) + dd(p3, b_bf16))


RWKV_BATCH_TILE = 2


def _rwkv_body(n_heads, z_ref, mu_ref, w0_ref, w2_ref, a0_ref, a2_ref, g2_ref, kk_ref, ka_ref, rk_ref,
               lng_ref, lnb_ref, y_ref, st_ref, prev_ref):
    c = pl.program_id(1)
    NB, C, _ = z_ref.shape
    W = y_ref.shape[2]
    N = W // n_heads
    f32, bf16 = jnp.float32, jnp.bfloat16
    dd = functools.partial(jnp.dot, preferred_element_type=f32)

    @pl.when(c == 0)
    def _():
        st_ref[...] = jnp.zeros_like(st_ref)
        prev_ref[...] = jnp.zeros_like(prev_ref)

    hrow = lax.broadcasted_iota(jnp.int32, (W, W), 0) // N
    hcol = lax.broadcasted_iota(jnp.int32, (W, W), 1) // N
    seg = jnp.where(hrow == hcol, 1.0, 0.0).astype(bf16)
    o = np.cumsum((0,) + SHIFT_SPLITS)
    ops = []
    for b in range(NB):
        z = z_ref[b]
        z_prev = jnp.concatenate([prev_ref[b], z[:C - 1]], axis=0)
        prev_ref[b] = z[C - 1:C]
        zs = z + (z_prev - z) * mu_ref[...]
        r, k, v, wl, al, gl = (zs[:, o[i]:o[i + 1]] for i in range(6))
        yw = w0_ref[...] + dd(jnp.tanh(wl).astype(bf16), w2_ref[...])
        logw = -(jnp.maximum(-yw, 0.0) + jnp.log1p(jnp.exp(-jnp.abs(yw)))) - 0.5
        lw = -jnp.exp(logw)
        lr = jax.nn.sigmoid(a0_ref[...] + dd(al.astype(bf16), a2_ref[...]))
        g = dd(jax.nn.sigmoid(gl).astype(bf16), g2_ref[...])
        kk = k * kk_ref[...]
        kk = kk / jnp.maximum(jnp.sqrt(_dot_exact_rhs(kk * kk, seg)), 1e-12)
        k = k * (1.0 + (lr - 1.0) * ka_ref[...])
        ops.append((r, lw, k, v, kk, lr, g))
    side = lambda idx: jnp.concatenate([op[idx] for op in ops], axis=1)
    y_all = _rwkv_chunk(side(0), side(1), side(2), side(3), side(4), side(5), st_ref, NB * n_heads)
    for b, (r, lw, k, v, kk, lr, g) in enumerate(ops):
        y = y_all[:, b * W:(b + 1) * W]
        mu = _dot_exact_rhs(y, seg) * (1.0 / N)
        yc = y - mu
        var = _dot_exact_rhs(yc * yc, seg) * (1.0 / N)
        yn = yc * lax.rsqrt(var + RWKV_GN_EPS) * lng_ref[...] + lnb_ref[...]
        bonus = _dot_exact_rhs(r * k * rk_ref[...], seg) * v
        y_ref[b] = (yn + bonus) * g


def _rwkv_fused(z, batch, seq, shift_mu, w0, w2, a0, a2, g2, k_k, k_a, r_k, ln_g, ln_b):
    H, N, W = N_HEADS_RWKV, HEAD_RWKV, RWKV_WIDTH
    C = min(RWKV_CHUNK, seq)
    assert N <= C and seq % C == 0 and SHIFT_WIDTH % 128 == 0
    nb = RWKV_BATCH_TILE if batch % RWKV_BATCH_TILE == 0 else 1
    bf16 = jnp.bfloat16
    row = lambda a: a.reshape(1, -1).astype(jnp.float32)
    params = [row(shift_mu), row(w0), w2.astype(bf16), row(a0), a2.astype(bf16), g2.astype(bf16),
              row(k_k), row(k_a), row(r_k), row(ln_g), row(ln_b)]
    full = lambda a: pl.BlockSpec(a.shape, lambda b, c: (0, 0))
    y = pl.pallas_call(
        functools.partial(_rwkv_body, H),
        grid=(batch // nb, seq // C),
        in_specs=[pl.BlockSpec((nb, C, SHIFT_WIDTH), lambda b, c: (b, c, 0))] + [full(p) for p in params],
        out_specs=pl.BlockSpec((nb, C, W), lambda b, c: (b, c, 0)),
        out_shape=jax.ShapeDtypeStruct((batch, seq, W), jnp.float32),
        scratch_shapes=[pltpu.VMEM((nb * H, N, N), jnp.float32), pltpu.VMEM((nb, 1, SHIFT_WIDTH), jnp.float32)],
        compiler_params=pltpu.CompilerParams(dimension_semantics=("arbitrary", "arbitrary"),
                                             vmem_limit_bytes=VMEM_LIMIT),
        name="rwkv_fused",
    )(z.reshape(batch, seq, -1), *params)
    return y.reshape(batch * seq, W)


def _moe_body(be_ref, nu_ref, x_ref, w1_ref, w3_ref, w2_ref, o_ref):
    bf16 = jnp.bfloat16
    used = pl.program_id(0) < nu_ref[0]

    @pl.when(used)
    def _():
        x = x_ref[...].astype(bf16)
        h1 = jnp.dot(x, w1_ref[0, 0].astype(bf16), preferred_element_type=jnp.float32)
        h3 = jnp.dot(x, w3_ref[0, 0].astype(bf16), preferred_element_type=jnp.float32)
        h = (h1 * jax.nn.sigmoid(h1)) * h3
        o_ref[...] = jnp.dot(h.astype(bf16), w2_ref[0, 0].astype(bf16), preferred_element_type=jnp.float32)

    @pl.when(jnp.logical_not(used))
    def _():
        o_ref[...] = jnp.zeros_like(o_ref)


def _moe_experts(xs, blk_exp, n_used, w1, w3, w2, layer, blk):
    n_rows, D = xs.shape
    De = w1.shape[3]
    grid_spec = pltpu.PrefetchScalarGridSpec(
        num_scalar_prefetch=2,
        grid=(n_rows // blk,),
        in_specs=[pl.BlockSpec((blk, D), lambda i, be, nu: (i, 0)),
                  pl.BlockSpec((1, 1, D, De), lambda i, be, nu: (layer, be[i], 0, 0)),
                  pl.BlockSpec((1, 1, D, De), lambda i, be, nu: (layer, be[i], 0, 0)),
                  pl.BlockSpec((1, 1, De, D), lambda i, be, nu: (layer, be[i], 0, 0))],
        out_specs=pl.BlockSpec((blk, D), lambda i, be, nu: (i, 0)))
    return pl.pallas_call(
        _moe_body,
        grid_spec=grid_spec,
        out_shape=jax.ShapeDtypeStruct((n_rows, D), jnp.float32),
        compiler_params=pltpu.CompilerParams(dimension_semantics=("arbitrary",),
                                             vmem_limit_bytes=VMEM_LIMIT),
        name="moe_experts",
    )(blk_exp, n_used, xs, w1, w3, w2)


MOE_DEST_CHUNK = 512


def _moe_dest_body(n_exp, blk, eid_ref, dest_ref, cnt_ref, run_ref, pst_ref):
    ph, c = pl.program_id(0), pl.program_id(1)
    T = eid_ref.shape[1]
    f32, bf16 = jnp.float32, jnp.bfloat16
    onehot = jnp.where(lax.broadcasted_iota(jnp.int32, (n_exp, T), 0) == eid_ref[...], 1.0, 0.0)
    here = jnp.sum(onehot, axis=1, keepdims=True)

    @pl.when((ph == 0) & (c == 0))
    def _():
        run_ref[...] = jnp.zeros_like(run_ref)

    @pl.when(ph == 0)
    def _():
        run_ref[...] = run_ref[...] + here
        dest_ref[...] = jnp.zeros_like(dest_ref)

    @pl.when((ph == 1) & (c == 0))
    def _():
        counts = run_ref[...]
        cnt_ref[...] = counts
        padded = jnp.floor((counts + (blk - 1)) * (1.0 / blk)) * blk
        er = lax.broadcasted_iota(jnp.int32, (n_exp, n_exp), 0)
        ec = lax.broadcasted_iota(jnp.int32, (n_exp, n_exp), 1)
        pst_ref[...] = _dot_exact_lhs(jnp.where(ec < er, 1.0, 0.0).astype(bf16), padded)
        run_ref[...] = jnp.zeros_like(run_ref)

    @pl.when(ph == 1)
    def _():
        sr = lax.broadcasted_iota(jnp.int32, (T, T), 0)
        sc = lax.broadcasted_iota(jnp.int32, (T, T), 1)
        earlier = jnp.dot(onehot.astype(bf16), jnp.where(sr < sc, 1.0, 0.0).astype(bf16),
                          preferred_element_type=f32)
        base = pst_ref[:, 0:1] + run_ref[:, 0:1]
        dest = jnp.sum(onehot * (earlier + base), axis=0, keepdims=True)
        dest_ref[...] = dest.astype(jnp.int32)
        run_ref[...] = run_ref[...] + here


def _moe_dest(eid_row, n_exp, blk):
    n_slots = eid_row.shape[1]
    T = min(MOE_DEST_CHUNK, n_slots)
    assert n_slots % T == 0 and n_slots + n_exp * blk < 2 ** 24
    dest, cnt = pl.pallas_call(
        functools.partial(_moe_dest_body, n_exp, blk),
        grid=(2, n_slots // T),
        in_specs=[pl.BlockSpec((1, T), lambda p, c: (0, c))],
        out_specs=[pl.BlockSpec((1, T), lambda p, c: (0, c * p)),
                   pl.BlockSpec((n_exp, 128), lambda p, c: (0, 0))],
        out_shape=[jax.ShapeDtypeStruct((1, n_slots), jnp.int32),
                   jax.ShapeDtypeStruct((n_exp, 128), jnp.float32)],
        scratch_shapes=[pltpu.VMEM((n_exp, 128), jnp.float32), pltpu.VMEM((n_exp, 128), jnp.float32)],
        compiler_params=pltpu.CompilerParams(dimension_semantics=("arbitrary", "arbitrary")),
        name="moe_dest",
    )(eid_row)
    return dest, cnt[:, 0].astype(jnp.int32)


MOE_TOKEN_TILE = 256


def _row_copy(src_ref, src_row, dst_ref, dst_row, sem):
    return pltpu.make_async_copy(src_ref.at[pl.ds(src_row, 1)], dst_ref.at[pl.ds(dst_row, 1)], sem)


def _moe_dispatch_body(dest_ref, x_ref, init_ref, xs_ref, sem):
    del init_ref
    n_choice, T = dest_ref.shape

    def start(r, carry):
        for k in range(n_choice):
            _row_copy(x_ref, r, xs_ref, dest_ref[k, r], sem).start(priority=k % 2)
        return carry

    def wait(r, carry):
        for k in range(n_choice):
            _row_copy(x_ref, 0, xs_ref, 0, sem).wait()
        return carry

    lax.fori_loop(0, T, start, 0, unroll=8)
    lax.fori_loop(0, T, wait, 0)


def _moe_dispatch(x, dest2, n_rows):
    N, D = x.shape
    T = min(MOE_TOKEN_TILE, N)
    assert N % T == 0
    return pl.pallas_call(
        _moe_dispatch_body,
        grid=(N // T,),
        in_specs=[pl.BlockSpec((dest2.shape[0], T), lambda i: (0, i), memory_space=pltpu.SMEM),
                  pl.BlockSpec((T, D), lambda i: (i, 0)),
                  pl.BlockSpec(memory_space=pl.ANY)],
        out_specs=pl.BlockSpec(memory_space=pl.ANY),
        out_shape=jax.ShapeDtypeStruct((n_rows, D), x.dtype),
        scratch_shapes=[pltpu.SemaphoreType.DMA(())],
        input_output_aliases={2: 0},
        compiler_params=pltpu.CompilerParams(dimension_semantics=("arbitrary",)),
        name="moe_dispatch",
    )(dest2, x, jnp.zeros((n_rows, D), x.dtype))


def _moe_combine_body(dest_ref, ys_ref, w_ref, x_ref, g_ref, b_ref, o_ref, buf_ref, sem):
    n_choice, T = dest_ref.shape

    def start(r, carry):
        for k in range(n_choice):
            _row_copy(ys_ref, dest_ref[k, r], buf_ref.at[k], r, sem).start(priority=k % 2)
        return carry

    def wait(r, carry):
        for k in range(n_choice):
            _row_copy(ys_ref, 0, buf_ref.at[k], 0, sem).wait()
        return carry

    lax.fori_loop(0, T, start, 0, unroll=8)
    lax.fori_loop(0, T, wait, 0)
    w = w_ref[...]
    h = w[:, 0:1] * buf_ref[0]
    for k in range(1, n_choice):
        h = h + w[:, k:k + 1] * buf_ref[k]
    y = ALPHA * x_ref[...] + h
    mu = jnp.mean(y, axis=-1, keepdims=True)
    var = jnp.mean(jnp.square(y - mu), axis=-1, keepdims=True)
    o_ref[...] = (y - mu) * lax.rsqrt(var + LN_EPS) * g_ref[...] + b_ref[...]


def _moe_combine(ys, dest2, w, x, ln_g, ln_b):
    N, D = x.shape
    n_choice = dest2.shape[0]
    T = min(MOE_TOKEN_TILE, N)
    assert N % T == 0
    ln_g, ln_b = ln_g.reshape(1, D), ln_b.reshape(1, D)
    return pl.pallas_call(
        _moe_combine_body,
        grid=(N // T,),
        in_specs=[pl.BlockSpec((n_choice, T), lambda i: (0, i), memory_space=pltpu.SMEM),
                  pl.BlockSpec(memory_space=pl.ANY),
                  pl.BlockSpec((T, n_choice), lambda i: (i, 0)),
                  pl.BlockSpec((T, D), lambda i: (i, 0)),
                  pl.BlockSpec((1, D), lambda i: (0, 0)),
                  pl.BlockSpec((1, D), lambda i: (0, 0))],
        out_specs=pl.BlockSpec((T, D), lambda i: (i, 0)),
        out_shape=jax.ShapeDtypeStruct((N, D), jnp.float32),
        scratch_shapes=[pltpu.VMEM((n_choice, T, D), jnp.float32), pltpu.SemaphoreType.DMA(())],
        compiler_params=pltpu.CompilerParams(dimension_semantics=("arbitrary",)),
        name="moe_combine",
    )(dest2, ys, w, x, ln_g, ln_b)


def _split_cols(z, sizes):
    return jnp.split(z, np.cumsum(sizes)[:-1].tolist(), axis=-1)


def _layer_norm(x, g, b, eps=LN_EPS):
    mu = jnp.mean(x, axis=-1, keepdims=True)
    var = jnp.mean(jnp.square(x - mu), axis=-1, keepdims=True)
    return (x - mu) * lax.rsqrt(var + eps) * g + b


def _t5_bucket(dist):
    n = jnp.maximum(dist, 0)
    max_exact = NUM_BUCKETS // 2
    nf = jnp.maximum(n, 1).astype(jnp.float32)
    large = max_exact + (jnp.log(nf / max_exact) / math.log(MAX_DISTANCE / max_exact)
                         * (NUM_BUCKETS - max_exact)).astype(jnp.int32)
    large = jnp.minimum(large, NUM_BUCKETS - 1)
    return jnp.where(n < max_exact, n, large)


def _bias_from_buckets(rel_bias, bucket):
    rb = rel_bias.astype(jnp.float32)
    shape = (N_KV_GROUPS, HEADS_PER_GROUP) + (1,) * bucket.ndim
    out = jnp.zeros((N_KV_GROUPS, HEADS_PER_GROUP) + bucket.shape, jnp.float32)
    for b in range(NUM_BUCKETS):
        out = jnp.where(bucket == b, rb[b].reshape(shape), out)
    return out


def _nsa_positional(rel_bias, seq):
    n_cmp = seq // CMP_STRIDE - CMP_BLOCK // CMP_STRIDE + 1
    n_sel_blocks = seq // SEL_BLOCK
    t = jnp.arange(seq)[:, None]
    c = jnp.arange(n_cmp)[None, :]
    d_cmp = t - (c * CMP_STRIDE + CMP_BLOCK - 1)
    mask_cmp = d_cmp >= 0
    bias_cmp = _bias_from_buckets(rel_bias, _t5_bucket(d_cmp))
    qo = jnp.arange(Q_BLOCK)[:, None]
    m = jnp.arange(WINDOW + Q_BLOCK)[None, :]
    d_win = qo + WINDOW - m
    bias_win = _bias_from_buckets(rel_bias, _t5_bucket(d_win))
    blk = jnp.arange(seq // Q_BLOCK)[:, None, None]
    mask_win = (d_win >= 0) & (d_win < WINDOW) & (blk * Q_BLOCK - WINDOW + m >= 0)
    cs = jnp.arange(n_cmp)[:, None] * CMP_STRIDE
    ss = jnp.arange(n_sel_blocks)[None, :] * SEL_BLOCK
    overlap = jnp.clip(jnp.minimum(cs + CMP_BLOCK, ss + SEL_BLOCK) - jnp.maximum(cs, ss), 0, None)
    cmp_to_sel = overlap.astype(jnp.float32) / CMP_BLOCK
    return bias_cmp, mask_cmp, bias_win, mask_win, cmp_to_sel, _sel_bias_tiles(rel_bias, seq)


def _rwkv7_time_mix(r, k, v, wl, al, gl, w0, w2, a0, a2, g2, k_k, k_a, r_k, ln_g, ln_b):
    B, S, C = r.shape
    H, N = N_HEADS_RWKV, HEAD_RWKV
    f32 = jnp.float32
    logw = -jax.nn.softplus(-(w0 + jnp.tanh(wl) @ w2).astype(f32)) - 0.5
    log_decay = -jnp.exp(logw)
    a = jax.nn.sigmoid((a0 + al @ a2).astype(f32))
    g = jax.nn.sigmoid(gl) @ g2
    heads = lambda t: t.reshape(B, S, H, N)
    kk = heads(k * k_k)
    kk = kk / jnp.maximum(jnp.linalg.norm(kk, axis=-1, keepdims=True), 1e-12)
    k = k * (1.0 + (a - 1.0) * k_a)
    rh, kh, vh = heads(r), heads(k), heads(v)
    y = heads(_rwkv_scan(r, log_decay, k, v, kk.reshape(B, S, C), a, H))
    mu = jnp.mean(y, axis=-1, keepdims=True)
    var = jnp.mean(jnp.square(y - mu), axis=-1, keepdims=True)
    y = ((y - mu) * lax.rsqrt(var + RWKV_GN_EPS)).reshape(B, S, C) * ln_g + ln_b
    bonus = jnp.sum(rh * kh * r_k, axis=-1, keepdims=True) * vh
    return (y + bonus.reshape(B, S, C)) * g


def _compress(t, pe, w1, w2):
    B, S, G, Dh = t.shape
    rep = CMP_BLOCK // CMP_STRIDE
    nc = S // CMP_STRIDE - rep + 1
    sub = t.reshape(B, S // CMP_STRIDE, CMP_STRIDE, G, Dh)
    blk = jnp.concatenate([sub[:, j:j + nc] for j in range(rep)], axis=2)
    blk = blk + pe[:, None, :]
    blk = blk.transpose(0, 1, 3, 2, 4).reshape(B, nc, G, CMP_BLOCK * Dh)
    out = jax.nn.gelu(blk @ w1) @ w2
    return out.transpose(0, 2, 1, 3)


def _nsa_attention(zt, gates_t, k_cmp, v_cmp, pe_k, w1_k, w2_k, pe_v, w1_v, w2_v, tables):
    bias_cmp_t, sel_tiles, win_tiles, c2s_t, onehot_t, n_cmp = tables
    B, S, _ = k_cmp.shape
    G, Dh = N_KV_GROUPS, HEAD_NSA
    bf16 = jnp.bfloat16
    ncp = bias_cmp_t.shape[2]
    kc = _compress(k_cmp.reshape(B, S, G, Dh), pe_k, w1_k, w2_k)
    vc = _compress(v_cmp.reshape(B, S, G, Dh), pe_v, w1_v, w2_v)
    pad = ((0, 0), (0, 0), (0, ncp - n_cmp), (0, 0))
    kc_p = jnp.pad(kc, pad).astype(bf16)
    vc_t = jnp.pad(vc, pad).transpose(0, 1, 3, 2).astype(bf16)
    n_sel = min(N_SELECT, S // SEL_BLOCK)
    o_t = _nsa_fused(zt, gates_t, kc_p, vc_t, bias_cmp_t, sel_tiles, win_tiles, c2s_t, onehot_t,
                     n_sel, n_cmp)
    return o_t.reshape(B, NSA_Q_WIDTH, S)


def _split_in_proj(w_in):
    D = w_in.shape[0]
    scale = HEAD_NSA ** -0.5
    assert math.frexp(scale)[0] == 0.5
    o = np.cumsum((SHIFT_WIDTH,) + REST_SPLITS)
    col = lambda a, b: w_in[:, o[a]:o[b]]
    row_part = [w_in[:, :SHIFT_WIDTH], col(1, 3)]
    lead = SHIFT_WIDTH + int(o[3] - o[1])
    lead_pad = (-lead) % D_MODEL
    w_row = jnp.concatenate(row_part + [jnp.zeros((D, lead_pad), w_in.dtype), col(8, 10)], axis=1)
    gate_blk = (lead + lead_pad) // D_MODEL
    assert w_row.shape[1] % IN_PROJ_TN == 0
    G, n_g = N_KV_GROUPS, 3 * HEADS_PER_GROUP
    gates = col(7, 8).reshape(D, G, n_g)
    gates = jnp.pad(gates, ((0, 0), (0, 0), (0, NSA_GATE_ROWS - n_g))).reshape(D, G * NSA_GATE_ROWS)
    w_t = jnp.concatenate([col(0, 1) * scale, col(3, 7), gates], axis=1).T
    return w_row.astype(jnp.bfloat16), w_t.astype(jnp.bfloat16), gate_blk, int(o[7] - o[3]) + NSA_Q_WIDTH


def _token_mixer(x, w_in, shift_mu, rw_w0, rw_w2, rw_a0, rw_a2, rw_g2, rw_kk, rw_ka, rw_rk,
                 rw_ln_g, rw_ln_b, cmp_pe_k, cmp_w1_k, cmp_w2_k, cmp_pe_v, cmp_w1_v, cmp_w2_v,
                 w_up_rwkv, w_up_nsa, w_out, ln_g, ln_b, rel_bias, pos):
    B, S, D = x.shape
    xf = x.reshape(B * S, D)
    w_row, w_t, gate_blk, n_main = _split_in_proj(w_in)
    z = _matmul(xf, w_row, IN_PROJ_TM, IN_PROJ_TN)
    zt, gates_t = _matmul_t(xf, w_t, n_main, B, S)
    z3 = z.reshape(B, S, -1)
    k_cmp = z3[..., SHIFT_WIDTH:SHIFT_WIDTH + NSA_KV_WIDTH]
    v_cmp = z3[..., SHIFT_WIDTH + NSA_KV_WIDTH:SHIFT_WIDTH + 2 * NSA_KV_WIDTH]
    y_rw = _rwkv_fused(z, B, S, shift_mu, rw_w0, rw_w2, rw_a0, rw_a2, rw_g2,
                       rw_kk, rw_ka, rw_rk, rw_ln_g, rw_ln_b)
    y_nsa_t = _nsa_attention(zt, gates_t, k_cmp, v_cmp, cmp_pe_k, cmp_w1_k, cmp_w2_k,
                             cmp_pe_v, cmp_w1_v, cmp_w2_v, pos)
    bf16 = jnp.bfloat16
    return _merge_out(y_rw, y_nsa_t, z, gate_blk, xf,
                      w_up_rwkv.astype(bf16), w_up_nsa.astype(bf16), w_out.astype(bf16),
                      ln_g, ln_b).reshape(B, S, D)


def _hier_moe(x, wg, bg, we, be, w1, w3, w2, layer, ln_g, ln_b):
    B, S, D = x.shape
    N = B * S
    f32 = jnp.float32
    xf = x.reshape(N, D)
    g_prob = jax.nn.softmax((xf @ wg + bg).astype(f32), axis=-1)
    grp = jnp.argmax(g_prob, axis=-1)
    p_grp = jnp.take_along_axis(g_prob, grp[:, None], axis=1)[:, 0]
    e_logits = (xf @ we + be).astype(f32).reshape(N, N_GROUPS, EXPERTS_PER_GROUP)
    e_logits = jnp.take_along_axis(e_logits, grp[:, None, None], axis=1)[:, 0]
    top_v, top_i = lax.top_k(e_logits, TOP_K_INNER)
    top_w = jax.nn.softmax(top_v, axis=-1) * p_grp[:, None]
    eid = (grp[:, None] * EXPERTS_PER_GROUP + top_i).astype(jnp.int32)
    n_slots = N * TOP_K_INNER
    n_rows = n_slots + N_EXPERTS * MOE_BLOCK
    n_blk = n_rows // MOE_BLOCK
    dest, counts = _moe_dest(eid.T.reshape(1, n_slots), N_EXPERTS, MOE_BLOCK)
    dest2 = dest.reshape(TOP_K_INNER, N)
    pcounts = (counts + MOE_BLOCK - 1) // MOE_BLOCK * MOE_BLOCK
    pends = jnp.cumsum(pcounts)
    blk_exp = jnp.minimum(jnp.sum(jnp.arange(n_blk)[:, None] * MOE_BLOCK >= pends[None, :], axis=1),
                          N_EXPERTS - 1).astype(jnp.int32)
    xs = _moe_dispatch(xf, dest2, n_rows)
    n_used = (pends[-1:] // MOE_BLOCK).astype(jnp.int32)
    ys = _moe_experts(xs, blk_exp, n_used, w1, w3, w2, layer, MOE_BLOCK)
    return _moe_combine(ys, dest2, top_w, xf, ln_g, ln_b).reshape(B, S, D)


def kernel(x, rel_bias, w_in, shift_mu, rw_w0, rw_w2, rw_a0, rw_a2, rw_g2, rw_kk, rw_ka, rw_rk,
           rw_ln_g, rw_ln_b, cmp_pe_k, cmp_w1_k, cmp_w2_k, cmp_pe_v, cmp_w1_v, cmp_w2_v,
           w_up_rwkv, w_up_nsa, w_out, ln1_g, ln1_b, router_group_w, router_group_b,
           router_expert_w, router_expert_b, exp_w1, exp_w3, exp_w2, ln2_g, ln2_b):
    pos = _nsa_tables(rel_bias, x.shape[1])
    for l in range(DEPTH):
        x = _token_mixer(x, w_in[l], shift_mu[l], rw_w0[l], rw_w2[l], rw_a0[l], rw_a2[l], rw_g2[l],
                         rw_kk[l], rw_ka[l], rw_rk[l], rw_ln_g[l], rw_ln_b[l],
                         cmp_pe_k[l], cmp_w1_k[l], cmp_w2_k[l], cmp_pe_v[l], cmp_w1_v[l], cmp_w2_v[l],
                         w_up_rwkv[l], w_up_nsa[l], w_out[l], ln1_g[l], ln1_b[l], rel_bias, pos)
        x = _hier_moe(x, router_group_w[l], router_group_b[l], router_expert_w[l], router_expert_b[l],
                      exp_w1, exp_w3, exp_w2, l, ln2_g[l], ln2_b[l])
    return x
```

```python
import functools
import math

import jax
import jax.numpy as jnp
import numpy as np
from jax import lax
from jax.experimental import pallas as pl
from jax.experimental.pallas import tpu as pltpu

D_MODEL = 1024
DEPTH = 4
N_HEADS_RWKV = 8
HEAD_RWKV = 64
RWKV_WIDTH = N_HEADS_RWKV * HEAD_RWKV
LORA_W = 64
LORA_A = 64
LORA_G = 128
RWKV_GN_EPS = 64e-5
N_HEADS_NSA = 8
N_KV_GROUPS = 2
HEADS_PER_GROUP = N_HEADS_NSA // N_KV_GROUPS
HEAD_NSA = 64
NSA_Q_WIDTH = N_HEADS_NSA * HEAD_NSA
NSA_KV_WIDTH = N_KV_GROUPS * HEAD_NSA
CMP_BLOCK = 32
CMP_STRIDE = 16
CMP_HIDDEN = 128
SEL_BLOCK = 64
N_SELECT = 16
WINDOW = 512
Q_BLOCK = 128
N_BAND = WINDOW // Q_BLOCK + 1
NEG_INF = -1e30
FORCED_SCORE = 1e4
NUM_BUCKETS = 32
MAX_DISTANCE = 1024
N_GROUPS = 4
EXPERTS_PER_GROUP = 8
N_EXPERTS = N_GROUPS * EXPERTS_PER_GROUP
TOP_K_INNER = 2
D_EXPERT = 512
MOE_BLOCK = 256
ALPHA = (2 * DEPTH) ** 0.25
LN_EPS = 1e-5
SHIFT_SPLITS = (RWKV_WIDTH, RWKV_WIDTH, RWKV_WIDTH, LORA_W, LORA_A, LORA_G)
SHIFT_WIDTH = 3 * RWKV_WIDTH + LORA_W + LORA_A + LORA_G
REST_SPLITS = (NSA_Q_WIDTH,) + (NSA_KV_WIDTH,) * 6 + (3 * N_HEADS_NSA, D_MODEL, D_MODEL)

VMEM_LIMIT = 48 * 1024 * 1024


IN_PROJ_TM = 1024
IN_PROJ_TN = 1024


def _mm_body(x_ref, w_ref, o_ref, xb_ref):
    @pl.when(pl.program_id(1) == 0)
    def _():
        xb_ref[...] = x_ref[...].astype(jnp.bfloat16)

    o_ref[...] = jnp.dot(xb_ref[...], w_ref[...], preferred_element_type=jnp.float32)


def _matmul(x, w, tm, tn):
    m, k = x.shape
    n = w.shape[1]
    assert m % tm == 0 and n % tn == 0
    return pl.pallas_call(
        _mm_body,
        grid=(m // tm, n // tn),
        in_specs=[pl.BlockSpec((tm, k), lambda i, j: (i, 0)),
                  pl.BlockSpec((k, tn), lambda i, j: (0, j))],
        out_specs=pl.BlockSpec((tm, tn), lambda i, j: (i, j)),
        out_shape=jax.ShapeDtypeStruct((m, n), jnp.float32),
        scratch_shapes=[pltpu.VMEM((tm, k), jnp.bfloat16)],
        compiler_params=pltpu.CompilerParams(dimension_semantics=("arbitrary", "arbitrary"),
                                             vmem_limit_bytes=VMEM_LIMIT),
        name="in_proj",
    )(x, w)


IN_PROJ_T_TM = 512
NSA_GATE_ROWS = 16


def _mm_t_body(n_main, x_ref, wt_ref, zt_ref, gt_ref):
    xb = x_ref[...].astype(jnp.bfloat16)
    out = lax.dot_general(wt_ref[...], xb, (((1,), (1,)), ((), ())), preferred_element_type=jnp.float32)
    zt_ref[0] = out[:n_main].astype(zt_ref.dtype)
    gt_ref[0] = out[n_main:]


def _matmul_t(x, wt, n_main, batch, seq):
    m, k = x.shape
    r = wt.shape[0]
    tm = min(IN_PROJ_T_TM, seq)
    assert seq % tm == 0 and m == batch * seq
    per_b = seq // tm
    return pl.pallas_call(
        functools.partial(_mm_t_body, n_main),
        grid=(m // tm,),
        in_specs=[pl.BlockSpec((tm, k), lambda i: (i, 0)),
                  pl.BlockSpec((r, k), lambda i: (0, 0))],
        out_specs=[pl.BlockSpec((1, n_main, tm), lambda i: (i // per_b, 0, i % per_b)),
                   pl.BlockSpec((1, r - n_main, tm), lambda i: (i // per_b, 0, i % per_b))],
        out_shape=[jax.ShapeDtypeStruct((batch, n_main, seq), jnp.bfloat16),
                   jax.ShapeDtypeStruct((batch, r - n_main, seq), jnp.float32)],
        compiler_params=pltpu.CompilerParams(dimension_semantics=("arbitrary",),
                                             vmem_limit_bytes=VMEM_LIMIT),
        name="in_proj_t",
    )(x, wt)


MERGE_TM = 256


def _merge_body(yr_ref, yn_ref, grw_ref, gns_ref, x_ref, wur_ref, wun_ref, wo_ref, g_ref, b_ref, o_ref):
    f32, bf16 = jnp.float32, jnp.bfloat16
    up_r = jnp.dot(yr_ref[...].astype(bf16), wur_ref[...], preferred_element_type=f32)
    up_n = lax.dot_general(yn_ref[0].astype(bf16), wun_ref[...], _TN, preferred_element_type=f32)
    merged = jax.nn.sigmoid(grw_ref[...]) * up_r + jax.nn.sigmoid(gns_ref[...]) * up_n
    y = ALPHA * x_ref[...] + jnp.dot(merged.astype(bf16), wo_ref[...], preferred_element_type=f32)
    mu = jnp.mean(y, axis=-1, keepdims=True)
    var = jnp.mean(jnp.square(y - mu), axis=-1, keepdims=True)
    o_ref[...] = (y - mu) * lax.rsqrt(var + LN_EPS) * g_ref[...] + b_ref[...]


def _merge_out(y_rw, y_nsa_t, z, gate_blk, x, w_up_rwkv, w_up_nsa, w_out, ln_g, ln_b):
    m, d = x.shape
    tm = MERGE_TM
    nb, wn, seq = y_nsa_t.shape
    assert m % tm == 0 and seq % tm == 0
    per_b = seq // tm
    row = lambda width: pl.BlockSpec((tm, width), lambda i: (i, 0))
    full = lambda a: pl.BlockSpec(a.shape, lambda i: (0,) * a.ndim)
    ln_g, ln_b = ln_g.reshape(1, d), ln_b.reshape(1, d)
    return pl.pallas_call(
        _merge_body,
        grid=(m // tm,),
        in_specs=[row(y_rw.shape[1]), pl.BlockSpec((1, wn, tm), lambda i: (i // per_b, 0, i % per_b)),
                  pl.BlockSpec((tm, d), lambda i: (i, gate_blk)),
                  pl.BlockSpec((tm, d), lambda i: (i, gate_blk + 1)),
                  row(d), full(w_up_rwkv), full(w_up_nsa), full(w_out), full(ln_g), full(ln_b)],
        out_specs=row(d),
        out_shape=jax.ShapeDtypeStruct((m, d), jnp.float32),
        compiler_params=pltpu.CompilerParams(dimension_semantics=("arbitrary",),
                                             vmem_limit_bytes=VMEM_LIMIT),
        name="merge_out",
    )(y_rw, y_nsa_t, z, z, x, w_up_rwkv, w_up_nsa, w_out, ln_g, ln_b)


def _sel_bias_tiles(rel_bias, seq):
    nd = -(-MAX_DISTANCE // Q_BLOCK)
    nd = min(nd, seq // Q_BLOCK)
    half = NUM_BUCKETS // 2
    min_far = nd * Q_BLOCK - (Q_BLOCK - 1)
    assert half + math.log(min_far / half) / math.log(MAX_DISTANCE / half) * half >= NUM_BUCKETS - 0.75
    kj = np.arange(Q_BLOCK)[:, None]
    qi = np.arange(Q_BLOCK)[None, :]
    dist = np.arange(nd + 1)[:, None, None] * Q_BLOCK + qi - kj
    dist[nd] = max(seq - 1, nd * Q_BLOCK)
    tiles = _bias_from_buckets(rel_bias, _t5_bucket(jnp.asarray(np.maximum(dist, 0))))
    causal = jnp.asarray(dist >= 0)
    return jnp.where(causal, tiles, NEG_INF)


SEL_KEY_TILE = 256


def _flash_update(lgs, v_ts, ms, ls, acc_ref):
    H = range(len(lgs))
    m_new = []
    for h in H:
        m = ms[h]
        for lg in lgs[h]:
            m = jnp.maximum(m, jnp.max(lg, axis=0, keepdims=True))
        m_new.append(m)
    ps = [[jnp.exp(lg - m_new[h]) for lg in lgs[h]] for h in H]
    alpha = [jnp.exp(ms[h] - m_new[h]) for h in H]
    l_new = []
    for h in H:
        l = alpha[h] * ls[h]
        for p in ps[h]:
            l = l + jnp.sum(p, axis=0, keepdims=True)
        l_new.append(l)
    pv = [[jnp.dot(v_t, p.astype(jnp.bfloat16), preferred_element_type=jnp.float32)
           for p, v_t in zip(ps[h], v_ts[h])] for h in H]
    for h in H:
        acc = alpha[h] * acc_ref[h]
        for o in pv[h]:
            acc = acc + o
        acc_ref[h] = acc
    return tuple(m_new), tuple(l_new)


def _nsa_body(n_sel, n_cmp, n_delta, q_ref, kc_ref, vc_ref, ks_ref, vs_ref, kw_ref, vw_ref, gl_ref,
              bc_ref, ts_ref, tw_ref, c2s_ref, oh_ref, o_ref, acc_s, acc_w, ocmp_ref):
    i = pl.program_id(1)
    hg = HEADS_PER_GROUP
    ng = q_ref.shape[1] // hg
    dh, qb = q_ref.shape[2], q_ref.shape[3]
    ncp = kc_ref.shape[2]
    ns = c2s_ref.shape[0]
    f32, bf16 = jnp.float32, jnp.bfloat16
    H = range(ng * hg)
    G = range(ng)

    cidx = lax.broadcasted_iota(jnp.int32, (ncp, qb), 0)
    tpos = lax.broadcasted_iota(jnp.int32, (ncp, qb), 1) + i * qb
    valid = (tpos - (cidx * CMP_STRIDE + CMP_BLOCK - 1)) >= 0
    fill = jnp.where(cidx < n_cmp, NEG_INF, 2.0 * NEG_INF)
    qkc = [jnp.dot(kc_ref[0, h // hg], q_ref[0, h], preferred_element_type=f32) for h in H]
    lgc = [jnp.where(valid, qkc[h] + bc_ref[h // hg, h % hg], fill) for h in H]
    mc = [jnp.max(lgc[h], axis=0, keepdims=True) for h in H]
    pc = [jnp.exp(lgc[h] - mc[h]) for h in H]
    inv = [1.0 / jnp.sum(pc[h], axis=0, keepdims=True) for h in H]
    pc = [jnp.where(valid, pc[h] * inv[h], 0.0) for h in H]
    psum = [pc[g * hg] for g in G]
    for h in H:
        if h % hg:
            psum[h // hg] = psum[h // hg] + pc[h]
    for h in H:
        ocmp_ref[h] = jnp.dot(vc_ref[0, h // hg], pc[h].astype(bf16), preferred_element_type=f32)

    jblk = lax.broadcasted_iota(jnp.int32, (ns, qb), 0)
    cur = (lax.broadcasted_iota(jnp.int32, (ns, qb), 1) + i * qb) // SEL_BLOCK
    forced = (jblk == 0) | (jblk == cur) | (jblk == cur - 1)
    score = [jnp.dot(c2s_ref[...], psum[g].astype(bf16), preferred_element_type=f32) for g in G]
    score = [jnp.where(forced, FORCED_SCORE, jnp.where(jblk <= cur, score[g], -1.0)) for g in G]
    cnt = [jnp.zeros((ns, qb), f32) for g in G]
    for jp in range(ns):
        tie = jnp.where(jblk > jp, 1.0, 0.0)
        for g in G:
            row = score[g][jp:jp + 1, :]
            cnt[g] = cnt[g] + jnp.where(row > score[g], 1.0, jnp.where(row == score[g], tie, 0.0))
    maskneg = [jnp.where(cnt[g] < n_sel, 0.0, NEG_INF).astype(bf16) for g in G]
    q_aug = [jnp.concatenate([q_ref[0, h], maskneg[h // hg]], axis=0) for h in H]

    kt = SEL_KEY_TILE
    per = kt // qb
    last_tile = i // per
    acc_s[...] = jnp.zeros_like(acc_s)

    def sel_step(jj, carry):
        ms, ls = carry
        ks_, vs_, tiles = [], [], []
        for u in range(2):
            j = 2 * jj + u
            off = pl.multiple_of(jnp.minimum(j, last_tile) * kt, kt)
            oh = oh_ref[:, pl.ds(off, kt)]
            ks_.append([jnp.concatenate([ks_ref[0, g, :, pl.ds(off, kt)], oh], axis=0) for g in G])
            vs_.append([vs_ref[0, g, :, pl.ds(off, kt)] for g in G])
            tiles.append([jnp.clip(i - (j * per + w), -1, n_delta) + 1 for w in range(per)])
        qk = [[lax.dot_general(ks_[u][h // hg], q_aug[h], _TN, preferred_element_type=f32) for u in range(2)]
              for h in H]
        lgs = [[qk[h][u] + jnp.concatenate([ts_ref[h // hg, h % hg, tiles[u][w]] for w in range(per)], axis=0)
                for u in range(2)] for h in H]
        return _flash_update(lgs, [[vs_[u][h // hg] for u in range(2)] for h in H], ms, ls, acc_s)

    init = (tuple(jnp.full((1, qb), NEG_INF, f32) for _ in H), tuple(jnp.zeros((1, qb), f32) for _ in H))
    _, ls_s = lax.fori_loop(0, last_tile // 2 + 1, sel_step, init)

    acc_w[...] = jnp.zeros_like(acc_w)
    kws, vws, widx = [], [], []
    for u in range(N_BAND):
        j = i - (N_BAND - 1) + u
        off = pl.multiple_of(jnp.maximum(j, 0) * qb, qb)
        kws.append([kw_ref[0, g, :, pl.ds(off, qb)] for g in G])
        vws.append([vw_ref[0, g, :, pl.ds(off, qb)] for g in G])
        widx.append(jnp.where(j >= 0, N_BAND - 1 - u, N_BAND))
    qk = [[lax.dot_general(kws[u][h // hg], q_ref[0, h], _TN, preferred_element_type=f32) for u in range(N_BAND)]
          for h in H]
    lgs = [[qk[h][u] + tw_ref[h // hg, h % hg, widx[u]] for u in range(N_BAND)] for h in H]
    _, ls_w = _flash_update(lgs, [[vws[u][h // hg] for u in range(N_BAND)] for h in H], init[0], init[1], acc_w)

    gates = jax.nn.sigmoid(gl_ref[0])
    for h in H:
        r0 = (h // hg) * NSA_GATE_ROWS + 3 * (h % hg)
        o_ref[0, h] = (gates[r0:r0 + 1] * ocmp_ref[h]
                       + gates[r0 + 1:r0 + 2] * (acc_s[h] / ls_s[h])
                       + gates[r0 + 2:r0 + 3] * (acc_w[h] / ls_w[h]))


def _nsa_fused(zt, gates_t, kc, vc_t, bias_cmp_t, sel_tiles, win_tiles, c2s_t, onehot_t, n_sel, n_cmp):
    B, _, S = zt.shape
    G, Hg, Dh = N_KV_GROUPS, HEADS_PER_GROUP, HEAD_NSA
    NCP = kc.shape[2]
    NS = c2s_t.shape[0]
    QB = Q_BLOCK
    ND = sel_tiles.shape[2] - 2
    assert S % SEL_KEY_TILE == 0 and SEL_KEY_TILE % QB == 0
    z4 = zt.reshape(B, -1, Dh, S)
    stream = lambda n: pl.BlockSpec((1, G, Dh, S), lambda b, i: (b, Hg + n, 0, 0))
    bg = lambda *blk: pl.BlockSpec((1, G) + blk, lambda b, i: (b, 0) + (0,) * len(blk))
    gt = lambda arr: pl.BlockSpec(arr.shape, lambda b, i: (0,) * arr.ndim)
    return pl.pallas_call(
        functools.partial(_nsa_body, float(n_sel), n_cmp, ND),
        grid=(B, S // QB),
        in_specs=[pl.BlockSpec((1, G * Hg, Dh, QB), lambda b, i: (b, 0, 0, i)),
                  bg(NCP, Dh), bg(Dh, NCP), stream(0), stream(1), stream(2), stream(3),
                  pl.BlockSpec((1, G * NSA_GATE_ROWS, QB), lambda b, i: (b, 0, i)),
                  pl.BlockSpec((G, Hg, NCP, QB), lambda b, i: (0, 0, 0, i)),
                  gt(sel_tiles), gt(win_tiles),
                  pl.BlockSpec(c2s_t.shape, lambda b, i: (0, 0)),
                  pl.BlockSpec(onehot_t.shape, lambda b, i: (0, 0))],
        out_specs=pl.BlockSpec((1, G * Hg, Dh, QB), lambda b, i: (b, 0, 0, i)),
        out_shape=jax.ShapeDtypeStruct((B, G * Hg, Dh, S), jnp.float32),
        scratch_shapes=[pltpu.VMEM((G * Hg, Dh, QB), jnp.float32)] * 3,
        compiler_params=pltpu.CompilerParams(dimension_semantics=("arbitrary",) * 2,
                                             vmem_limit_bytes=VMEM_LIMIT),
        name="nsa_fused",
    )(z4, kc, vc_t, z4, z4, z4, z4, gates_t, bias_cmp_t, sel_tiles, win_tiles, c2s_t, onehot_t)


def _nsa_tables(rel_bias, seq):
    n_cmp = seq // CMP_STRIDE - CMP_BLOCK // CMP_STRIDE + 1
    ncp = -(-n_cmp // 128) * 128
    c = jnp.arange(ncp)[:, None]
    t = jnp.arange(seq)[None, :]
    bias_cmp_t = _bias_from_buckets(rel_bias, _t5_bucket(t - (c * CMP_STRIDE + CMP_BLOCK - 1)))
    base = _sel_bias_tiles(rel_bias, seq)
    sel_tiles = jnp.concatenate([jnp.full_like(base[:, :, :1], NEG_INF), base], axis=2)
    kj = np.arange(Q_BLOCK)[:, None]
    qi = np.arange(Q_BLOCK)[None, :]
    dwin = np.arange(N_BAND)[:, None, None] * Q_BLOCK + qi - kj
    nwin = min(N_BAND, base.shape[2])
    win_tiles = jnp.where(jnp.asarray(dwin[:nwin] < WINDOW), base[:, :, :nwin], NEG_INF)
    win_tiles = jnp.concatenate(
        [win_tiles, jnp.full(win_tiles.shape[:2] + (N_BAND + 1 - nwin,) + win_tiles.shape[3:], NEG_INF)], axis=2)
    n_sel_blocks = seq // SEL_BLOCK
    cs = np.arange(ncp)[None, :] * CMP_STRIDE
    ss = np.arange(n_sel_blocks)[:, None] * SEL_BLOCK
    overlap = np.clip(np.minimum(cs + CMP_BLOCK, ss + SEL_BLOCK) - np.maximum(cs, ss), 0, None)
    overlap = np.where(np.arange(ncp)[None, :] < n_cmp, overlap, 0)
    c2s_t = jnp.asarray(overlap.astype(np.float32) / CMP_BLOCK, jnp.bfloat16)
    onehot_t = jnp.asarray(np.arange(n_sel_blocks)[:, None] == np.arange(seq)[None, :] // SEL_BLOCK, jnp.bfloat16)
    return bias_cmp_t, sel_tiles, win_tiles, c2s_t, onehot_t, n_cmp


RWKV_CHUNK = 64
RWKV_INV_BASE = 8


def _split_bf16(x):
    hi = x.astype(jnp.bfloat16)
    lo = (x - hi.astype(jnp.float32)).astype(jnp.bfloat16)
    return hi, lo


def _dot3(a, b, dims=(((1,), (0,)), ((), ()))):
    a_hi, a_lo = _split_bf16(a)
    b_hi, b_lo = _split_bf16(b)
    d = functools.partial(lax.dot_general, dimension_numbers=dims, preferred_element_type=jnp.float32)
    free_axis = 1 - dims[0][0][0]
    m = a.shape[free_axis]
    both = d(jnp.concatenate([a_hi, a_lo], axis=free_axis), b_hi)
    return both[:m] + (both[m:] + d(a_hi, b_lo))


_NT = (((1,), (1,)), ((), ()))
_TN = (((0,), (0,)), ((), ()))


def _rwkv_chunk(r_w, lw, k_w, v_w, kk, lr, st_ref, n_heads):
    C = r_w.shape[0]
    N = r_w.shape[1] // n_heads
    H = range(n_heads)
    f32 = jnp.float32
    row = lax.broadcasted_iota(jnp.int32, (C, C), 0)
    col = lax.broadcasted_iota(jnp.int32, (C, C), 1)
    strict = col < row
    incl = col <= row
    eye = jnp.where(row == col, 1.0, 0.0).astype(f32)
    tri = jnp.where(incl, 1.0, 0.0).astype(jnp.bfloat16)

    cl = _dot_exact_lhs(tri, lw)

    base = RWKV_INV_BASE
    diag_blk = strict & ((row // base) == (col // base))
    level_masks = []
    s = base
    while s < C:
        level_masks.append(strict & ((row // (2 * s)) == (col // (2 * s))) & ((row // s) != (col // s)))
        s *= 2

    cl_end = cl[C - 1:C, :]
    bb = kk * lr
    g_inv = jnp.exp(-cl)
    g_rem = jnp.exp(cl_end - cl)
    g_end = jnp.exp(cl_end)
    abar_w = -kk * jnp.exp(cl - lw)
    rbar_w = r_w * jnp.exp(cl)
    bbar_w = bb * g_inv
    kbar_w = k_w * g_inv
    bhat_w = bb * g_rem
    khat_w = k_w * g_rem
    hs = lambda x, h: x[:, h * N:(h + 1) * N]

    abar = [hs(abar_w, h) for h in H]
    rbar = [hs(rbar_w, h) for h in H]
    v = [hs(v_w, h) for h in H]
    gmat = [_dot3(jnp.concatenate([abar[h], rbar[h]], axis=0),
                  jnp.concatenate([hs(bbar_w, h), hs(kbar_w, h)], axis=0), _NT) for h in H]
    a_ab = [jnp.where(strict, gmat[h][:C, :C], 0.0) for h in H]
    a_ak = [jnp.where(strict, gmat[h][:C, C:], 0.0) for h in H]
    m_rb = [jnp.where(incl, gmat[h][C:, :C], 0.0) for h in H]
    m_rk = [jnp.where(incl, gmat[h][C:, C:], 0.0) for h in H]
    akv = [_dot3(a_ak[h], v[h]) for h in H]
    dp = [jnp.where(diag_blk, a_ab[h], 0.0) for h in H]
    x = [eye + dp[h] for h in H]
    s = 2
    while s < base:
        dp = [_dot3(dp[h], dp[h]) for h in H]
        x = [x[h] + _dot3(x[h], dp[h]) for h in H]
        s *= 2
    for lm in level_masks:
        t = [_dot3(jnp.where(lm, a_ab[h], 0.0), x[h]) for h in H]
        x = [x[h] + _dot3(x[h], t[h]) for h in H]
    xw = [_dot3(x[h], jnp.concatenate([akv[h], abar[h]], axis=1)) for h in H]
    uv = [jnp.concatenate([xw[h][:, :N], v[h]], axis=0) for h in H]
    atil = [xw[h][:, N:] for h in H]
    y_loc = [_dot3(jnp.concatenate([m_rb[h], m_rk[h]], axis=1), uv[h]) for h in H]
    qm = [rbar[h] + _dot3(m_rb[h], atil[h]) for h in H]
    s_loc = [_dot3(jnp.concatenate([hs(bhat_w, h), hs(khat_w, h)], axis=0), uv[h], _TN) for h in H]
    pm = [eye[:N, :N] * hs(g_end, h) + _dot3(hs(bhat_w, h), atil[h], _TN) for h in H]
    s0 = [st_ref[h] for h in H]
    y = [y_loc[h] + _dot3(qm[h], s0[h]) for h in H]
    for h in H:
        st_ref[h] = s_loc[h] + _dot3(pm[h], s0[h])
    return jnp.concatenate(y, axis=1)


def _bf16_pieces(x):
    f32 = jnp.float32
    p1 = x.astype(jnp.bfloat16)
    r1 = x - p1.astype(f32)
    p2 = r1.astype(jnp.bfloat16)
    p3 = (r1 - p2.astype(f32)).astype(jnp.bfloat16)
    return p1, p2, p3


def _dot_exact_lhs(a_bf16, x):
    dd = functools.partial(jnp.dot, preferred_element_type=jnp.float32)
    p1, p2, p3 = _bf16_pieces(x)
    return dd(a_bf16, p1) + (dd(a_bf16, p2) + dd(a_bf16, p3))


def _dot_exact_rhs(x, b_bf16):
    dd = functools.partial(jnp.dot, preferred_element_type=jnp.float32)
    p1, p2, p3 = _bf16_pieces(x)
    return dd(p1, b_bf16) + (dd(p2, b_bf16) + dd(p3, b_bf16))


RWKV_BATCH_TILE = 2


def _rwkv_body(n_heads, z_ref, mu_ref, w0_ref, w2_ref, a0_ref, a2_ref, g2_ref, kk_ref, ka_ref, rk_ref,
               lng_ref, lnb_ref, y_ref, st_ref, prev_ref):
    c = pl.program_id(1)
    NB, C, _ = z_ref.shape
    W = y_ref.shape[2]
    N = W // n_heads
    f32, bf16 = jnp.float32, jnp.bfloat16
    dd = functools.partial(jnp.dot, preferred_element_type=f32)

    @pl.when(c == 0)
    def _():
        st_ref[...] = jnp.zeros_like(st_ref)
        prev_ref[...] = jnp.zeros_like(prev_ref)

    hrow = lax.broadcasted_iota(jnp.int32, (W, W), 0) // N
    hcol = lax.broadcasted_iota(jnp.int32, (W, W), 1) // N
    seg = jnp.where(hrow == hcol, 1.0, 0.0).astype(bf16)
    o = np.cumsum((0,) + SHIFT_SPLITS)
    ops = []
    for b in range(NB):
        z = z_ref[b]
        z_prev = jnp.concatenate([prev_ref[b], z[:C - 1]], axis=0)
        prev_ref[b] = z[C - 1:C]
        zs = z + (z_prev - z) * mu_ref[...]
        r, k, v, wl, al, gl = (zs[:, o[i]:o[i + 1]] for i in range(6))
        yw = w0_ref[...] + dd(jnp.tanh(wl).astype(bf16), w2_ref[...])
        logw = -(jnp.maximum(-yw, 0.0) + jnp.log1p(jnp.exp(-jnp.abs(yw)))) - 0.5
        lw = -jnp.exp(logw)
        lr = jax.nn.sigmoid(a0_ref[...] + dd(al.astype(bf16), a2_ref[...]))
        g = dd(jax.nn.sigmoid(gl).astype(bf16), g2_ref[...])
        kk = k * kk_ref[...]
        kk = kk / jnp.maximum(jnp.sqrt(_dot_exact_rhs(kk * kk, seg)), 1e-12)
        k = k * (1.0 + (lr - 1.0) * ka_ref[...])
        ops.append((r, lw, k, v, kk, lr, g))
    side = lambda idx: jnp.concatenate([op[idx] for op in ops], axis=1)
    y_all = _rwkv_chunk(side(0), side(1), side(2), side(3), side(4), side(5), st_ref, NB * n_heads)
    for b, (r, lw, k, v, kk, lr, g) in enumerate(ops):
        y = y_all[:, b * W:(b + 1) * W]
        mu = _dot_exact_rhs(y, seg) * (1.0 / N)
        yc = y - mu
        var = _dot_exact_rhs(yc * yc, seg) * (1.0 / N)
        yn = yc * lax.rsqrt(var + RWKV_GN_EPS) * lng_ref[...] + lnb_ref[...]
        bonus = _dot_exact_rhs(r * k * rk_ref[...], seg) * v
        y_ref[b] = (yn + bonus) * g


def _rwkv_fused(z, batch, seq, shift_mu, w0, w2, a0, a2, g2, k_k, k_a, r_k, ln_g, ln_b):
    H, N, W = N_HEADS_RWKV, HEAD_RWKV, RWKV_WIDTH
    C = min(RWKV_CHUNK, seq)
    assert N <= C and seq % C == 0 and SHIFT_WIDTH % 128 == 0
    nb = RWKV_BATCH_TILE if batch % RWKV_BATCH_TILE == 0 else 1
    bf16 = jnp.bfloat16
    row = lambda a: a.reshape(1, -1).astype(jnp.float32)
    params = [row(shift_mu), row(w0), w2.astype(bf16), row(a0), a2.astype(bf16), g2.astype(bf16),
              row(k_k), row(k_a), row(r_k), row(ln_g), row(ln_b)]
    full = lambda a: pl.BlockSpec(a.shape, lambda b, c: (0, 0))
    y = pl.pallas_call(
        functools.partial(_rwkv_body, H),
        grid=(batch // nb, seq // C),
        in_specs=[pl.BlockSpec((nb, C, SHIFT_WIDTH), lambda b, c: (b, c, 0))] + [full(p) for p in params],
        out_specs=pl.BlockSpec((nb, C, W), lambda b, c: (b, c, 0)),
        out_shape=jax.ShapeDtypeStruct((batch, seq, W), jnp.float32),
        scratch_shapes=[pltpu.VMEM((nb * H, N, N), jnp.float32), pltpu.VMEM((nb, 1, SHIFT_WIDTH), jnp.float32)],
        compiler_params=pltpu.CompilerParams(dimension_semantics=("arbitrary", "arbitrary"),
                                             vmem_limit_bytes=VMEM_LIMIT),
        name="rwkv_fused",
    )(z.reshape(batch, seq, -1), *params)
    return y.reshape(batch * seq, W)


def _moe_body(be_ref, nu_ref, x_ref, w1_ref, w3_ref, w2_ref, o_ref):
    bf16 = jnp.bfloat16
    used = pl.program_id(0) < nu_ref[0]

    @pl.when(used)
    def _():
        x = x_ref[...].astype(bf16)
        h1 = jnp.dot(x, w1_ref[0, 0].astype(bf16), preferred_element_type=jnp.float32)
        h3 = jnp.dot(x, w3_ref[0, 0].astype(bf16), preferred_element_type=jnp.float32)
        h = (h1 * jax.nn.sigmoid(h1)) * h3
        o_ref[...] = jnp.dot(h.astype(bf16), w2_ref[0, 0].astype(bf16), preferred_element_type=jnp.float32)

    @pl.when(jnp.logical_not(used))
    def _():
        o_ref[...] = jnp.zeros_like(o_ref)


def _moe_experts(xs, blk_exp, n_used, w1, w3, w2, layer, blk):
    n_rows, D = xs.shape
    De = w1.shape[3]
    grid_spec = pltpu.PrefetchScalarGridSpec(
        num_scalar_prefetch=2,
        grid=(n_rows // blk,),
        in_specs=[pl.BlockSpec((blk, D), lambda i, be, nu: (i, 0)),
                  pl.BlockSpec((1, 1, D, De), lambda i, be, nu: (layer, be[i], 0, 0)),
                  pl.BlockSpec((1, 1, D, De), lambda i, be, nu: (layer, be[i], 0, 0)),
                  pl.BlockSpec((1, 1, De, D), lambda i, be, nu: (layer, be[i], 0, 0))],
        out_specs=pl.BlockSpec((blk, D), lambda i, be, nu: (i, 0)))
    return pl.pallas_call(
        _moe_body,
        grid_spec=grid_spec,
        out_shape=jax.ShapeDtypeStruct((n_rows, D), jnp.float32),
        compiler_params=pltpu.CompilerParams(dimension_semantics=("arbitrary",),
                                             vmem_limit_bytes=VMEM_LIMIT),
        name="moe_experts",
    )(blk_exp, n_used, xs, w1, w3, w2)


MOE_DEST_CHUNK = 512


def _moe_dest_body(n_exp, blk, eid_ref, dest_ref, cnt_ref, run_ref, pst_ref):
    ph, c = pl.program_id(0), pl.program_id(1)
    T = eid_ref.shape[1]
    f32, bf16 = jnp.float32, jnp.bfloat16
    onehot = jnp.where(lax.broadcasted_iota(jnp.int32, (n_exp, T), 0) == eid_ref[...], 1.0, 0.0)
    here = jnp.sum(onehot, axis=1, keepdims=True)

    @pl.when((ph == 0) & (c == 0))
    def _():
        run_ref[...] = jnp.zeros_like(run_ref)

    @pl.when(ph == 0)
    def _():
        run_ref[...] = run_ref[...] + here
        dest_ref[...] = jnp.zeros_like(dest_ref)

    @pl.when((ph == 1) & (c == 0))
    def _():
        counts = run_ref[...]
        cnt_ref[...] = counts
        padded = jnp.floor((counts + (blk - 1)) * (1.0 / blk)) * blk
        er = lax.broadcasted_iota(jnp.int32, (n_exp, n_exp), 0)
        ec = lax.broadcasted_iota(jnp.int32, (n_exp, n_exp), 1)
        pst_ref[...] = _dot_exact_lhs(jnp.where(ec < er, 1.0, 0.0).astype(bf16), padded)
        run_ref[...] = jnp.zeros_like(run_ref)

    @pl.when(ph == 1)
    def _():
        sr = lax.broadcasted_iota(jnp.int32, (T, T), 0)
        sc = lax.broadcasted_iota(jnp.int32, (T, T), 1)
        earlier = jnp.dot(onehot.astype(bf16), jnp.where(sr < sc, 1.0, 0.0).astype(bf16),
                          preferred_element_type=f32)
        base = pst_ref[:, 0:1] + run_ref[:, 0:1]
        dest = jnp.sum(onehot * (earlier + base), axis=0, keepdims=True)
        dest_ref[...] = dest.astype(jnp.int32)
        run_ref[...] = run_ref[...] + here


def _moe_dest(eid_row, n_exp, blk):
    n_slots = eid_row.shape[1]
    T = min(MOE_DEST_CHUNK, n_slots)
    assert n_slots % T == 0 and n_slots + n_exp * blk < 2 ** 24
    dest, cnt = pl.pallas_call(
        functools.partial(_moe_dest_body, n_exp, blk),
        grid=(2, n_slots // T),
        in_specs=[pl.BlockSpec((1, T), lambda p, c: (0, c))],
        out_specs=[pl.BlockSpec((1, T), lambda p, c: (0, c * p)),
                   pl.BlockSpec((n_exp, 128), lambda p, c: (0, 0))],
        out_shape=[jax.ShapeDtypeStruct((1, n_slots), jnp.int32),
                   jax.ShapeDtypeStruct((n_exp, 128), jnp.float32)],
        scratch_shapes=[pltpu.VMEM((n_exp, 128), jnp.float32), pltpu.VMEM((n_exp, 128), jnp.float32)],
        compiler_params=pltpu.CompilerParams(dimension_semantics=("arbitrary", "arbitrary")),
        name="moe_dest",
    )(eid_row)
    return dest, cnt[:, 0].astype(jnp.int32)


MOE_TOKEN_TILE = 512


def _row_copy(src_ref, src_row, dst_ref, dst_row, sem):
    return pltpu.make_async_copy(src_ref.at[pl.ds(src_row, 1)], dst_ref.at[pl.ds(dst_row, 1)], sem)


def _moe_dispatch_body(dest_ref, x_ref, init_ref, xs_ref, sem):
    del init_ref
    n_choice, T = dest_ref.shape

    def start(r, carry):
        for k in range(n_choice):
            _row_copy(x_ref, r, xs_ref, dest_ref[k, r], sem).start(priority=k % 2)
        return carry

    def wait(r, carry):
        for k in range(n_choice):
            _row_copy(x_ref, 0, xs_ref, 0, sem).wait()
        return carry

    lax.fori_loop(0, T, start, 0, unroll=8)
    lax.fori_loop(0, T, wait, 0)


def _moe_dispatch(x, dest2, n_rows):
    N, D = x.shape
    T = min(MOE_TOKEN_TILE, N)
    assert N % T == 0
    return pl.pallas_call(
        _moe_dispatch_body,
        grid=(N // T,),
        in_specs=[pl.BlockSpec((dest2.shape[0], T), lambda i: (0, i), memory_space=pltpu.SMEM),
                  pl.BlockSpec((T, D), lambda i: (i, 0)),
                  pl.BlockSpec(memory_space=pl.ANY)],
        out_specs=pl.BlockSpec(memory_space=pl.ANY),
        out_shape=jax.ShapeDtypeStruct((n_rows, D), x.dtype),
        scratch_shapes=[pltpu.SemaphoreType.DMA(())],
        input_output_aliases={2: 0},
        compiler_params=pltpu.CompilerParams(dimension_semantics=("arbitrary",)),
        name="moe_dispatch",
    )(dest2, x, jnp.zeros((n_rows, D), x.dtype))


def _moe_combine_body(dest_ref, ys_ref, w_ref, x_ref, g_ref, b_ref, o_ref, buf_ref, sem):
    n_choice, T = dest_ref.shape

    def start(r, carry):
        for k in range(n_choice):
            _row_copy(ys_ref, dest_ref[k, r], buf_ref.at[k], r, sem).start(priority=k % 2)
        return carry

    def wait(r, carry):
        for k in range(n_choice):
            _row_copy(ys_ref, 0, buf_ref.at[k], 0, sem).wait()
        return carry

    lax.fori_loop(0, T, start, 0, unroll=8)
    lax.fori_loop(0, T, wait, 0)
    w = w_ref[...]
    h = w[:, 0:1] * buf_ref[0]
    for k in range(1, n_choice):
        h = h + w[:, k:k + 1] * buf_ref[k]
    y = ALPHA * x_ref[...] + h
    mu = jnp.mean(y, axis=-1, keepdims=True)
    var = jnp.mean(jnp.square(y - mu), axis=-1, keepdims=True)
    o_ref[...] = (y - mu) * lax.rsqrt(var + LN_EPS) * g_ref[...] + b_ref[...]


def _moe_combine(ys, dest2, w, x, ln_g, ln_b):
    N, D = x.shape
    n_choice = dest2.shape[0]
    T = min(MOE_TOKEN_TILE, N)
    assert N % T == 0
    ln_g, ln_b = ln_g.reshape(1, D), ln_b.reshape(1, D)
    return pl.pallas_call(
        _moe_combine_body,
        grid=(N // T,),
        in_specs=[pl.BlockSpec((n_choice, T), lambda i: (0, i), memory_space=pltpu.SMEM),
                  pl.BlockSpec(memory_space=pl.ANY),
                  pl.BlockSpec((T, n_choice), lambda i: (i, 0)),
                  pl.BlockSpec((T, D), lambda i: (i, 0)),
                  pl.BlockSpec((1, D), lambda i: (0, 0)),
                  pl.BlockSpec((1, D), lambda i: (0, 0))],
        out_specs=pl.BlockSpec((T, D), lambda i: (i, 0)),
        out_shape=jax.ShapeDtypeStruct((N, D), jnp.float32),
        scratch_shapes=[pltpu.VMEM((n_choice, T, D), jnp.float32), pltpu.SemaphoreType.DMA(())],
        compiler_params=pltpu.CompilerParams(dimension_semantics=("arbitrary",)),
        name="moe_combine",
    )(dest2, ys, w, x, ln_g, ln_b)


def _t5_bucket(dist):
    n = jnp.maximum(dist, 0)
    max_exact = NUM_BUCKETS // 2
    nf = jnp.maximum(n, 1).astype(jnp.float32)
    large = max_exact + (jnp.log(nf / max_exact) / math.log(MAX_DISTANCE / max_exact)
                         * (NUM_BUCKETS - max_exact)).astype(jnp.int32)
    large = jnp.minimum(large, NUM_BUCKETS - 1)
    return jnp.where(n < max_exact, n, large)


def _bias_from_buckets(rel_bias, bucket):
    rb = rel_bias.astype(jnp.float32)
    shape = (N_KV_GROUPS, HEADS_PER_GROUP) + (1,) * bucket.ndim
    out = jnp.zeros((N_KV_GROUPS, HEADS_PER_GROUP) + bucket.shape, jnp.float32)
    for b in range(NUM_BUCKETS):
        out = jnp.where(bucket == b, rb[b].reshape(shape), out)
    return out


def _compress(t, pe, w1, w2):
    B, S, G, Dh = t.shape
    rep = CMP_BLOCK // CMP_STRIDE
    nc = S // CMP_STRIDE - rep + 1
    sub = t.reshape(B, S // CMP_STRIDE, CMP_STRIDE, G, Dh)
    blk = jnp.concatenate([sub[:, j:j + nc] for j in range(rep)], axis=2)
    blk = blk + pe[:, None, :]
    blk = blk.transpose(0, 1, 3, 2, 4).reshape(B, nc, G, CMP_BLOCK * Dh)
    out = jax.nn.gelu(blk @ w1) @ w2
    return out.transpose(0, 2, 1, 3)


def _nsa_attention(zt, gates_t, k_cmp, v_cmp, pe_k, w1_k, w2_k, pe_v, w1_v, w2_v, tables):
    bias_cmp_t, sel_tiles, win_tiles, c2s_t, onehot_t, n_cmp = tables
    B, S, _ = k_cmp.shape
    G, Dh = N_KV_GROUPS, HEAD_NSA
    bf16 = jnp.bfloat16
    ncp = bias_cmp_t.shape[2]
    kc = _compress(k_cmp.reshape(B, S, G, Dh), pe_k, w1_k, w2_k)
    vc = _compress(v_cmp.reshape(B, S, G, Dh), pe_v, w1_v, w2_v)
    pad = ((0, 0), (0, 0), (0, ncp - n_cmp), (0, 0))
    kc_p = jnp.pad(kc, pad).astype(bf16)
    vc_t = jnp.pad(vc, pad).transpose(0, 1, 3, 2).astype(bf16)
    n_sel = min(N_SELECT, S // SEL_BLOCK)
    o_t = _nsa_fused(zt, gates_t, kc_p, vc_t, bias_cmp_t, sel_tiles, win_tiles, c2s_t, onehot_t,
                     n_sel, n_cmp)
    return o_t.reshape(B, NSA_Q_WIDTH, S)


def _split_in_proj(w_in):
    D = w_in.shape[0]
    scale = HEAD_NSA ** -0.5
    assert math.frexp(scale)[0] == 0.5
    o = np.cumsum((SHIFT_WIDTH,) + REST_SPLITS)
    col = lambda a, b: w_in[:, o[a]:o[b]]
    row_part = [w_in[:, :SHIFT_WIDTH], col(1, 3)]
    lead = SHIFT_WIDTH + int(o[3] - o[1])
    lead_pad = (-lead) % D_MODEL
    w_row = jnp.concatenate(row_part + [jnp.zeros((D, lead_pad), w_in.dtype), col(8, 10)], axis=1)
    gate_blk = (lead + lead_pad) // D_MODEL
    assert w_row.shape[1] % IN_PROJ_TN == 0
    G, n_g = N_KV_GROUPS, 3 * HEADS_PER_GROUP
    gates = col(7, 8).reshape(D, G, n_g)
    gates = jnp.pad(gates, ((0, 0), (0, 0), (0, NSA_GATE_ROWS - n_g))).reshape(D, G * NSA_GATE_ROWS)
    w_t = jnp.concatenate([col(0, 1) * scale, col(3, 7), gates], axis=1).T
    return w_row.astype(jnp.bfloat16), w_t.astype(jnp.bfloat16), gate_blk, int(o[7] - o[3]) + NSA_Q_WIDTH


def _token_mixer(x, w_in, shift_mu, rw_w0, rw_w2, rw_a0, rw_a2, rw_g2, rw_kk, rw_ka, rw_rk,
                 rw_ln_g, rw_ln_b, cmp_pe_k, cmp_w1_k, cmp_w2_k, cmp_pe_v, cmp_w1_v, cmp_w2_v,
                 w_up_rwkv, w_up_nsa, w_out, ln_g, ln_b, rel_bias, pos):
    B, S, D = x.shape
    xf = x.reshape(B * S, D)
    w_row, w_t, gate_blk, n_main = _split_in_proj(w_in)
    z = _matmul(xf, w_row, IN_PROJ_TM, IN_PROJ_TN)
    zt, gates_t = _matmul_t(xf, w_t, n_main, B, S)
    z3 = z.reshape(B, S, -1)
    k_cmp = z3[..., SHIFT_WIDTH:SHIFT_WIDTH + NSA_KV_WIDTH]
    v_cmp = z3[..., SHIFT_WIDTH + NSA_KV_WIDTH:SHIFT_WIDTH + 2 * NSA_KV_WIDTH]
    y_rw = _rwkv_fused(z, B, S, shift_mu, rw_w0, rw_w2, rw_a0, rw_a2, rw_g2,
                       rw_kk, rw_ka, rw_rk, rw_ln_g, rw_ln_b)
    y_nsa_t = _nsa_attention(zt, gates_t, k_cmp, v_cmp, cmp_pe_k, cmp_w1_k, cmp_w2_k,
                             cmp_pe_v, cmp_w1_v, cmp_w2_v, pos)
    bf16 = jnp.bfloat16
    return _merge_out(y_rw, y_nsa_t, z, gate_blk, xf,
                      w_up_rwkv.astype(bf16), w_up_nsa.astype(bf16), w_out.astype(bf16),
                      ln_g, ln_b).reshape(B, S, D)


def _hier_moe(x, wg, bg, we, be, w1, w3, w2, layer, ln_g, ln_b):
    B, S, D = x.shape
    N = B * S
    f32 = jnp.float32
    xf = x.reshape(N, D)
    g_prob = jax.nn.softmax((xf @ wg + bg).astype(f32), axis=-1)
    grp = jnp.argmax(g_prob, axis=-1)
    p_grp = jnp.take_along_axis(g_prob, grp[:, None], axis=1)[:, 0]
    e_logits = (xf @ we + be).astype(f32).reshape(N, N_GROUPS, EXPERTS_PER_GROUP)
    e_logits = jnp.take_along_axis(e_logits, grp[:, None, None], axis=1)[:, 0]
    top_v, top_i = lax.top_k(e_logits, TOP_K_INNER)
    top_w = jax.nn.softmax(top_v, axis=-1) * p_grp[:, None]
    eid = (grp[:, None] * EXPERTS_PER_GROUP + top_i).astype(jnp.int32)
    n_slots = N * TOP_K_INNER
    n_rows = n_slots + N_EXPERTS * MOE_BLOCK
    n_blk = n_rows // MOE_BLOCK
    dest, counts = _moe_dest(eid.T.reshape(1, n_slots), N_EXPERTS, MOE_BLOCK)
    dest2 = dest.reshape(TOP_K_INNER, N)
    pcounts = (counts + MOE_BLOCK - 1) // MOE_BLOCK * MOE_BLOCK
    pends = jnp.cumsum(pcounts)
    blk_exp = jnp.minimum(jnp.sum(jnp.arange(n_blk)[:, None] * MOE_BLOCK >= pends[None, :], axis=1),
                          N_EXPERTS - 1).astype(jnp.int32)
    xs = _moe_dispatch(xf, dest2, n_rows)
    n_used = (pends[-1:] // MOE_BLOCK).astype(jnp.int32)
    ys = _moe_experts(xs, blk_exp, n_used, w1, w3, w2, layer, MOE_BLOCK)
    return _moe_combine(ys, dest2, top_w, xf, ln_g, ln_b).reshape(B, S, D)


def kernel(x, rel_bias, w_in, shift_mu, rw_w0, rw_w2, rw_a0, rw_a2, rw_g2, rw_kk, rw_ka, rw_rk,
           rw_ln_g, rw_ln_b, cmp_pe_k, cmp_w1_k, cmp_w2_k, cmp_pe_v, cmp_w1_v, cmp_w2_v,
           w_up_rwkv, w_up_nsa, w_out, ln1_g, ln1_b, router_group_w, router_group_b,
           router_expert_w, router_expert_b, exp_w1, exp_w3, exp_w2, ln2_g, ln2_b):
    pos = _nsa_tables(rel_bias, x.shape[1])
    for l in range(DEPTH):
        x = _token_mixer(x, w_in[l], shift_mu[l], rw_w0[l], rw_w2[l], rw_a0[l], rw_a2[l], rw_g2[l],
                         rw_kk[l], rw_ka[l], rw_rk[l], rw_ln_g[l], rw_ln_b[l],
                         cmp_pe_k[l], cmp_w1_k[l], cmp_w2_k[l], cmp_pe_v[l], cmp_w1_v[l], cmp_w2_v[l],
                         w_up_rwkv[l], w_up_nsa[l], w_out[l], ln1_g[l], ln1_b[l], rel_bias, pos)
        x = _hier_moe(x, router_group_w[l], router_group_b[l], router_expert_w[l], router_expert_b[l],
                      exp_w1, exp_w3, exp_w2, l, ln2_g[l], ln2_b[l])
    return x
```

```python
import functools
import math

import jax
import jax.numpy as jnp
import numpy as np
from jax import lax
from jax.experimental import pallas as pl
from jax.experimental.pallas import tpu as pltpu

D_MODEL = 1024
DEPTH = 4
N_HEADS_RWKV = 8
HEAD_RWKV = 64
RWKV_WIDTH = N_HEADS_RWKV * HEAD_RWKV
LORA_W = 64
LORA_A = 64
LORA_G = 128
RWKV_GN_EPS = 64e-5
N_HEADS_NSA = 8
N_KV_GROUPS = 2
HEADS_PER_GROUP = N_HEADS_NSA // N_KV_GROUPS
HEAD_NSA = 64
NSA_Q_WIDTH = N_HEADS_NSA * HEAD_NSA
NSA_KV_WIDTH = N_KV_GROUPS * HEAD_NSA
CMP_BLOCK = 32
CMP_STRIDE = 16
CMP_HIDDEN = 128
SEL_BLOCK = 64
N_SELECT = 16
WINDOW = 512
Q_BLOCK = 128
N_BAND = WINDOW // Q_BLOCK + 1
NEG_INF = -1e30
FORCED_SCORE = 1e4
NUM_BUCKETS = 32
MAX_DISTANCE = 1024
N_GROUPS = 4
EXPERTS_PER_GROUP = 8
N_EXPERTS = N_GROUPS * EXPERTS_PER_GROUP
TOP_K_INNER = 2
D_EXPERT = 512
MOE_BLOCK = 256
ALPHA = (2 * DEPTH) ** 0.25
LN_EPS = 1e-5
SHIFT_SPLITS = (RWKV_WIDTH, RWKV_WIDTH, RWKV_WIDTH, LORA_W, LORA_A, LORA_G)
SHIFT_WIDTH = 3 * RWKV_WIDTH + LORA_W + LORA_A + LORA_G
REST_SPLITS = (NSA_Q_WIDTH,) + (NSA_KV_WIDTH,) * 6 + (3 * N_HEADS_NSA, D_MODEL, D_MODEL)

VMEM_LIMIT = 48 * 1024 * 1024


IN_PROJ_TM = 1024
IN_PROJ_TN = 1024


def _mm_body(x_ref, w_ref, o_ref, xb_ref):
    @pl.when(pl.program_id(1) == 0)
    def _():
        xb_ref[...] = x_ref[...].astype(jnp.bfloat16)

    o_ref[...] = jnp.dot(xb_ref[...], w_ref[...], preferred_element_type=jnp.float32)


def _matmul(x, w, tm, tn):
    m, k = x.shape
    n = w.shape[1]
    assert m % tm == 0 and n % tn == 0
    return pl.pallas_call(
        _mm_body,
        grid=(m // tm, n // tn),
        in_specs=[pl.BlockSpec((tm, k), lambda i, j: (i, 0)),
                  pl.BlockSpec((k, tn), lambda i, j: (0, j))],
        out_specs=pl.BlockSpec((tm, tn), lambda i, j: (i, j)),
        out_shape=jax.ShapeDtypeStruct((m, n), jnp.float32),
        scratch_shapes=[pltpu.VMEM((tm, k), jnp.bfloat16)],
        compiler_params=pltpu.CompilerParams(dimension_semantics=("arbitrary", "arbitrary"),
                                             vmem_limit_bytes=VMEM_LIMIT),
        name="in_proj",
    )(x, w)


IN_PROJ_T_TM = 512
NSA_GATE_ROWS = 16


def _mm_t_body(n_main, x_ref, wt_ref, zt_ref, gt_ref):
    xb = x_ref[...].astype(jnp.bfloat16)
    out = lax.dot_general(wt_ref[...], xb, (((1,), (1,)), ((), ())), preferred_element_type=jnp.float32)
    zt_ref[0] = out[:n_main].astype(zt_ref.dtype)
    gt_ref[0] = out[n_main:]


def _matmul_t(x, wt, n_main, batch, seq):
    m, k = x.shape
    r = wt.shape[0]
    tm = min(IN_PROJ_T_TM, seq)
    assert seq % tm == 0 and m == batch * seq
    per_b = seq // tm
    return pl.pallas_call(
        functools.partial(_mm_t_body, n_main),
        grid=(m // tm,),
        in_specs=[pl.BlockSpec((tm, k), lambda i: (i, 0)),
                  pl.BlockSpec((r, k), lambda i: (0, 0))],
        out_specs=[pl.BlockSpec((1, n_main, tm), lambda i: (i // per_b, 0, i % per_b)),
                   pl.BlockSpec((1, r - n_main, tm), lambda i: (i // per_b, 0, i % per_b))],
        out_shape=[jax.ShapeDtypeStruct((batch, n_main, seq), jnp.bfloat16),
                   jax.ShapeDtypeStruct((batch, r - n_main, seq), jnp.float32)],
        compiler_params=pltpu.CompilerParams(dimension_semantics=("arbitrary",),
                                             vmem_limit_bytes=VMEM_LIMIT),
        name="in_proj_t",
    )(x, wt)


MERGE_TM = 512


def _merge_body(yr_ref, yn_ref, grw_ref, gns_ref, x_ref, wur_ref, wun_ref, wo_ref, g_ref, b_ref, o_ref):
    f32, bf16 = jnp.float32, jnp.bfloat16
    up_r = jnp.dot(yr_ref[...].astype(bf16), wur_ref[...], preferred_element_type=f32)
    up_n = lax.dot_general(yn_ref[0].astype(bf16), wun_ref[...], _TN, preferred_element_type=f32)
    merged = jax.nn.sigmoid(grw_ref[...]) * up_r + jax.nn.sigmoid(gns_ref[...]) * up_n
    y = ALPHA * x_ref[...] + jnp.dot(merged.astype(bf16), wo_ref[...], preferred_element_type=f32)
    mu = jnp.mean(y, axis=-1, keepdims=True)
    var = jnp.mean(jnp.square(y - mu), axis=-1, keepdims=True)
    o_ref[...] = (y - mu) * lax.rsqrt(var + LN_EPS) * g_ref[...] + b_ref[...]


def _merge_out(y_rw, y_nsa_t, z, gate_blk, x, w_up_rwkv, w_up_nsa, w_out, ln_g, ln_b):
    m, d = x.shape
    tm = MERGE_TM
    nb, wn, seq = y_nsa_t.shape
    assert m % tm == 0 and seq % tm == 0
    per_b = seq // tm
    row = lambda width: pl.BlockSpec((tm, width), lambda i: (i, 0))
    full = lambda a: pl.BlockSpec(a.shape, lambda i: (0,) * a.ndim)
    ln_g, ln_b = ln_g.reshape(1, d), ln_b.reshape(1, d)
    return pl.pallas_call(
        _merge_body,
        grid=(m // tm,),
        in_specs=[row(y_rw.shape[1]), pl.BlockSpec((1, wn, tm), lambda i: (i // per_b, 0, i % per_b)),
                  pl.BlockSpec((tm, d), lambda i: (i, gate_blk)),
                  pl.BlockSpec((tm, d), lambda i: (i, gate_blk + 1)),
                  row(d), full(w_up_rwkv), full(w_up_nsa), full(w_out), full(ln_g), full(ln_b)],
        out_specs=row(d),
        out_shape=jax.ShapeDtypeStruct((m, d), jnp.float32),
        compiler_params=pltpu.CompilerParams(dimension_semantics=("arbitrary",),
                                             vmem_limit_bytes=VMEM_LIMIT),
        name="merge_out",
    )(y_rw, y_nsa_t, z, z, x, w_up_rwkv, w_up_nsa, w_out, ln_g, ln_b)


def _sel_bias_tiles(rel_bias, seq):
    nd = -(-MAX_DISTANCE // Q_BLOCK)
    nd = min(nd, seq // Q_BLOCK)
    half = NUM_BUCKETS // 2
    min_far = nd * Q_BLOCK - (Q_BLOCK - 1)
    assert half + math.log(min_far / half) / math.log(MAX_DISTANCE / half) * half >= NUM_BUCKETS - 0.75
    kj = np.arange(Q_BLOCK)[:, None]
    qi = np.arange(Q_BLOCK)[None, :]
    dist = np.arange(nd + 1)[:, None, None] * Q_BLOCK + qi - kj
    dist[nd] = max(seq - 1, nd * Q_BLOCK)
    tiles = _bias_from_buckets(rel_bias, _t5_bucket(jnp.asarray(np.maximum(dist, 0))))
    causal = jnp.asarray(dist >= 0)
    return jnp.where(causal, tiles, NEG_INF)


SEL_KEY_TILE = 256


def _flash_update(lgs, v_ts, ms, ls, acc_ref):
    H = range(len(lgs))
    m_new = []
    for h in H:
        m = ms[h]
        for lg in lgs[h]:
            m = jnp.maximum(m, jnp.max(lg, axis=0, keepdims=True))
        m_new.append(m)
    ps = [[jnp.exp(lg - m_new[h]) for lg in lgs[h]] for h in H]
    alpha = [jnp.exp(ms[h] - m_new[h]) for h in H]
    l_new = []
    for h in H:
        l = alpha[h] * ls[h]
        for p in ps[h]:
            l = l + jnp.sum(p, axis=0, keepdims=True)
        l_new.append(l)
    pv = [[jnp.dot(v_t, p.astype(jnp.bfloat16), preferred_element_type=jnp.float32)
           for p, v_t in zip(ps[h], v_ts[h])] for h in H]
    for h in H:
        acc = alpha[h] * acc_ref[h]
        for o in pv[h]:
            acc = acc + o
        acc_ref[h] = acc
    return tuple(m_new), tuple(l_new)


def _nsa_body(n_sel, n_cmp, n_delta, q_ref, kc_ref, vc_ref, ks_ref, vs_ref, kw_ref, vw_ref, gl_ref,
              bc_ref, ts_ref, tw_ref, c2s_ref, oh_ref, o_ref, acc_s, acc_w, ocmp_ref):
    i = pl.program_id(1)
    hg = HEADS_PER_GROUP
    ng = q_ref.shape[1] // hg
    dh, qb = q_ref.shape[2], q_ref.shape[3]
    ncp = kc_ref.shape[2]
    ns = c2s_ref.shape[0]
    f32, bf16 = jnp.float32, jnp.bfloat16
    H = range(ng * hg)
    G = range(ng)

    cidx = lax.broadcasted_iota(jnp.int32, (ncp, qb), 0)
    tpos = lax.broadcasted_iota(jnp.int32, (ncp, qb), 1) + i * qb
    valid = (tpos - (cidx * CMP_STRIDE + CMP_BLOCK - 1)) >= 0
    fill = jnp.where(cidx < n_cmp, NEG_INF, 2.0 * NEG_INF)
    qkc = [jnp.dot(kc_ref[0, h // hg], q_ref[0, h], preferred_element_type=f32) for h in H]
    lgc = [jnp.where(valid, qkc[h] + bc_ref[h // hg, h % hg], fill) for h in H]
    mc = [jnp.max(lgc[h], axis=0, keepdims=True) for h in H]
    pc = [jnp.exp(lgc[h] - mc[h]) for h in H]
    inv = [1.0 / jnp.sum(pc[h], axis=0, keepdims=True) for h in H]
    pc = [jnp.where(valid, pc[h] * inv[h], 0.0) for h in H]
    psum = [pc[g * hg] for g in G]
    for h in H:
        if h % hg:
            psum[h // hg] = psum[h // hg] + pc[h]
    for h in H:
        ocmp_ref[h] = jnp.dot(vc_ref[0, h // hg], pc[h].astype(bf16), preferred_element_type=f32)

    jblk = lax.broadcasted_iota(jnp.int32, (ns, qb), 0)
    cur = (lax.broadcasted_iota(jnp.int32, (ns, qb), 1) + i * qb) // SEL_BLOCK
    forced = (jblk == 0) | (jblk == cur) | (jblk == cur - 1)
    score = [jnp.dot(c2s_ref[...], psum[g].astype(bf16), preferred_element_type=f32) for g in G]
    score = [jnp.where(forced, FORCED_SCORE, jnp.where(jblk <= cur, score[g], -1.0)) for g in G]
    cnt = [jnp.zeros((ns, qb), f32) for g in G]
    for jp in range(ns):
        tie = jnp.where(jblk > jp, 1.0, 0.0)
        for g in G:
            row = score[g][jp:jp + 1, :]
            cnt[g] = cnt[g] + jnp.where(row > score[g], 1.0, jnp.where(row == score[g], tie, 0.0))
    maskneg = [jnp.where(cnt[g] < n_sel, 0.0, NEG_INF).astype(bf16) for g in G]
    q_aug = [jnp.concatenate([q_ref[0, h], maskneg[h // hg]], axis=0) for h in H]

    kt = SEL_KEY_TILE
    per = kt // qb
    last_tile = i // per
    acc_s[...] = jnp.zeros_like(acc_s)

    def sel_step(jj, carry):
        ms, ls = carry
        ks_, vs_, tiles = [], [], []
        for u in range(2):
            j = 2 * jj + u
            off = pl.multiple_of(jnp.minimum(j, last_tile) * kt, kt)
            oh = oh_ref[:, pl.ds(off, kt)]
            ks_.append([jnp.concatenate([ks_ref[0, g, :, pl.ds(off, kt)], oh], axis=0) for g in G])
            vs_.append([vs_ref[0, g, :, pl.ds(off, kt)] for g in G])
            tiles.append([jnp.clip(i - (j * per + w), -1, n_delta) + 1 for w in range(per)])
        qk = [[lax.dot_general(ks_[u][h // hg], q_aug[h], _TN, preferred_element_type=f32) for u in range(2)]
              for h in H]
        lgs = [[qk[h][u] + jnp.concatenate([ts_ref[h // hg, h % hg, tiles[u][w]] for w in range(per)], axis=0)
                for u in range(2)] for h in H]
        return _flash_update(lgs, [[vs_[u][h // hg] for u in range(2)] for h in H], ms, ls, acc_s)

    init = (tuple(jnp.full((1, qb), NEG_INF, f32) for _ in H), tuple(jnp.zeros((1, qb), f32) for _ in H))
    _, ls_s = lax.fori_loop(0, last_tile // 2 + 1, sel_step, init)

    acc_w[...] = jnp.zeros_like(acc_w)
    kws, vws, widx = [], [], []
    for u in range(N_BAND):
        j = i - (N_BAND - 1) + u
        off = pl.multiple_of(jnp.maximum(j, 0) * qb, qb)
        kws.append([kw_ref[0, g, :, pl.ds(off, qb)] for g in G])
        vws.append([vw_ref[0, g, :, pl.ds(off, qb)] for g in G])
        widx.append(jnp.where(j >= 0, N_BAND - 1 - u, N_BAND))
    qk = [[lax.dot_general(kws[u][h // hg], q_ref[0, h], _TN, preferred_element_type=f32) for u in range(N_BAND)]
          for h in H]
    lgs = [[qk[h][u] + tw_ref[h // hg, h % hg, widx[u]] for u in range(N_BAND)] for h in H]
    _, ls_w = _flash_update(lgs, [[vws[u][h // hg] for u in range(N_BAND)] for h in H], init[0], init[1], acc_w)

    gates = jax.nn.sigmoid(gl_ref[0])
    for h in H:
        r0 = (h // hg) * NSA_GATE_ROWS + 3 * (h % hg)
        o_ref[0, h] = (gates[r0:r0 + 1] * ocmp_ref[h]
                       + gates[r0 + 1:r0 + 2] * (acc_s[h] / ls_s[h])
                       + gates[r0 + 2:r0 + 3] * (acc_w[h] / ls_w[h]))


def _nsa_fused(zt, gates_t, kc, vc_t, bias_cmp_t, sel_tiles, win_tiles, c2s_t, onehot_t, n_sel, n_cmp):
    B, _, S = zt.shape
    G, Hg, Dh = N_KV_GROUPS, HEADS_PER_GROUP, HEAD_NSA
    NCP = kc.shape[2]
    NS = c2s_t.shape[0]
    QB = Q_BLOCK
    ND = sel_tiles.shape[2] - 2
    assert S % SEL_KEY_TILE == 0 and SEL_KEY_TILE % QB == 0
    z4 = zt.reshape(B, -1, Dh, S)
    stream = lambda n: pl.BlockSpec((1, G, Dh, S), lambda b, i: (b, Hg + n, 0, 0))
    bg = lambda *blk: pl.BlockSpec((1, G) + blk, lambda b, i: (b, 0) + (0,) * len(blk))
    gt = lambda arr: pl.BlockSpec(arr.shape, lambda b, i: (0,) * arr.ndim)
    return pl.pallas_call(
        functools.partial(_nsa_body, float(n_sel), n_cmp, ND),
        grid=(B, S // QB),
        in_specs=[pl.BlockSpec((1, G * Hg, Dh, QB), lambda b, i: (b, 0, 0, i)),
                  bg(NCP, Dh), bg(Dh, NCP), stream(0), stream(1), stream(2), stream(3),
                  pl.BlockSpec((1, G * NSA_GATE_ROWS, QB), lambda b, i: (b, 0, i)),
                  pl.BlockSpec((G, Hg, NCP, QB), lambda b, i: (0, 0, 0, i)),
                  gt(sel_tiles), gt(win_tiles),
                  pl.BlockSpec(c2s_t.shape, lambda b, i: (0, 0)),
                  pl.BlockSpec(onehot_t.shape, lambda b, i: (0, 0))],
        out_specs=pl.BlockSpec((1, G * Hg, Dh, QB), lambda b, i: (b, 0, 0, i)),
        out_shape=jax.ShapeDtypeStruct((B, G * Hg, Dh, S), jnp.float32),
        scratch_shapes=[pltpu.VMEM((G * Hg, Dh, QB), jnp.float32)] * 3,
        compiler_params=pltpu.CompilerParams(dimension_semantics=("arbitrary",) * 2,
                                             vmem_limit_bytes=VMEM_LIMIT),
        name="nsa_fused",
    )(z4, kc, vc_t, z4, z4, z4, z4, gates_t, bias_cmp_t, sel_tiles, win_tiles, c2s_t, onehot_t)


def _nsa_tables(rel_bias, seq):
    n_cmp = seq // CMP_STRIDE - CMP_BLOCK // CMP_STRIDE + 1
    ncp = -(-n_cmp // 128) * 128
    c = jnp.arange(ncp)[:, None]
    t = jnp.arange(seq)[None, :]
    bias_cmp_t = _bias_from_buckets(rel_bias, _t5_bucket(t - (c * CMP_STRIDE + CMP_BLOCK - 1)))
    base = _sel_bias_tiles(rel_bias, seq)
    sel_tiles = jnp.concatenate([jnp.full_like(base[:, :, :1], NEG_INF), base], axis=2)
    kj = np.arange(Q_BLOCK)[:, None]
    qi = np.arange(Q_BLOCK)[None, :]
    dwin = np.arange(N_BAND)[:, None, None] * Q_BLOCK + qi - kj
    nwin = min(N_BAND, base.shape[2])
    win_tiles = jnp.where(jnp.asarray(dwin[:nwin] < WINDOW), base[:, :, :nwin], NEG_INF)
    win_tiles = jnp.concatenate(
        [win_tiles, jnp.full(win_tiles.shape[:2] + (N_BAND + 1 - nwin,) + win_tiles.shape[3:], NEG_INF)], axis=2)
    n_sel_blocks = seq // SEL_BLOCK
    cs = np.arange(ncp)[None, :] * CMP_STRIDE
    ss = np.arange(n_sel_blocks)[:, None] * SEL_BLOCK
    overlap = np.clip(np.minimum(cs + CMP_BLOCK, ss + SEL_BLOCK) - np.maximum(cs, ss), 0, None)
    overlap = np.where(np.arange(ncp)[None, :] < n_cmp, overlap, 0)
    c2s_t = jnp.asarray(overlap.astype(np.float32) / CMP_BLOCK, jnp.bfloat16)
    onehot_t = jnp.asarray(np.arange(n_sel_blocks)[:, None] == np.arange(seq)[None, :] // SEL_BLOCK, jnp.bfloat16)
    return bias_cmp_t, sel_tiles, win_tiles, c2s_t, onehot_t, n_cmp


RWKV_CHUNK = 64
RWKV_INV_BASE = 8


def _split_bf16(x):
    hi = x.astype(jnp.bfloat16)
    lo = (x - hi.astype(jnp.float32)).astype(jnp.bfloat16)
    return hi, lo


def _dot3(a, b, dims=(((1,), (0,)), ((), ()))):
    a_hi, a_lo = _split_bf16(a)
    b_hi, b_lo = _split_bf16(b)
    d = functools.partial(lax.dot_general, dimension_numbers=dims, preferred_element_type=jnp.float32)
    free_axis = 1 - dims[0][0][0]
    m = a.shape[free_axis]
    both = d(jnp.concatenate([a_hi, a_lo], axis=free_axis), b_hi)
    return both[:m] + (both[m:] + d(a_hi, b_lo))


_NT = (((1,), (1,)), ((), ()))
_TN = (((0,), (0,)), ((), ()))


def _rwkv_chunk(r_w, lw, k_w, v_w, kk, lr, st_ref, n_heads):
    C = r_w.shape[0]
    N = r_w.shape[1] // n_heads
    H = range(n_heads)
    f32 = jnp.float32
    row = lax.broadcasted_iota(jnp.int32, (C, C), 0)
    col = lax.broadcasted_iota(jnp.int32, (C, C), 1)
    strict = col < row
    incl = col <= row
    eye = jnp.where(row == col, 1.0, 0.0).astype(f32)
    tri = jnp.where(incl, 1.0, 0.0).astype(jnp.bfloat16)

    cl = _dot_exact_lhs(tri, lw)

    base = RWKV_INV_BASE
    diag_blk = strict & ((row // base) == (col // base))
    level_masks = []
    s = base
    while s < C:
        level_masks.append(strict & ((row // (2 * s)) == (col // (2 * s))) & ((row // s) != (col // s)))
        s *= 2

    cl_end = cl[C - 1:C, :]
    bb = kk * lr
    g_inv = jnp.exp(-cl)
    g_rem = jnp.exp(cl_end - cl)
    g_end = jnp.exp(cl_end)
    abar_w = -kk * jnp.exp(cl - lw)
    rbar_w = r_w * jnp.exp(cl)
    bbar_w = bb * g_inv
    kbar_w = k_w * g_inv
    bhat_w = bb * g_rem
    khat_w = k_w * g_rem
    hs = lambda x, h: x[:, h * N:(h + 1) * N]

    abar = [hs(abar_w, h) for h in H]
    rbar = [hs(rbar_w, h) for h in H]
    v = [hs(v_w, h) for h in H]
    gmat = [_dot3(jnp.concatenate([abar[h], rbar[h]], axis=0),
                  jnp.concatenate([hs(bbar_w, h), hs(kbar_w, h)], axis=0), _NT) for h in H]
    a_ab = [jnp.where(strict, gmat[h][:C, :C], 0.0) for h in H]
    a_ak = [jnp.where(strict, gmat[h][:C, C:], 0.0) for h in H]
    m_rb = [jnp.where(incl, gmat[h][C:, :C], 0.0) for h in H]
    m_rk = [jnp.where(incl, gmat[h][C:, C:], 0.0) for h in H]
    akv = [_dot3(a_ak[h], v[h]) for h in H]
    dp = [jnp.where(diag_blk, a_ab[h], 0.0) for h in H]
    x = [eye + dp[h] for h in H]
    s = 2
    while s < base:
        dp = [_dot3(dp[h], dp[h]) for h in H]
        x = [x[h] + _dot3(x[h], dp[h]) for h in H]
        s *= 2
    for lm in level_masks:
        t = [_dot3(jnp.where(lm, a_ab[h], 0.0), x[h]) for h in H]
        x = [x[h] + _dot3(x[h], t[h]) for h in H]
    xw = [_dot3(x[h], jnp.concatenate([akv[h], abar[h]], axis=1)) for h in H]
    uv = [jnp.concatenate([xw[h][:, :N], v[h]], axis=0) for h in H]
    atil = [xw[h][:, N:] for h in H]
    y_loc = [_dot3(jnp.concatenate([m_rb[h], m_rk[h]], axis=1), uv[h]) for h in H]
    qm = [rbar[h] + _dot3(m_rb[h], atil[h]) for h in H]
    s_loc = [_dot3(jnp.concatenate([hs(bhat_w, h), hs(khat_w, h)], axis=0), uv[h], _TN) for h in H]
    pm = [eye[:N, :N] * hs(g_end, h) + _dot3(hs(bhat_w, h), atil[h], _TN) for h in H]
    s0 = [st_ref[h] for h in H]
    y = [y_loc[h] + _dot3(qm[h], s0[h]) for h in H]
    for h in H:
        st_ref[h] = s_loc[h] + _dot3(pm[h], s0[h])
    return jnp.concatenate(y, axis=1)


def _bf16_pieces(x):
    f32 = jnp.float32
    p1 = x.astype(jnp.bfloat16)
    r1 = x - p1.astype(f32)
    p2 = r1.astype(jnp.bfloat16)
    p3 = (r1 - p2.astype(f32)).astype(jnp.bfloat16)
    return p1, p2, p3


def _dot_exact_lhs(a_bf16, x):
    dd = functools.partial(jnp.dot, preferred_element_type=jnp.float32)
    p1, p2, p3 = _bf16_pieces(x)
    return dd(a_bf16, p1) + (dd(a_bf16, p2) + dd(a_bf16, p3))


def _dot_exact_rhs(x, b_bf16):
    dd = functools.partial(jnp.dot, preferred_element_type=jnp.float32)
    p1, p2, p3 = _bf16_pieces(x)
    return dd(p1, b_bf16) + (dd(p2, b_bf16) + dd(p3, b_bf16))


RWKV_BATCH_TILE = 4


def _rwkv_body(n_heads, z_ref, mu_ref, w0_ref, w2_ref, a0_ref, a2_ref, g2_ref, kk_ref, ka_ref, rk_ref,
               lng_ref, lnb_ref, y_ref, st_ref, prev_ref):
    c = pl.program_id(1)
    NB, C, _ = z_ref.shape
    W = y_ref.shape[2]
    N = W // n_heads
    f32, bf16 = jnp.float32, jnp.bfloat16
    dd = functools.partial(jnp.dot, preferred_element_type=f32)

    @pl.when(c == 0)
    def _():
        st_ref[...] = jnp.zeros_like(st_ref)
        prev_ref[...] = jnp.zeros_like(prev_ref)

    hrow = lax.broadcasted_iota(jnp.int32, (W, W), 0) // N
    hcol = lax.broadcasted_iota(jnp.int32, (W, W), 1) // N
    seg = jnp.where(hrow == hcol, 1.0, 0.0).astype(bf16)
    o = np.cumsum((0,) + SHIFT_SPLITS)
    ops = []
    for b in range(NB):
        z = z_ref[b]
        z_prev = jnp.concatenate([prev_ref[b], z[:C - 1]], axis=0)
        prev_ref[b] = z[C - 1:C]
        zs = z + (z_prev - z) * mu_ref[...]
        r, k, v, wl, al, gl = (zs[:, o[i]:o[i + 1]] for i in range(6))
        yw = w0_ref[...] + dd(jnp.tanh(wl).astype(bf16), w2_ref[...])
        logw = -(jnp.maximum(-yw, 0.0) + jnp.log1p(jnp.exp(-jnp.abs(yw)))) - 0.5
        lw = -jnp.exp(logw)
        lr = jax.nn.sigmoid(a0_ref[...] + dd(al.astype(bf16), a2_ref[...]))
        g = dd(jax.nn.sigmoid(gl).astype(bf16), g2_ref[...])
        kk = k * kk_ref[...]
        kk = kk / jnp.maximum(jnp.sqrt(_dot_exact_rhs(kk * kk, seg)), 1e-12)
        k = k * (1.0 + (lr - 1.0) * ka_ref[...])
        ops.append((r, lw, k, v, kk, lr, g))
    side = lambda idx: jnp.concatenate([op[idx] for op in ops], axis=1)
    y_all = _rwkv_chunk(side(0), side(1), side(2), side(3), side(4), side(5), st_ref, NB * n_heads)
    for b, (r, lw, k, v, kk, lr, g) in enumerate(ops):
        y = y_all[:, b * W:(b + 1) * W]
        mu = _dot_exact_rhs(y, seg) * (1.0 / N)
        yc = y - mu
        var = _dot_exact_rhs(yc * yc, seg) * (1.0 / N)
        yn = yc * lax.rsqrt(var + RWKV_GN_EPS) * lng_ref[...] + lnb_ref[...]
        bonus = _dot_exact_rhs(r * k * rk_ref[...], seg) * v
        y_ref[b] = (yn + bonus) * g


def _rwkv_fused(z, batch, seq, shift_mu, w0, w2, a0, a2, g2, k_k, k_a, r_k, ln_g, ln_b):
    H, N, W = N_HEADS_RWKV, HEAD_RWKV, RWKV_WIDTH
    C = min(RWKV_CHUNK, seq)
    assert N <= C and seq % C == 0 and SHIFT_WIDTH % 128 == 0
    nb = RWKV_BATCH_TILE if batch % RWKV_BATCH_TILE == 0 else 1
    bf16 = jnp.bfloat16
    row = lambda a: a.reshape(1, -1).astype(jnp.float32)
    params = [row(shift_mu), row(w0), w2.astype(bf16), row(a0), a2.astype(bf16), g2.astype(bf16),
              row(k_k), row(k_a), row(r_k), row(ln_g), row(ln_b)]
    full = lambda a: pl.BlockSpec(a.shape, lambda b, c: (0, 0))
    y = pl.pallas_call(
        functools.partial(_rwkv_body, H),
        grid=(batch // nb, seq // C),
        in_specs=[pl.BlockSpec((nb, C, SHIFT_WIDTH), lambda b, c: (b, c, 0))] + [full(p) for p in params],
        out_specs=pl.BlockSpec((nb, C, W), lambda b, c: (b, c, 0)),
        out_shape=jax.ShapeDtypeStruct((batch, seq, W), jnp.float32),
        scratch_shapes=[pltpu.VMEM((nb * H, N, N), jnp.float32), pltpu.VMEM((nb, 1, SHIFT_WIDTH), jnp.float32)],
        compiler_params=pltpu.CompilerParams(dimension_semantics=("arbitrary", "arbitrary"),
                                             vmem_limit_bytes=VMEM_LIMIT),
        name="rwkv_fused",
    )(z.reshape(batch, seq, -1), *params)
    return y.reshape(batch * seq, W)


def _moe_body(be_ref, nu_ref, x_ref, w1_ref, w3_ref, w2_ref, o_ref):
    bf16 = jnp.bfloat16
    used = pl.program_id(0) < nu_ref[0]

    @pl.when(used)
    def _():
        x = x_ref[...].astype(bf16)
        h1 = jnp.dot(x, w1_ref[0, 0].astype(bf16), preferred_element_type=jnp.float32)
        h3 = jnp.dot(x, w3_ref[0, 0].astype(bf16), preferred_element_type=jnp.float32)
        h = (h1 * jax.nn.sigmoid(h1)) * h3
        o_ref[...] = jnp.dot(h.astype(bf16), w2_ref[0, 0].astype(bf16), preferred_element_type=jnp.float32)

    @pl.when(jnp.logical_not(used))
    def _():
        o_ref[...] = jnp.zeros_like(o_ref)


def _moe_experts(xs, blk_exp, n_used, w1, w3, w2, layer, blk):
    n_rows, D = xs.shape
    De = w1.shape[3]
    grid_spec = pltpu.PrefetchScalarGridSpec(
        num_scalar_prefetch=2,
        grid=(n_rows // blk,),
        in_specs=[pl.BlockSpec((blk, D), lambda i, be, nu: (i, 0)),
                  pl.BlockSpec((1, 1, D, De), lambda i, be, nu: (layer, be[i], 0, 0)),
                  pl.BlockSpec((1, 1, D, De), lambda i, be, nu: (layer, be[i], 0, 0)),
                  pl.BlockSpec((1, 1, De, D), lambda i, be, nu: (layer, be[i], 0, 0))],
        out_specs=pl.BlockSpec((blk, D), lambda i, be, nu: (i, 0)))
    return pl.pallas_call(
        _moe_body,
        grid_spec=grid_spec,
        out_shape=jax.ShapeDtypeStruct((n_rows, D), jnp.float32),
        compiler_params=pltpu.CompilerParams(dimension_semantics=("arbitrary",),
                                             vmem_limit_bytes=VMEM_LIMIT),
        name="moe_experts",
    )(blk_exp, n_used, xs, w1, w3, w2)


MOE_DEST_CHUNK = 512


def _moe_dest_body(n_exp, blk, eid_ref, dest_ref, cnt_ref, run_ref, pst_ref):
    ph, c = pl.program_id(0), pl.program_id(1)
    T = eid_ref.shape[1]
    f32, bf16 = jnp.float32, jnp.bfloat16
    onehot = jnp.where(lax.broadcasted_iota(jnp.int32, (n_exp, T), 0) == eid_ref[...], 1.0, 0.0)
    here = jnp.sum(onehot, axis=1, keepdims=True)

    @pl.when((ph == 0) & (c == 0))
    def _():
        run_ref[...] = jnp.zeros_like(run_ref)

    @pl.when(ph == 0)
    def _():
        run_ref[...] = run_ref[...] + here
        dest_ref[...] = jnp.zeros_like(dest_ref)

    @pl.when((ph == 1) & (c == 0))
    def _():
        counts = run_ref[...]
        cnt_ref[...] = counts
        padded = jnp.floor((counts + (blk - 1)) * (1.0 / blk)) * blk
        er = lax.broadcasted_iota(jnp.int32, (n_exp, n_exp), 0)
        ec = lax.broadcasted_iota(jnp.int32, (n_exp, n_exp), 1)
        pst_ref[...] = _dot_exact_lhs(jnp.where(ec < er, 1.0, 0.0).astype(bf16), padded)
        run_ref[...] = jnp.zeros_like(run_ref)

    @pl.when(ph == 1)
    def _():
        sr = lax.broadcasted_iota(jnp.int32, (T, T), 0)
        sc = lax.broadcasted_iota(jnp.int32, (T, T), 1)
        earlier = jnp.dot(onehot.astype(bf16), jnp.where(sr < sc, 1.0, 0.0).astype(bf16),
                          preferred_element_type=f32)
        base = pst_ref[:, 0:1] + run_ref[:, 0:1]
        dest = jnp.sum(onehot * (earlier + base), axis=0, keepdims=True)
        dest_ref[...] = dest.astype(jnp.int32)
        run_ref[...] = run_ref[...] + here


def _moe_dest(eid_row, n_exp, blk):
    n_slots = eid_row.shape[1]
    T = min(MOE_DEST_CHUNK, n_slots)
    assert n_slots % T == 0 and n_slots + n_exp * blk < 2 ** 24
    dest, cnt = pl.pallas_call(
        functools.partial(_moe_dest_body, n_exp, blk),
        grid=(2, n_slots // T),
        in_specs=[pl.BlockSpec((1, T), lambda p, c: (0, c))],
        out_specs=[pl.BlockSpec((1, T), lambda p, c: (0, c * p)),
                   pl.BlockSpec((n_exp, 128), lambda p, c: (0, 0))],
        out_shape=[jax.ShapeDtypeStruct((1, n_slots), jnp.int32),
                   jax.ShapeDtypeStruct((n_exp, 128), jnp.float32)],
        scratch_shapes=[pltpu.VMEM((n_exp, 128), jnp.float32), pltpu.VMEM((n_exp, 128), jnp.float32)],
        compiler_params=pltpu.CompilerParams(dimension_semantics=("arbitrary", "arbitrary")),
        name="moe_dest",
    )(eid_row)
    return dest, cnt[:, 0].astype(jnp.int32)


MOE_TOKEN_TILE = 512


def _row_copy(src_ref, src_row, dst_ref, dst_row, sem):
    return pltpu.make_async_copy(src_ref.at[pl.ds(src_row, 1)], dst_ref.at[pl.ds(dst_row, 1)], sem)


def _moe_dispatch_body(dest_ref, x_ref, init_ref, xs_ref, sem):
    del init_ref
    n_choice, T = dest_ref.shape

    def start(r, carry):
        for k in range(n_choice):
            _row_copy(x_ref, r, xs_ref, dest_ref[k, r], sem).start(priority=k % 2)
        return carry

    def wait(r, carry):
        for k in range(n_choice):
            _row_copy(x_ref, 0, xs_ref, 0, sem).wait()
        return carry

    lax.fori_loop(0, T, start, 0, unroll=8)
    lax.fori_loop(0, T, wait, 0)


def _moe_dispatch(x, dest2, n_rows):
    N, D = x.shape
    T = min(MOE_TOKEN_TILE, N)
    assert N % T == 0
    return pl.pallas_call(
        _moe_dispatch_body,
        grid=(N // T,),
        in_specs=[pl.BlockSpec((dest2.shape[0], T), lambda i: (0, i), memory_space=pltpu.SMEM),
                  pl.BlockSpec((T, D), lambda i: (i, 0)),
                  pl.BlockSpec(memory_space=pl.ANY)],
        out_specs=pl.BlockSpec(memory_space=pl.ANY),
        out_shape=jax.ShapeDtypeStruct((n_rows, D), x.dtype),
        scratch_shapes=[pltpu.SemaphoreType.DMA(())],
        input_output_aliases={2: 0},
        compiler_params=pltpu.CompilerParams(dimension_semantics=("arbitrary",)),
        name="moe_dispatch",
    )(dest2, x, jnp.zeros((n_rows, D), x.dtype))


def _moe_combine_body(dest_ref, ys_ref, w_ref, x_ref, g_ref, b_ref, o_ref, buf_ref, sem):
    n_choice, T = dest_ref.shape

    def start(r, carry):
        for k in range(n_choice):
            _row_copy(ys_ref, dest_ref[k, r], buf_ref.at[k], r, sem).start(priority=k % 2)
        return carry

    def wait(r, carry):
        for k in range(n_choice):
            _row_copy(ys_ref, 0, buf_ref.at[k], 0, sem).wait()
        return carry

    lax.fori_loop(0, T, start, 0, unroll=8)
    lax.fori_loop(0, T, wait, 0)
    w = w_ref[...]
    h = w[:, 0:1] * buf_ref[0]
    for k in range(1, n_choice):
        h = h + w[:, k:k + 1] * buf_ref[k]
    y = ALPHA * x_ref[...] + h
    mu = jnp.mean(y, axis=-1, keepdims=True)
    var = jnp.mean(jnp.square(y - mu), axis=-1, keepdims=True)
    o_ref[...] = (y - mu) * lax.rsqrt(var + LN_EPS) * g_ref[...] + b_ref[...]


def _moe_combine(ys, dest2, w, x, ln_g, ln_b):
    N, D = x.shape
    n_choice = dest2.shape[0]
    T = min(MOE_TOKEN_TILE, N)
    assert N % T == 0
    ln_g, ln_b = ln_g.reshape(1, D), ln_b.reshape(1, D)
    return pl.pallas_call(
        _moe_combine_body,
        grid=(N // T,),
        in_specs=[pl.BlockSpec((n_choice, T), lambda i: (0, i), memory_space=pltpu.SMEM),
                  pl.BlockSpec(memory_space=pl.ANY),
                  pl.BlockSpec((T, n_choice), lambda i: (i, 0)),
                  pl.BlockSpec((T, D), lambda i: (i, 0)),
                  pl.BlockSpec((1, D), lambda i: (0, 0)),
                  pl.BlockSpec((1, D), lambda i: (0, 0))],
        out_specs=pl.BlockSpec((T, D), lambda i: (i, 0)),
        out_shape=jax.ShapeDtypeStruct((N, D), jnp.float32),
        scratch_shapes=[pltpu.VMEM((n_choice, T, D), jnp.float32), pltpu.SemaphoreType.DMA(())],
        compiler_params=pltpu.CompilerParams(dimension_semantics=("arbitrary",)),
        name="moe_combine",
    )(dest2, ys, w, x, ln_g, ln_b)


def _t5_bucket(dist):
    n = jnp.maximum(dist, 0)
    max_exact = NUM_BUCKETS // 2
    nf = jnp.maximum(n, 1).astype(jnp.float32)
    large = max_exact + (jnp.log(nf / max_exact) / math.log(MAX_DISTANCE / max_exact)
                         * (NUM_BUCKETS - max_exact)).astype(jnp.int32)
    large = jnp.minimum(large, NUM_BUCKETS - 1)
    return jnp.where(n < max_exact, n, large)


def _bias_from_buckets(rel_bias, bucket):
    rb = rel_bias.astype(jnp.float32)
    shape = (N_KV_GROUPS, HEADS_PER_GROUP) + (1,) * bucket.ndim
    out = jnp.zeros((N_KV_GROUPS, HEADS_PER_GROUP) + bucket.shape, jnp.float32)
    for b in range(NUM_BUCKETS):
        out = jnp.where(bucket == b, rb[b].reshape(shape), out)
    return out


def _compress(t, pe, w1, w2):
    B, S, G, Dh = t.shape
    rep = CMP_BLOCK // CMP_STRIDE
    nc = S // CMP_STRIDE - rep + 1
    sub = t.reshape(B, S // CMP_STRIDE, CMP_STRIDE, G, Dh)
    blk = jnp.concatenate([sub[:, j:j + nc] for j in range(rep)], axis=2)
    blk = blk + pe[:, None, :]
    blk = blk.transpose(0, 1, 3, 2, 4).reshape(B, nc, G, CMP_BLOCK * Dh)
    out = jax.nn.gelu(blk @ w1) @ w2
    return out.transpose(0, 2, 1, 3)


def _nsa_attention(zt, gates_t, k_cmp, v_cmp, pe_k, w1_k, w2_k, pe_v, w1_v, w2_v, tables):
    bias_cmp_t, sel_tiles, win_tiles, c2s_t, onehot_t, n_cmp = tables
    B, S, _ = k_cmp.shape
    G, Dh = N_KV_GROUPS, HEAD_NSA
    bf16 = jnp.bfloat16
    ncp = bias_cmp_t.shape[2]
    kc = _compress(k_cmp.reshape(B, S, G, Dh), pe_k, w1_k, w2_k)
    vc = _compress(v_cmp.reshape(B, S, G, Dh), pe_v, w1_v, w2_v)
    pad = ((0, 0), (0, 0), (0, ncp - n_cmp), (0, 0))
    kc_p = jnp.pad(kc, pad).astype(bf16)
    vc_t = jnp.pad(vc, pad).transpose(0, 1, 3, 2).astype(bf16)
    n_sel = min(N_SELECT, S // SEL_BLOCK)
    o_t = _nsa_fused(zt, gates_t, kc_p, vc_t, bias_cmp_t, sel_tiles, win_tiles, c2s_t, onehot_t,
                     n_sel, n_cmp)
    return o_t.reshape(B, NSA_Q_WIDTH, S)


def _split_in_proj(w_in):
    D = w_in.shape[0]
    scale = HEAD_NSA ** -0.5
    assert math.frexp(scale)[0] == 0.5
    o = np.cumsum((SHIFT_WIDTH,) + REST_SPLITS)
    col = lambda a, b: w_in[:, o[a]:o[b]]
    row_part = [w_in[:, :SHIFT_WIDTH], col(1, 3)]
    lead = SHIFT_WIDTH + int(o[3] - o[1])
    lead_pad = (-lead) % D_MODEL
    w_row = jnp.concatenate(row_part + [jnp.zeros((D, lead_pad), w_in.dtype), col(8, 10)], axis=1)
    gate_blk = (lead + lead_pad) // D_MODEL
    assert w_row.shape[1] % IN_PROJ_TN == 0
    G, n_g = N_KV_GROUPS, 3 * HEADS_PER_GROUP
    gates = col(7, 8).reshape(D, G, n_g)
    gates = jnp.pad(gates, ((0, 0), (0, 0), (0, NSA_GATE_ROWS - n_g))).reshape(D, G * NSA_GATE_ROWS)
    w_t = jnp.concatenate([col(0, 1) * scale, col(3, 7), gates], axis=1).T
    return w_row.astype(jnp.bfloat16), w_t.astype(jnp.bfloat16), gate_blk, int(o[7] - o[3]) + NSA_Q_WIDTH


def _token_mixer(x, w_in, shift_mu, rw_w0, rw_w2, rw_a0, rw_a2, rw_g2, rw_kk, rw_ka, rw_rk,
                 rw_ln_g, rw_ln_b, cmp_pe_k, cmp_w1_k, cmp_w2_k, cmp_pe_v, cmp_w1_v, cmp_w2_v,
                 w_up_rwkv, w_up_nsa, w_out, ln_g, ln_b, rel_bias, pos):
    B, S, D = x.shape
    xf = x.reshape(B * S, D)
    w_row, w_t, gate_blk, n_main = _split_in_proj(w_in)
    z = _matmul(xf, w_row, IN_PROJ_TM, IN_PROJ_TN)
    zt, gates_t = _matmul_t(xf, w_t, n_main, B, S)
    z3 = z.reshape(B, S, -1)
    k_cmp = z3[..., SHIFT_WIDTH:SHIFT_WIDTH + NSA_KV_WIDTH]
    v_cmp = z3[..., SHIFT_WIDTH + NSA_KV_WIDTH:SHIFT_WIDTH + 2 * NSA_KV_WIDTH]
    y_rw = _rwkv_fused(z, B, S, shift_mu, rw_w0, rw_w2, rw_a0, rw_a2, rw_g2,
                       rw_kk, rw_ka, rw_rk, rw_ln_g, rw_ln_b)
    y_nsa_t = _nsa_attention(zt, gates_t, k_cmp, v_cmp, cmp_pe_k, cmp_w1_k, cmp_w2_k,
                             cmp_pe_v, cmp_w1_v, cmp_w2_v, pos)
    bf16 = jnp.bfloat16
    return _merge_out(y_rw, y_nsa_t, z, gate_blk, xf,
                      w_up_rwkv.astype(bf16), w_up_nsa.astype(bf16), w_out.astype(bf16),
                      ln_g, ln_b).reshape(B, S, D)


def _hier_moe(x, wg, bg, we, be, w1, w3, w2, layer, ln_g, ln_b):
    B, S, D = x.shape
    N = B * S
    f32 = jnp.float32
    xf = x.reshape(N, D)
    g_prob = jax.nn.softmax((xf @ wg + bg).astype(f32), axis=-1)
    grp = jnp.argmax(g_prob, axis=-1)
    p_grp = jnp.take_along_axis(g_prob, grp[:, None], axis=1)[:, 0]
    e_logits = (xf @ we + be).astype(f32).reshape(N, N_GROUPS, EXPERTS_PER_GROUP)
    e_logits = jnp.take_along_axis(e_logits, grp[:, None, None], axis=1)[:, 0]
    top_v, top_i = lax.top_k(e_logits, TOP_K_INNER)
    top_w = jax.nn.softmax(top_v, axis=-1) * p_grp[:, None]
    eid = (grp[:, None] * EXPERTS_PER_GROUP + top_i).astype(jnp.int32)
    n_slots = N * TOP_K_INNER
    n_rows = n_slots + N_EXPERTS * MOE_BLOCK
    n_blk = n_rows // MOE_BLOCK
    dest, counts = _moe_dest(eid.T.reshape(1, n_slots), N_EXPERTS, MOE_BLOCK)
    dest2 = dest.reshape(TOP_K_INNER, N)
    pcounts = (counts + MOE_BLOCK - 1) // MOE_BLOCK * MOE_BLOCK
    pends = jnp.cumsum(pcounts)
    blk_exp = jnp.minimum(jnp.sum(jnp.arange(n_blk)[:, None] * MOE_BLOCK >= pends[None, :], axis=1),
                          N_EXPERTS - 1).astype(jnp.int32)
    xs = _moe_dispatch(xf, dest2, n_rows)
    n_used = (pends[-1:] // MOE_BLOCK).astype(jnp.int32)
    ys = _moe_experts(xs, blk_exp, n_used, w1, w3, w2, layer, MOE_BLOCK)
    return _moe_combine(ys, dest2, top_w, xf, ln_g, ln_b).reshape(B, S, D)


def kernel(x, rel_bias, w_in, shift_mu, rw_w0, rw_w2, rw_a0, rw_a2, rw_g2, rw_kk, rw_ka, rw_rk,
           rw_ln_g, rw_ln_b, cmp_pe_k, cmp_w1_k, cmp_w2_k, cmp_pe_v, cmp_w1_v, cmp_w2_v,
           w_up_rwkv, w_up_nsa, w_out, ln1_g, ln1_b, router_group_w, router_group_b,
           router_expert_w, router_expert_b, exp_w1, exp_w3, exp_w2, ln2_g, ln2_b):
    pos = _nsa_tables(rel_bias, x.shape[1])
    for l in range(DEPTH):
        x = _token_mixer(x, w_in[l], shift_mu[l], rw_w0[l], rw_w2[l], rw_a0[l], rw_a2[l], rw_g2[l],
                         rw_kk[l], rw_ka[l], rw_rk[l], rw_ln_g[l], rw_ln_b[l],
                         cmp_pe_k[l], cmp_w1_k[l], cmp_w2_k[l], cmp_pe_v[l], cmp_w1_v[l], cmp_w2_v[l],
                         w_up_rwkv[l], w_up_nsa[l], w_out[l], ln1_g[l], ln1_b[l], rel_bias, pos)
        x = _hier_moe(x, router_group_w[l], router_group_b[l], router_expert_w[l], router_expert_b[l],
                      exp_w1, exp_w3, exp_w2, l, ln2_g[l], ln2_b[l])
    return x
```

```python
import functools
import math

import jax
import jax.numpy as jnp
import numpy as np
from jax import lax
from jax.experimental import pallas as pl
from jax.experimental.pallas import tpu as pltpu

D_MODEL = 1024
DEPTH = 4
N_HEADS_RWKV = 8
HEAD_RWKV = 64
RWKV_WIDTH = N_HEADS_RWKV * HEAD_RWKV
LORA_W = 64
LORA_A = 64
LORA_G = 128
RWKV_GN_EPS = 64e-5
N_HEADS_NSA = 8
N_KV_GROUPS = 2
HEADS_PER_GROUP = N_HEADS_NSA // N_KV_GROUPS
HEAD_NSA = 64
NSA_Q_WIDTH = N_HEADS_NSA * HEAD_NSA
NSA_KV_WIDTH = N_KV_GROUPS * HEAD_NSA
CMP_BLOCK = 32
CMP_STRIDE = 16
CMP_HIDDEN = 128
SEL_BLOCK = 64
N_SELECT = 16
WINDOW = 512
Q_BLOCK = 128
N_BAND = WINDOW // Q_BLOCK + 1
NEG_INF = -1e30
FORCED_SCORE = 1e4
NUM_BUCKETS = 32
MAX_DISTANCE = 1024
N_GROUPS = 4
EXPERTS_PER_GROUP = 8
N_EXPERTS = N_GROUPS * EXPERTS_PER_GROUP
TOP_K_INNER = 2
D_EXPERT = 512
MOE_BLOCK = 256
ALPHA = (2 * DEPTH) ** 0.25
LN_EPS = 1e-5
SHIFT_SPLITS = (RWKV_WIDTH, RWKV_WIDTH, RWKV_WIDTH, LORA_W, LORA_A, LORA_G)
SHIFT_WIDTH = 3 * RWKV_WIDTH + LORA_W + LORA_A + LORA_G
REST_SPLITS = (NSA_Q_WIDTH,) + (NSA_KV_WIDTH,) * 6 + (3 * N_HEADS_NSA, D_MODEL, D_MODEL)

VMEM_LIMIT = 48 * 1024 * 1024


IN_PROJ_TM = 1024
IN_PROJ_TN = 1024


def _mm_body(x_ref, w_ref, o_ref, xb_ref):
    @pl.when(pl.program_id(1) == 0)
    def _():
        xb_ref[...] = x_ref[...].astype(jnp.bfloat16)

    o_ref[...] = jnp.dot(xb_ref[...], w_ref[...], preferred_element_type=jnp.float32)


def _matmul(x, w, tm, tn):
    m, k = x.shape
    n = w.shape[1]
    assert m % tm == 0 and n % tn == 0
    return pl.pallas_call(
        _mm_body,
        grid=(m // tm, n // tn),
        in_specs=[pl.BlockSpec((tm, k), lambda i, j: (i, 0)),
                  pl.BlockSpec((k, tn), lambda i, j: (0, j))],
        out_specs=pl.BlockSpec((tm, tn), lambda i, j: (i, j)),
        out_shape=jax.ShapeDtypeStruct((m, n), jnp.float32),
        scratch_shapes=[pltpu.VMEM((tm, k), jnp.bfloat16)],
        compiler_params=pltpu.CompilerParams(dimension_semantics=("arbitrary", "arbitrary"),
                                             vmem_limit_bytes=VMEM_LIMIT),
        name="in_proj",
    )(x, w)


IN_PROJ_T_TM = 512
NSA_GATE_ROWS = 16


def _mm_t_body(n_main, x_ref, wt_ref, zt_ref, gt_ref):
    xb = x_ref[...].astype(jnp.bfloat16)
    out = lax.dot_general(wt_ref[...], xb, (((1,), (1,)), ((), ())), preferred_element_type=jnp.float32)
    zt_ref[0] = out[:n_main].astype(zt_ref.dtype)
    gt_ref[0] = out[n_main:]


def _matmul_t(x, wt, n_main, batch, seq):
    m, k = x.shape
    r = wt.shape[0]
    tm = min(IN_PROJ_T_TM, seq)
    assert seq % tm == 0 and m == batch * seq
    per_b = seq // tm
    return pl.pallas_call(
        functools.partial(_mm_t_body, n_main),
        grid=(m // tm,),
        in_specs=[pl.BlockSpec((tm, k), lambda i: (i, 0)),
                  pl.BlockSpec((r, k), lambda i: (0, 0))],
        out_specs=[pl.BlockSpec((1, n_main, tm), lambda i: (i // per_b, 0, i % per_b)),
                   pl.BlockSpec((1, r - n_main, tm), lambda i: (i // per_b, 0, i % per_b))],
        out_shape=[jax.ShapeDtypeStruct((batch, n_main, seq), jnp.bfloat16),
                   jax.ShapeDtypeStruct((batch, r - n_main, seq), jnp.float32)],
        compiler_params=pltpu.CompilerParams(dimension_semantics=("arbitrary",),
                                             vmem_limit_bytes=VMEM_LIMIT),
        name="in_proj_t",
    )(x, wt)


MERGE_TM = 512


def _merge_body(yr_ref, yn_ref, grw_ref, gns_ref, x_ref, wur_ref, wun_ref, wo_ref, g_ref, b_ref, o_ref):
    f32, bf16 = jnp.float32, jnp.bfloat16
    up_r = jnp.dot(yr_ref[...].astype(bf16), wur_ref[...], preferred_element_type=f32)
    up_n = lax.dot_general(yn_ref[0].astype(bf16), wun_ref[...], _TN, preferred_element_type=f32)
    merged = jax.nn.sigmoid(grw_ref[...]) * up_r + jax.nn.sigmoid(gns_ref[...]) * up_n
    y = ALPHA * x_ref[...] + jnp.dot(merged.astype(bf16), wo_ref[...], preferred_element_type=f32)
    mu = jnp.mean(y, axis=-1, keepdims=True)
    var = jnp.mean(jnp.square(y - mu), axis=-1, keepdims=True)
    o_ref[...] = (y - mu) * lax.rsqrt(var + LN_EPS) * g_ref[...] + b_ref[...]


def _merge_out(y_rw, y_nsa_t, z, gate_blk, x, w_up_rwkv, w_up_nsa, w_out, ln_g, ln_b):
    m, d = x.shape
    tm = MERGE_TM
    nb, wn, seq = y_nsa_t.shape
    assert m % tm == 0 and seq % tm == 0
    per_b = seq // tm
    row = lambda width: pl.BlockSpec((tm, width), lambda i: (i, 0))
    full = lambda a: pl.BlockSpec(a.shape, lambda i: (0,) * a.ndim)
    ln_g, ln_b = ln_g.reshape(1, d), ln_b.reshape(1, d)
    return pl.pallas_call(
        _merge_body,
        grid=(m // tm,),
        in_specs=[row(y_rw.shape[1]), pl.BlockSpec((1, wn, tm), lambda i: (i // per_b, 0, i % per_b)),
                  pl.BlockSpec((tm, d), lambda i: (i, gate_blk)),
                  pl.BlockSpec((tm, d), lambda i: (i, gate_blk + 1)),
                  row(d), full(w_up_rwkv), full(w_up_nsa), full(w_out), full(ln_g), full(ln_b)],
        out_specs=row(d),
        out_shape=jax.ShapeDtypeStruct((m, d), jnp.float32),
        compiler_params=pltpu.CompilerParams(dimension_semantics=("arbitrary",),
                                             vmem_limit_bytes=VMEM_LIMIT),
        name="merge_out",
    )(y_rw, y_nsa_t, z, z, x, w_up_rwkv, w_up_nsa, w_out, ln_g, ln_b)


def _sel_bias_tiles(rel_bias, seq):
    nd = -(-MAX_DISTANCE // Q_BLOCK)
    nd = min(nd, seq // Q_BLOCK)
    half = NUM_BUCKETS // 2
    min_far = nd * Q_BLOCK - (Q_BLOCK - 1)
    assert half + math.log(min_far / half) / math.log(MAX_DISTANCE / half) * half >= NUM_BUCKETS - 0.75
    kj = np.arange(Q_BLOCK)[:, None]
    qi = np.arange(Q_BLOCK)[None, :]
    dist = np.arange(nd + 1)[:, None, None] * Q_BLOCK + qi - kj
    dist[nd] = max(seq - 1, nd * Q_BLOCK)
    tiles = _bias_from_buckets(rel_bias, _t5_bucket(jnp.asarray(np.maximum(dist, 0))))
    causal = jnp.asarray(dist >= 0)
    return jnp.where(causal, tiles, NEG_INF)


SEL_KEY_TILE = 256


def _flash_update(lgs, v_ts, ms, ls, acc_ref):
    H = range(len(lgs))
    m_new = []
    for h in H:
        m = ms[h]
        for lg in lgs[h]:
            m = jnp.maximum(m, jnp.max(lg, axis=0, keepdims=True))
        m_new.append(m)
    ps = [[jnp.exp(lg - m_new[h]) for lg in lgs[h]] for h in H]
    alpha = [jnp.exp(ms[h] - m_new[h]) for h in H]
    l_new = []
    for h in H:
        l = alpha[h] * ls[h]
        for p in ps[h]:
            l = l + jnp.sum(p, axis=0, keepdims=True)
        l_new.append(l)
    pv = [[jnp.dot(v_t, p.astype(jnp.bfloat16), preferred_element_type=jnp.float32)
           for p, v_t in zip(ps[h], v_ts[h])] for h in H]
    for h in H:
        acc = alpha[h] * acc_ref[h]
        for o in pv[h]:
            acc = acc + o
        acc_ref[h] = acc
    return tuple(m_new), tuple(l_new)


def _nsa_body(n_sel, n_cmp, n_delta, q_ref, kc_ref, vc_ref, ks_ref, vs_ref, kw_ref, vw_ref, gl_ref,
              bc_ref, ts_ref, tw_ref, c2s_ref, oh_ref, o_ref, acc_s, acc_w, ocmp_ref):
    i = pl.program_id(1)
    hg = HEADS_PER_GROUP
    ng = q_ref.shape[1] // hg
    dh, qb = q_ref.shape[2], q_ref.shape[3]
    ncp = kc_ref.shape[2]
    ns = c2s_ref.shape[0]
    f32, bf16 = jnp.float32, jnp.bfloat16
    H = range(ng * hg)
    G = range(ng)

    cidx = lax.broadcasted_iota(jnp.int32, (ncp, qb), 0)
    tpos = lax.broadcasted_iota(jnp.int32, (ncp, qb), 1) + i * qb
    valid = (tpos - (cidx * CMP_STRIDE + CMP_BLOCK - 1)) >= 0
    fill = jnp.where(cidx < n_cmp, NEG_INF, 2.0 * NEG_INF)
    qkc = [jnp.dot(kc_ref[0, h // hg], q_ref[0, h], preferred_element_type=f32) for h in H]
    lgc = [jnp.where(valid, qkc[h] + bc_ref[h // hg, h % hg], fill) for h in H]
    mc = [jnp.max(lgc[h], axis=0, keepdims=True) for h in H]
    pc = [jnp.exp(lgc[h] - mc[h]) for h in H]
    inv = [1.0 / jnp.sum(pc[h], axis=0, keepdims=True) for h in H]
    pc = [jnp.where(valid, pc[h] * inv[h], 0.0) for h in H]
    psum = [pc[g * hg] for g in G]
    for h in H:
        if h % hg:
            psum[h // hg] = psum[h // hg] + pc[h]
    for h in H:
        ocmp_ref[h] = jnp.dot(vc_ref[0, h // hg], pc[h].astype(bf16), preferred_element_type=f32)

    jblk = lax.broadcasted_iota(jnp.int32, (ns, qb), 0)
    cur = (lax.broadcasted_iota(jnp.int32, (ns, qb), 1) + i * qb) // SEL_BLOCK
    forced = (jblk == 0) | (jblk == cur) | (jblk == cur - 1)
    score = [jnp.dot(c2s_ref[...], psum[g].astype(bf16), preferred_element_type=f32) for g in G]
    score = [jnp.where(forced, FORCED_SCORE, jnp.where(jblk <= cur, score[g], -1.0)) for g in G]
    cnt = [jnp.zeros((ns, qb), f32) for g in G]
    for jp in range(ns):
        tie = jnp.where(jblk > jp, 1.0, 0.0)
        for g in G:
            row = score[g][jp:jp + 1, :]
            cnt[g] = cnt[g] + jnp.where(row > score[g], 1.0, jnp.where(row == score[g], tie, 0.0))
    maskneg = [jnp.where(cnt[g] < n_sel, 0.0, NEG_INF).astype(bf16) for g in G]
    q_aug = [jnp.concatenate([q_ref[0, h], maskneg[h // hg]], axis=0) for h in H]

    kt = SEL_KEY_TILE
    per = kt // qb
    last_tile = i // per
    acc_s[...] = jnp.zeros_like(acc_s)

    def sel_step(jj, carry):
        ms, ls = carry
        ks_, vs_, tiles = [], [], []
        for u in range(2):
            j = 2 * jj + u
            off = pl.multiple_of(jnp.minimum(j, last_tile) * kt, kt)
            oh = oh_ref[:, pl.ds(off, kt)]
            ks_.append([jnp.concatenate([ks_ref[0, g, :, pl.ds(off, kt)], oh], axis=0) for g in G])
            vs_.append([vs_ref[0, g, :, pl.ds(off, kt)] for g in G])
            tiles.append([jnp.clip(i - (j * per + w), -1, n_delta) + 1 for w in range(per)])
        qk = [[lax.dot_general(ks_[u][h // hg], q_aug[h], _TN, preferred_element_type=f32) for u in range(2)]
              for h in H]
        lgs = [[qk[h][u] + jnp.concatenate([ts_ref[h // hg, h % hg, tiles[u][w]] for w in range(per)], axis=0)
                for u in range(2)] for h in H]
        return _flash_update(lgs, [[vs_[u][h // hg] for u in range(2)] for h in H], ms, ls, acc_s)

    init = (tuple(jnp.full((1, qb), NEG_INF, f32) for _ in H), tuple(jnp.zeros((1, qb), f32) for _ in H))
    _, ls_s = lax.fori_loop(0, last_tile // 2 + 1, sel_step, init)

    acc_w[...] = jnp.zeros_like(acc_w)
    kws, vws, widx = [], [], []
    for u in range(N_BAND):
        j = i - (N_BAND - 1) + u
        off = pl.multiple_of(jnp.maximum(j, 0) * qb, qb)
        kws.append([kw_ref[0, g, :, pl.ds(off, qb)] for g in G])
        vws.append([vw_ref[0, g, :, pl.ds(off, qb)] for g in G])
        widx.append(jnp.where(j >= 0, N_BAND - 1 - u, N_BAND))
    qk = [[lax.dot_general(kws[u][h // hg], q_ref[0, h], _TN, preferred_element_type=f32) for u in range(N_BAND)]
          for h in H]
    lgs = [[qk[h][u] + tw_ref[h // hg, h % hg, widx[u]] for u in range(N_BAND)] for h in H]
    _, ls_w = _flash_update(lgs, [[vws[u][h // hg] for u in range(N_BAND)] for h in H], init[0], init[1], acc_w)

    gates = jax.nn.sigmoid(gl_ref[0])
    for h in H:
        r0 = (h // hg) * NSA_GATE_ROWS + 3 * (h % hg)
        o_ref[0, h] = (gates[r0:r0 + 1] * ocmp_ref[h]
                       + gates[r0 + 1:r0 + 2] * (acc_s[h] / ls_s[h])
                       + gates[r0 + 2:r0 + 3] * (acc_w[h] / ls_w[h]))


def _nsa_fused(zt, gates_t, kc, vc_t, bias_cmp_t, sel_tiles, win_tiles, c2s_t, onehot_t, n_sel, n_cmp):
    B, _, S = zt.shape
    G, Hg, Dh = N_KV_GROUPS, HEADS_PER_GROUP, HEAD_NSA
    NCP = kc.shape[2]
    NS = c2s_t.shape[0]
    QB = Q_BLOCK
    ND = sel_tiles.shape[2] - 2
    assert S % SEL_KEY_TILE == 0 and SEL_KEY_TILE % QB == 0
    z4 = zt.reshape(B, -1, Dh, S)
    stream = lambda n: pl.BlockSpec((1, G, Dh, S), lambda b, i: (b, Hg + n, 0, 0))
    bg = lambda *blk: pl.BlockSpec((1, G) + blk, lambda b, i: (b, 0) + (0,) * len(blk))
    gt = lambda arr: pl.BlockSpec(arr.shape, lambda b, i: (0,) * arr.ndim)
    return pl.pallas_call(
        functools.partial(_nsa_body, float(n_sel), n_cmp, ND),
        grid=(B, S // QB),
        in_specs=[pl.BlockSpec((1, G * Hg, Dh, QB), lambda b, i: (b, 0, 0, i)),
                  bg(NCP, Dh), bg(Dh, NCP), stream(0), stream(1), stream(2), stream(3),
                  pl.BlockSpec((1, G * NSA_GATE_ROWS, QB), lambda b, i: (b, 0, i)),
                  pl.BlockSpec((G, Hg, NCP, QB), lambda b, i: (0, 0, 0, i)),
                  gt(sel_tiles), gt(win_tiles),
                  pl.BlockSpec(c2s_t.shape, lambda b, i: (0, 0)),
                  pl.BlockSpec(onehot_t.shape, lambda b, i: (0, 0))],
        out_specs=pl.BlockSpec((1, G * Hg, Dh, QB), lambda b, i: (b, 0, 0, i)),
        out_shape=jax.ShapeDtypeStruct((B, G * Hg, Dh, S), jnp.float32),
        scratch_shapes=[pltpu.VMEM((G * Hg, Dh, QB), jnp.float32)] * 3,
        compiler_params=pltpu.CompilerParams(dimension_semantics=("arbitrary",) * 2,
                                             vmem_limit_bytes=VMEM_LIMIT),
        name="nsa_fused",
    )(z4, kc, vc_t, z4, z4, z4, z4, gates_t, bias_cmp_t, sel_tiles, win_tiles, c2s_t, onehot_t)


def _nsa_tables(rel_bias, seq):
    n_cmp = seq // CMP_STRIDE - CMP_BLOCK // CMP_STRIDE + 1
    ncp = -(-n_cmp // 128) * 128
    c = jnp.arange(ncp)[:, None]
    t = jnp.arange(seq)[None, :]
    bias_cmp_t = _bias_from_buckets(rel_bias, _t5_bucket(t - (c * CMP_STRIDE + CMP_BLOCK - 1)))
    base = _sel_bias_tiles(rel_bias, seq)
    sel_tiles = jnp.concatenate([jnp.full_like(base[:, :, :1], NEG_INF), base], axis=2)
    kj = np.arange(Q_BLOCK)[:, None]
    qi = np.arange(Q_BLOCK)[None, :]
    dwin = np.arange(N_BAND)[:, None, None] * Q_BLOCK + qi - kj
    nwin = min(N_BAND, base.shape[2])
    win_tiles = jnp.where(jnp.asarray(dwin[:nwin] < WINDOW), base[:, :, :nwin], NEG_INF)
    win_tiles = jnp.concatenate(
        [win_tiles, jnp.full(win_tiles.shape[:2] + (N_BAND + 1 - nwin,) + win_tiles.shape[3:], NEG_INF)], axis=2)
    n_sel_blocks = seq // SEL_BLOCK
    cs = np.arange(ncp)[None, :] * CMP_STRIDE
    ss = np.arange(n_sel_blocks)[:, None] * SEL_BLOCK
    overlap = np.clip(np.minimum(cs + CMP_BLOCK, ss + SEL_BLOCK) - np.maximum(cs, ss), 0, None)
    overlap = np.where(np.arange(ncp)[None, :] < n_cmp, overlap, 0)
    c2s_t = jnp.asarray(overlap.astype(np.float32) / CMP_BLOCK, jnp.bfloat16)
    onehot_t = jnp.asarray(np.arange(n_sel_blocks)[:, None] == np.arange(seq)[None, :] // SEL_BLOCK, jnp.bfloat16)
    return bias_cmp_t, sel_tiles, win_tiles, c2s_t, onehot_t, n_cmp


RWKV_CHUNK = 64
RWKV_INV_BASE = 8


def _split_bf16(x):
    hi = x.astype(jnp.bfloat16)
    lo = (x - hi.astype(jnp.float32)).astype(jnp.bfloat16)
    return hi, lo


def _dot3(a, b, dims=(((1,), (0,)), ((), ()))):
    a_hi, a_lo = _split_bf16(a)
    b_hi, b_lo = _split_bf16(b)
    d = functools.partial(lax.dot_general, dimension_numbers=dims, preferred_element_type=jnp.float32)
    free_axis = 1 - dims[0][0][0]
    m = a.shape[free_axis]
    both = d(jnp.concatenate([a_hi, a_lo], axis=free_axis), b_hi)
    return both[:m] + (both[m:] + d(a_hi, b_lo))


_NT = (((1,), (1,)), ((), ()))
_TN = (((0,), (0,)), ((), ()))


def _rwkv_chunk(r_w, lw, k_w, v_w, kk, lr, st_ref, n_heads):
    C = r_w.shape[0]
    N = r_w.shape[1] // n_heads
    H = range(n_heads)
    f32 = jnp.float32
    row = lax.broadcasted_iota(jnp.int32, (C, C), 0)
    col = lax.broadcasted_iota(jnp.int32, (C, C), 1)
    strict = col < row
    incl = col <= row
    eye = jnp.where(row == col, 1.0, 0.0).astype(f32)
    tri = jnp.where(incl, 1.0, 0.0).astype(jnp.bfloat16)

    cl = _dot_exact_lhs(tri, lw)

    base = RWKV_INV_BASE
    diag_blk = strict & ((row // base) == (col // base))
    level_masks = []
    s = base
    while s < C:
        level_masks.append(strict & ((row // (2 * s)) == (col // (2 * s))) & ((row // s) != (col // s)))
        s *= 2

    cl_end = cl[C - 1:C, :]
    bb = kk * lr
    g_inv = jnp.exp(-cl)
    g_rem = jnp.exp(cl_end - cl)
    g_end = jnp.exp(cl_end)
    abar_w = -kk * jnp.exp(cl - lw)
    rbar_w = r_w * jnp.exp(cl)
    bbar_w = bb * g_inv
    kbar_w = k_w * g_inv
    bhat_w = bb * g_rem
    khat_w = k_w * g_rem
    hs = lambda x, h: x[:, h * N:(h + 1) * N]

    abar = [hs(abar_w, h) for h in H]
    rbar = [hs(rbar_w, h) for h in H]
    v = [hs(v_w, h) for h in H]
    gmat = [_dot3(jnp.concatenate([abar[h], rbar[h]], axis=0),
                  jnp.concatenate([hs(bbar_w, h), hs(kbar_w, h)], axis=0), _NT) for h in H]
    a_ab = [jnp.where(strict, gmat[h][:C, :C], 0.0) for h in H]
    a_ak = [jnp.where(strict, gmat[h][:C, C:], 0.0) for h in H]
    m_rb = [jnp.where(incl, gmat[h][C:, :C], 0.0) for h in H]
    m_rk = [jnp.where(incl, gmat[h][C:, C:], 0.0) for h in H]
    akv = [_dot3(a_ak[h], v[h]) for h in H]
    dp = [jnp.where(diag_blk, a_ab[h], 0.0) for h in H]
    x = [eye + dp[h] for h in H]
    s = 2
    while s < base:
        dp = [_dot3(dp[h], dp[h]) for h in H]
        x = [x[h] + _dot3(x[h], dp[h]) for h in H]
        s *= 2
    for lm in level_masks:
        t = [_dot3(jnp.where(lm, a_ab[h], 0.0), x[h]) for h in H]
        x = [x[h] + _dot3(x[h], t[h]) for h in H]
    xw = [_dot3(x[h], jnp.concatenate([akv[h], abar[h]], axis=1)) for h in H]
    uv = [jnp.concatenate([xw[h][:, :N], v[h]], axis=0) for h in H]
    atil = [xw[h][:, N:] for h in H]
    y_loc = [_dot3(jnp.concatenate([m_rb[h], m_rk[h]], axis=1), uv[h]) for h in H]
    qm = [rbar[h] + _dot3(m_rb[h], atil[h]) for h in H]
    s_loc = [_dot3(jnp.concatenate([hs(bhat_w, h), hs(khat_w, h)], axis=0), uv[h], _TN) for h in H]
    pm = [eye[:N, :N] * hs(g_end, h) + _dot3(hs(bhat_w, h), atil[h], _TN) for h in H]
    s0 = [st_ref[h] for h in H]
    y = [y_loc[h] + _dot3(qm[h], s0[h]) for h in H]
    for h in H:
        st_ref[h] = s_loc[h] + _dot3(pm[h], s0[h])
    return jnp.concatenate(y, axis=1)


def _bf16_pieces(x):
    f32 = jnp.float32
    p1 = x.astype(jnp.bfloat16)
    r1 = x - p1.astype(f32)
    p2 = r1.astype(jnp.bfloat16)
    p3 = (r1 - p2.astype(f32)).astype(jnp.bfloat16)
    return p1, p2, p3


def _dot_exact_lhs(a_bf16, x):
    dd = functools.partial(jnp.dot, preferred_element_type=jnp.float32)
    p1, p2, p3 = _bf16_pieces(x)
    return dd(a_bf16, p1) + (dd(a_bf16, p2) + dd(a_bf16, p3))


def _dot_exact_rhs(x, b_bf16):
    dd = functools.partial(jnp.dot, preferred_element_type=jnp.float32)
    p1, p2, p3 = _bf16_pieces(x)
    return dd(p1, b_bf16) + (dd(p2, b_bf16) + dd(p3, b_bf16))


RWKV_BATCH_TILE = 4


def _rwkv_body(n_heads, z_ref, mu_ref, w0_ref, w2_ref, a0_ref, a2_ref, g2_ref, kk_ref, ka_ref, rk_ref,
               lng_ref, lnb_ref, y_ref, st_ref, prev_ref):
    c = pl.program_id(1)
    NB, C, _ = z_ref.shape
    W = y_ref.shape[2]
    N = W // n_heads
    f32, bf16 = jnp.float32, jnp.bfloat16
    dd = functools.partial(jnp.dot, preferred_element_type=f32)

    @pl.when(c == 0)
    def _():
        st_ref[...] = jnp.zeros_like(st_ref)
        prev_ref[...] = jnp.zeros_like(prev_ref)

    hrow = lax.broadcasted_iota(jnp.int32, (W, W), 0) // N
    hcol = lax.broadcasted_iota(jnp.int32, (W, W), 1) // N
    seg = jnp.where(hrow == hcol, 1.0, 0.0).astype(bf16)
    o = np.cumsum((0,) + SHIFT_SPLITS)
    ops = []
    for b in range(NB):
        z = z_ref[b]
        z_prev = jnp.concatenate([prev_ref[b], z[:C - 1]], axis=0)
        prev_ref[b] = z[C - 1:C]
        zs = z + (z_prev - z) * mu_ref[...]
        r, k, v, wl, al, gl = (zs[:, o[i]:o[i + 1]] for i in range(6))
        yw = w0_ref[...] + dd(jnp.tanh(wl).astype(bf16), w2_ref[...])
        logw = -(jnp.maximum(-yw, 0.0) + jnp.log1p(jnp.exp(-jnp.abs(yw)))) - 0.5
        lw = -jnp.exp(logw)
        lr = jax.nn.sigmoid(a0_ref[...] + dd(al.astype(bf16), a2_ref[...]))
        g = dd(jax.nn.sigmoid(gl).astype(bf16), g2_ref[...])
        kk = k * kk_ref[...]
        kk = kk / jnp.maximum(jnp.sqrt(_dot_exact_rhs(kk * kk, seg)), 1e-12)
        k = k * (1.0 + (lr - 1.0) * ka_ref[...])
        ops.append((r, lw, k, v, kk, lr, g))
    side = lambda idx: jnp.concatenate([op[idx] for op in ops], axis=1)
    y_all = _rwkv_chunk(side(0), side(1), side(2), side(3), side(4), side(5), st_ref, NB * n_heads)
    for b, (r, lw, k, v, kk, lr, g) in enumerate(ops):
        y = y_all[:, b * W:(b + 1) * W]
        mu = _dot_exact_rhs(y, seg) * (1.0 / N)
        yc = y - mu
        var = _dot_exact_rhs(yc * yc, seg) * (1.0 / N)
        yn = yc * lax.rsqrt(var + RWKV_GN_EPS) * lng_ref[...] + lnb_ref[...]
        bonus = _dot_exact_rhs(r * k * rk_ref[...], seg) * v
        y_ref[b] = (yn + bonus) * g


def _rwkv_fused(z, batch, seq, shift_mu, w0, w2, a0, a2, g2, k_k, k_a, r_k, ln_g, ln_b):
    H, N, W = N_HEADS_RWKV, HEAD_RWKV, RWKV_WIDTH
    C = min(RWKV_CHUNK, seq)
    assert N <= C and seq % C == 0 and SHIFT_WIDTH % 128 == 0
    nb = RWKV_BATCH_TILE if batch % RWKV_BATCH_TILE == 0 else 1
    bf16 = jnp.bfloat16
    row = lambda a: a.reshape(1, -1).astype(jnp.float32)
    params = [row(shift_mu), row(w0), w2.astype(bf16), row(a0), a2.astype(bf16), g2.astype(bf16),
              row(k_k), row(k_a), row(r_k), row(ln_g), row(ln_b)]
    full = lambda a: pl.BlockSpec(a.shape, lambda b, c: (0, 0))
    y = pl.pallas_call(
        functools.partial(_rwkv_body, H),
        grid=(batch // nb, seq // C),
        in_specs=[pl.BlockSpec((nb, C, SHIFT_WIDTH), lambda b, c: (b, c, 0))] + [full(p) for p in params],
        out_specs=pl.BlockSpec((nb, C, W), lambda b, c: (b, c, 0)),
        out_shape=jax.ShapeDtypeStruct((batch, seq, W), jnp.float32),
        scratch_shapes=[pltpu.VMEM((nb * H, N, N), jnp.float32), pltpu.VMEM((nb, 1, SHIFT_WIDTH), jnp.float32)],
        compiler_params=pltpu.CompilerParams(dimension_semantics=("arbitrary", "arbitrary"),
                                             vmem_limit_bytes=VMEM_LIMIT),
        name="rwkv_fused",
    )(z.reshape(batch, seq, -1), *params)
    return y.reshape(batch * seq, W)


def _moe_body(be_ref, nu_ref, x_ref, w1_ref, w3_ref, w2_ref, o_ref):
    bf16 = jnp.bfloat16
    used = pl.program_id(0) < nu_ref[0]

    @pl.when(used)
    def _():
        x = x_ref[...].astype(bf16)
        h1 = jnp.dot(x, w1_ref[0, 0].astype(bf16), preferred_element_type=jnp.float32)
        h3 = jnp.dot(x, w3_ref[0, 0].astype(bf16), preferred_element_type=jnp.float32)
        h = (h1 * jax.nn.sigmoid(h1)) * h3
        o_ref[...] = jnp.dot(h.astype(bf16), w2_ref[0, 0].astype(bf16), preferred_element_type=jnp.float32)

    @pl.when(jnp.logical_not(used))
    def _():
        o_ref[...] = jnp.zeros_like(o_ref)


def _moe_experts(xs, blk_exp, n_used, w1, w3, w2, layer, blk):
    n_rows, D = xs.shape
    De = w1.shape[3]
    grid_spec = pltpu.PrefetchScalarGridSpec(
        num_scalar_prefetch=2,
        grid=(n_rows // blk,),
        in_specs=[pl.BlockSpec((blk, D), lambda i, be, nu: (i, 0)),
                  pl.BlockSpec((1, 1, D, De), lambda i, be, nu: (layer, be[i], 0, 0)),
                  pl.BlockSpec((1, 1, D, De), lambda i, be, nu: (layer, be[i], 0, 0)),
                  pl.BlockSpec((1, 1, De, D), lambda i, be, nu: (layer, be[i], 0, 0))],
        out_specs=pl.BlockSpec((blk, D), lambda i, be, nu: (i, 0)))
    return pl.pallas_call(
        _moe_body,
        grid_spec=grid_spec,
        out_shape=jax.ShapeDtypeStruct((n_rows, D), jnp.float32),
        compiler_params=pltpu.CompilerParams(dimension_semantics=("arbitrary",),
                                             vmem_limit_bytes=VMEM_LIMIT),
        name="moe_experts",
    )(blk_exp, n_used, xs, w1, w3, w2)


MOE_DEST_CHUNK = 512


def _moe_dest_body(n_exp, blk, eid_ref, dest_ref, cnt_ref, run_ref, pst_ref):
    ph, c = pl.program_id(0), pl.program_id(1)
    T = eid_ref.shape[1]
    f32, bf16 = jnp.float32, jnp.bfloat16
    onehot = jnp.where(lax.broadcasted_iota(jnp.int32, (n_exp, T), 0) == eid_ref[...], 1.0, 0.0)
    here = jnp.sum(onehot, axis=1, keepdims=True)

    @pl.when((ph == 0) & (c == 0))
    def _():
        run_ref[...] = jnp.zeros_like(run_ref)

    @pl.when(ph == 0)
    def _():
        run_ref[...] = run_ref[...] + here
        dest_ref[...] = jnp.zeros_like(dest_ref)

    @pl.when((ph == 1) & (c == 0))
    def _():
        counts = run_ref[...]
        cnt_ref[...] = counts
        padded = jnp.floor((counts + (blk - 1)) * (1.0 / blk)) * blk
        er = lax.broadcasted_iota(jnp.int32, (n_exp, n_exp), 0)
        ec = lax.broadcasted_iota(jnp.int32, (n_exp, n_exp), 1)
        pst_ref[...] = _dot_exact_lhs(jnp.where(ec < er, 1.0, 0.0).astype(bf16), padded)
        run_ref[...] = jnp.zeros_like(run_ref)

    @pl.when(ph == 1)
    def _():
        sr = lax.broadcasted_iota(jnp.int32, (T, T), 0)
        sc = lax.broadcasted_iota(jnp.int32, (T, T), 1)
        earlier = jnp.dot(onehot.astype(bf16), jnp.where(sr < sc, 1.0, 0.0).astype(bf16),
                          preferred_element_type=f32)
        base = pst_ref[:, 0:1] + run_ref[:, 0:1]
        dest = jnp.sum(onehot * (earlier + base), axis=0, keepdims=True)
        dest_ref[...] = dest.astype(jnp.int32)
        run_ref[...] = run_ref[...] + here


def _moe_dest(eid_row, n_exp, blk):
    n_slots = eid_row.shape[1]
    T = min(MOE_DEST_CHUNK, n_slots)
    assert n_slots % T == 0 and n_slots + n_exp * blk < 2 ** 24
    dest, cnt = pl.pallas_call(
        functools.partial(_moe_dest_body, n_exp, blk),
        grid=(2, n_slots // T),
        in_specs=[pl.BlockSpec((1, T), lambda p, c: (0, c))],
        out_specs=[pl.BlockSpec((1, T), lambda p, c: (0, c * p)),
                   pl.BlockSpec((n_exp, 128), lambda p, c: (0, 0))],
        out_shape=[jax.ShapeDtypeStruct((1, n_slots), jnp.int32),
                   jax.ShapeDtypeStruct((n_exp, 128), jnp.float32)],
        scratch_shapes=[pltpu.VMEM((n_exp, 128), jnp.float32), pltpu.VMEM((n_exp, 128), jnp.float32)],
        compiler_params=pltpu.CompilerParams(dimension_semantics=("arbitrary", "arbitrary")),
        name="moe_dest",
    )(eid_row)
    return dest, cnt[:, 0].astype(jnp.int32)


MOE_TOKEN_TILE = 512


def _row_copy(src_ref, src_row, dst_ref, dst_row, sem):
    return pltpu.make_async_copy(src_ref.at[pl.ds(src_row, 1)], dst_ref.at[pl.ds(dst_row, 1)], sem)


def _moe_dispatch_body(dest_ref, x_ref, init_ref, xs_ref, sem):
    del init_ref
    n_choice, T = dest_ref.shape

    def start(r, carry):
        for k in range(n_choice):
            _row_copy(x_ref, r, xs_ref, dest_ref[k, r], sem).start(priority=k % 2)
        return carry

    def wait(r, carry):
        for k in range(n_choice):
            _row_copy(x_ref, 0, xs_ref, 0, sem).wait()
        return carry

    lax.fori_loop(0, T, start, 0, unroll=8)
    lax.fori_loop(0, T, wait, 0)


def _moe_dispatch(x, dest2, n_rows):
    N, D = x.shape
    T = min(MOE_TOKEN_TILE, N)
    assert N % T == 0
    return pl.pallas_call(
        _moe_dispatch_body,
        grid=(N // T,),
        in_specs=[pl.BlockSpec((dest2.shape[0], T), lambda i: (0, i), memory_space=pltpu.SMEM),
                  pl.BlockSpec((T, D), lambda i: (i, 0)),
                  pl.BlockSpec(memory_space=pl.ANY)],
        out_specs=pl.BlockSpec(memory_space=pl.ANY),
        out_shape=jax.ShapeDtypeStruct((n_rows, D), x.dtype),
        scratch_shapes=[pltpu.SemaphoreType.DMA(())],
        input_output_aliases={2: 0},
        compiler_params=pltpu.CompilerParams(dimension_semantics=("arbitrary",)),
        name="moe_dispatch",
    )(dest2, x, jnp.zeros((n_rows, D), x.dtype))


def _moe_combine_body(dest_ref, next_ref, ys_ref, w_ref, x_ref, g_ref, b_ref, o_ref, buf_ref, sem):
    i = pl.program_id(0)
    n_steps = pl.num_programs(0)
    n_choice, T = dest_ref.shape
    slot = i % 2

    def issue(idx_ref, s):
        def start(r, carry):
            for k in range(n_choice):
                _row_copy(ys_ref, idx_ref[k, r], buf_ref.at[s, k], r, sem.at[s]).start(priority=k % 2)
            return carry
        lax.fori_loop(0, T, start, 0, unroll=8)

    @pl.when(i == 0)
    def _():
        issue(dest_ref, 0)

    @pl.when(i + 1 < n_steps)
    def _():
        issue(next_ref, 1 - slot)

    def wait(r, carry):
        for k in range(n_choice):
            _row_copy(ys_ref, 0, buf_ref.at[slot, k], 0, sem.at[slot]).wait()
        return carry

    lax.fori_loop(0, T, wait, 0)
    w = w_ref[...]
    h = w[:, 0:1] * buf_ref[slot, 0]
    for k in range(1, n_choice):
        h = h + w[:, k:k + 1] * buf_ref[slot, k]
    y = ALPHA * x_ref[...] + h
    mu = jnp.mean(y, axis=-1, keepdims=True)
    var = jnp.mean(jnp.square(y - mu), axis=-1, keepdims=True)
    o_ref[...] = (y - mu) * lax.rsqrt(var + LN_EPS) * g_ref[...] + b_ref[...]


def _moe_combine(ys, dest2, w, x, ln_g, ln_b):
    N, D = x.shape
    n_choice = dest2.shape[0]
    T = min(MOE_TOKEN_TILE, N)
    assert N % T == 0
    nt = N // T
    ln_g, ln_b = ln_g.reshape(1, D), ln_b.reshape(1, D)
    return pl.pallas_call(
        _moe_combine_body,
        grid=(nt,),
        in_specs=[pl.BlockSpec((n_choice, T), lambda i: (0, i), memory_space=pltpu.SMEM),
                  pl.BlockSpec((n_choice, T), lambda i: (0, jnp.minimum(i + 1, nt - 1)), memory_space=pltpu.SMEM),
                  pl.BlockSpec(memory_space=pl.ANY),
                  pl.BlockSpec((T, n_choice), lambda i: (i, 0)),
                  pl.BlockSpec((T, D), lambda i: (i, 0)),
                  pl.BlockSpec((1, D), lambda i: (0, 0)),
                  pl.BlockSpec((1, D), lambda i: (0, 0))],
        out_specs=pl.BlockSpec((T, D), lambda i: (i, 0)),
        out_shape=jax.ShapeDtypeStruct((N, D), jnp.float32),
        scratch_shapes=[pltpu.VMEM((2, n_choice, T, D), jnp.float32), pltpu.SemaphoreType.DMA((2,))],
        compiler_params=pltpu.CompilerParams(dimension_semantics=("arbitrary",), vmem_limit_bytes=VMEM_LIMIT),
        name="moe_combine",
    )(dest2, dest2, ys, w, x, ln_g, ln_b)


def _t5_bucket(dist):
    n = jnp.maximum(dist, 0)
    max_exact = NUM_BUCKETS // 2
    nf = jnp.maximum(n, 1).astype(jnp.float32)
    large = max_exact + (jnp.log(nf / max_exact) / math.log(MAX_DISTANCE / max_exact)
                         * (NUM_BUCKETS - max_exact)).astype(jnp.int32)
    large = jnp.minimum(large, NUM_BUCKETS - 1)
    return jnp.where(n < max_exact, n, large)


def _bias_from_buckets(rel_bias, bucket):
    rb = rel_bias.astype(jnp.float32)
    shape = (N_KV_GROUPS, HEADS_PER_GROUP) + (1,) * bucket.ndim
    out = jnp.zeros((N_KV_GROUPS, HEADS_PER_GROUP) + bucket.shape, jnp.float32)
    for b in range(NUM_BUCKETS):
        out = jnp.where(bucket == b, rb[b].reshape(shape), out)
    return out


def _compress(t, pe, w1, w2):
    B, S, G, Dh = t.shape
    rep = CMP_BLOCK // CMP_STRIDE
    nc = S // CMP_STRIDE - rep + 1
    sub = t.reshape(B, S // CMP_STRIDE, CMP_STRIDE, G, Dh)
    blk = jnp.concatenate([sub[:, j:j + nc] for j in range(rep)], axis=2)
    blk = blk + pe[:, None, :]
    blk = blk.transpose(0, 1, 3, 2, 4).reshape(B, nc, G, CMP_BLOCK * Dh)
    out = jax.nn.gelu(blk @ w1) @ w2
    return out.transpose(0, 2, 1, 3)


def _nsa_attention(zt, gates_t, k_cmp, v_cmp, pe_k, w1_k, w2_k, pe_v, w1_v, w2_v, tables):
    bias_cmp_t, sel_tiles, win_tiles, c2s_t, onehot_t, n_cmp = tables
    B, S, _ = k_cmp.shape
    G, Dh = N_KV_GROUPS, HEAD_NSA
    bf16 = jnp.bfloat16
    ncp = bias_cmp_t.shape[2]
    kc = _compress(k_cmp.reshape(B, S, G, Dh), pe_k, w1_k, w2_k)
    vc = _compress(v_cmp.reshape(B, S, G, Dh), pe_v, w1_v, w2_v)
    pad = ((0, 0), (0, 0), (0, ncp - n_cmp), (0, 0))
    kc_p = jnp.pad(kc, pad).astype(bf16)
    vc_t = jnp.pad(vc, pad).transpose(0, 1, 3, 2).astype(bf16)
    n_sel = min(N_SELECT, S // SEL_BLOCK)
    o_t = _nsa_fused(zt, gates_t, kc_p, vc_t, bias_cmp_t, sel_tiles, win_tiles, c2s_t, onehot_t,
                     n_sel, n_cmp)
    return o_t.reshape(B, NSA_Q_WIDTH, S)


def _split_in_proj(w_in):
    D = w_in.shape[0]
    scale = HEAD_NSA ** -0.5
    assert math.frexp(scale)[0] == 0.5
    o = np.cumsum((SHIFT_WIDTH,) + REST_SPLITS)
    col = lambda a, b: w_in[:, o[a]:o[b]]
    row_part = [w_in[:, :SHIFT_WIDTH], col(1, 3)]
    lead = SHIFT_WIDTH + int(o[3] - o[1])
    lead_pad = (-lead) % D_MODEL
    w_row = jnp.concatenate(row_part + [jnp.zeros((D, lead_pad), w_in.dtype), col(8, 10)], axis=1)
    gate_blk = (lead + lead_pad) // D_MODEL
    assert w_row.shape[1] % IN_PROJ_TN == 0
    G, n_g = N_KV_GROUPS, 3 * HEADS_PER_GROUP
    gates = col(7, 8).reshape(D, G, n_g)
    gates = jnp.pad(gates, ((0, 0), (0, 0), (0, NSA_GATE_ROWS - n_g))).reshape(D, G * NSA_GATE_ROWS)
    w_t = jnp.concatenate([col(0, 1) * scale, col(3, 7), gates], axis=1).T
    return w_row.astype(jnp.bfloat16), w_t.astype(jnp.bfloat16), gate_blk, int(o[7] - o[3]) + NSA_Q_WIDTH


def _token_mixer(x, w_in, shift_mu, rw_w0, rw_w2, rw_a0, rw_a2, rw_g2, rw_kk, rw_ka, rw_rk,
                 rw_ln_g, rw_ln_b, cmp_pe_k, cmp_w1_k, cmp_w2_k, cmp_pe_v, cmp_w1_v, cmp_w2_v,
                 w_up_rwkv, w_up_nsa, w_out, ln_g, ln_b, rel_bias, pos):
    B, S, D = x.shape
    xf = x.reshape(B * S, D)
    w_row, w_t, gate_blk, n_main = _split_in_proj(w_in)
    z = _matmul(xf, w_row, IN_PROJ_TM, IN_PROJ_TN)
    zt, gates_t = _matmul_t(xf, w_t, n_main, B, S)
    z3 = z.reshape(B, S, -1)
    k_cmp = z3[..., SHIFT_WIDTH:SHIFT_WIDTH + NSA_KV_WIDTH]
    v_cmp = z3[..., SHIFT_WIDTH + NSA_KV_WIDTH:SHIFT_WIDTH + 2 * NSA_KV_WIDTH]
    y_rw = _rwkv_fused(z, B, S, shift_mu, rw_w0, rw_w2, rw_a0, rw_a2, rw_g2,
                       rw_kk, rw_ka, rw_rk, rw_ln_g, rw_ln_b)
    y_nsa_t = _nsa_attention(zt, gates_t, k_cmp, v_cmp, cmp_pe_k, cmp_w1_k, cmp_w2_k,
                             cmp_pe_v, cmp_w1_v, cmp_w2_v, pos)
    bf16 = jnp.bfloat16
    return _merge_out(y_rw, y_nsa_t, z, gate_blk, xf,
                      w_up_rwkv.astype(bf16), w_up_nsa.astype(bf16), w_out.astype(bf16),
                      ln_g, ln_b).reshape(B, S, D)


def _hier_moe(x, wg, bg, we, be, w1, w3, w2, layer, ln_g, ln_b):
    B, S, D = x.shape
    N = B * S
    f32 = jnp.float32
    xf = x.reshape(N, D)
    g_prob = jax.nn.softmax((xf @ wg + bg).astype(f32), axis=-1)
    grp = jnp.argmax(g_prob, axis=-1)
    p_grp = jnp.take_along_axis(g_prob, grp[:, None], axis=1)[:, 0]
    e_logits = (xf @ we + be).astype(f32).reshape(N, N_GROUPS, EXPERTS_PER_GROUP)
    e_logits = jnp.take_along_axis(e_logits, grp[:, None, None], axis=1)[:, 0]
    top_v, top_i = lax.top_k(e_logits, TOP_K_INNER)
    top_w = jax.nn.softmax(top_v, axis=-1) * p_grp[:, None]
    eid = (grp[:, None] * EXPERTS_PER_GROUP + top_i).astype(jnp.int32)
    n_slots = N * TOP_K_INNER
    n_rows = n_slots + N_EXPERTS * MOE_BLOCK
    n_blk = n_rows // MOE_BLOCK
    dest, counts = _moe_dest(eid.T.reshape(1, n_slots), N_EXPERTS, MOE_BLOCK)
    dest2 = dest.reshape(TOP_K_INNER, N)
    pcounts = (counts + MOE_BLOCK - 1) // MOE_BLOCK * MOE_BLOCK
    pends = jnp.cumsum(pcounts)
    blk_exp = jnp.minimum(jnp.sum(jnp.arange(n_blk)[:, None] * MOE_BLOCK >= pends[None, :], axis=1),
                          N_EXPERTS - 1).astype(jnp.int32)
    xs = _moe_dispatch(xf, dest2, n_rows)
    n_used = (pends[-1:] // MOE_BLOCK).astype(jnp.int32)
    ys = _moe_experts(xs, blk_exp, n_used, w1, w3, w2, layer, MOE_BLOCK)
    return _moe_combine(ys, dest2, top_w, xf, ln_g, ln_b).reshape(B, S, D)


def kernel(x, rel_bias, w_in, shift_mu, rw_w0, rw_w2, rw_a0, rw_a2, rw_g2, rw_kk, rw_ka, rw_rk,
           rw_ln_g, rw_ln_b, cmp_pe_k, cmp_w1_k, cmp_w2_k, cmp_pe_v, cmp_w1_v, cmp_w2_v,
           w_up_rwkv, w_up_nsa, w_out, ln1_g, ln1_b, router_group_w, router_group_b,
           router_expert_w, router_expert_b, exp_w1, exp_w3, exp_w2, ln2_g, ln2_b):
    pos = _nsa_tables(rel_bias, x.shape[1])
    for l in range(DEPTH):
        x = _token_mixer(x, w_in[l], shift_mu[l], rw_w0[l], rw_w2[l], rw_a0[l], rw_a2[l], rw_g2[l],
                         rw_kk[l], rw_ka[l], rw_rk[l], rw_ln_g[l], rw_ln_b[l],
                         cmp_pe_k[l], cmp_w1_k[l], cmp_w2_k[l], cmp_pe_v[l], cmp_w1_v[l], cmp_w2_v[l],
                         w_up_rwkv[l], w_up_nsa[l], w_out[l], ln1_g[l], ln1_b[l], rel_bias, pos)
        x = _hier_moe(x, router_group_w[l], router_group_b[l], router_expert_w[l], router_expert_b[l],
                      exp_w1, exp_w3, exp_w2, l, ln2_g[l], ln2_b[l])
    return x
```
